```python
import math
import jax, jax.numpy as jnp
from jax import lax
import numpy as np

D_MODEL = 1024
BATCH = 2
SEQ = 8192
DEPTH = 4

N_A = DEPTH // 2
N_B = DEPTH - N_A
CONV_WIDTH = 31
N_HEADS = 8
HEAD_DIM = 64
V_DIM = 2 * HEAD_DIM
QK_WIDTH = 2 * N_HEADS * HEAD_DIM
V_WIDTH = N_HEADS * V_DIM
N_BUCKETS = 32
MAX_DISTANCE = 128
Q_BLOCK = 128
N_EXPERTS = 16
N_GROUPS = 4
EXPERTS_PER_GROUP = N_EXPERTS // N_GROUPS
TOP_K = 2
D_EXPERT = 512
ALPHA = (2.0 * DEPTH) ** 0.25
BETA = (8.0 * DEPTH) ** -0.25
LN_EPS = 1e-5

kernel_name = "yoco_conformer_diffattn_grouped_moe"


def layer_norm(x, g, b):
    xf = x.astype(jnp.float32)
    mu = jnp.mean(xf, -1, keepdims=True)
    var = jnp.mean(jnp.square(xf - mu), -1, keepdims=True)
    return ((xf - mu) * lax.rsqrt(var + LN_EPS) * g.astype(jnp.float32) + b.astype(jnp.float32)).astype(x.dtype)


def conformer_conv(x, w_pw1, b_pw1, w_dw, b_dw, g, b, w_pw2, b_pw2):
    h = x @ w_pw1 + b_pw1
    a, gate = jnp.split(h, 2, axis=-1)
    h = a * jax.nn.sigmoid(gate)
    h = lax.conv_general_dilated(
        h, w_dw[:, None, :].astype(h.dtype), window_strides=(1,),
        padding=[(CONV_WIDTH - 1, 0)],
        dimension_numbers=("NWC", "WIO", "NWC"),
        feature_group_count=D_MODEL) + b_dw
    h = jax.nn.silu(layer_norm(h, g, b))
    return h @ w_pw2 + b_pw2


def t5_bucket(n):
    n = jnp.maximum(n, 0)
    max_exact = N_BUCKETS // 2
    nf = jnp.maximum(n, 1).astype(jnp.float32)
    large = max_exact + (jnp.log(nf / max_exact) / math.log(MAX_DISTANCE / max_exact)
                         * (N_BUCKETS - max_exact)).astype(jnp.int32)
    large = jnp.minimum(large, N_BUCKETS - 1)
    return jnp.where(n < max_exact, n, large)


def diff_attention(x, k1, k2, v, w_q, lam_params, subln_g, w_o, rel_bias, layer_idx):
    bsz, seq, _ = x.shape
    n_blk = seq // Q_BLOCK
    lambda_init = 0.8 - 0.6 * math.exp(-0.3 * layer_idx)
    lp = lam_params.astype(jnp.float32)
    lam = jnp.exp(jnp.sum(lp[0] * lp[1])) - jnp.exp(jnp.sum(lp[2] * lp[3])) + lambda_init
    q = (x @ w_q).reshape(bsz, n_blk, Q_BLOCK, 2, N_HEADS, HEAD_DIM)
    q = jnp.transpose(q, (3, 1, 0, 4, 2, 5))
    scale = HEAD_DIM ** -0.5
    k_pos = jnp.arange(seq)

    def block(args):
        qb1, qb2, i = args
        q_pos = i * Q_BLOCK + jnp.arange(Q_BLOCK)
        rel = q_pos[:, None] - k_pos[None, :]
        bias = jnp.transpose(jnp.take(rel_bias, t5_bucket(rel), axis=0), (2, 0, 1)).astype(jnp.float32)
        causal = rel >= 0

        def probs(qb, k):
            s = jnp.einsum("bhqd,bhkd->bhqk", qb, k).astype(jnp.float32) * scale + bias
            return jax.nn.softmax(jnp.where(causal, s, -jnp.inf), axis=-1)

        a = probs(qb1, k1) - lam * probs(qb2, k2)
        return jnp.einsum("bhqk,bhkv->bhqv", a.astype(v.dtype), v)

    out = lax.map(block, (q[0], q[1], jnp.arange(n_blk)))
    out = jnp.transpose(out, (1, 0, 3, 2, 4)).reshape(bsz, seq, N_HEADS, V_DIM)
    of = out.astype(jnp.float32)
    of = of * lax.rsqrt(jnp.mean(jnp.square(of), -1, keepdims=True) + LN_EPS) * subln_g.astype(jnp.float32)
    of = of * (1.0 - lambda_init)
    return of.astype(x.dtype).reshape(bsz, seq, V_WIDTH) @ w_o


def grouped_moe(x, router_w, router_bias, w_gate, w_up, w_down):
    bsz, seq, d = x.shape
    x2 = x.reshape(-1, d)
    aff = jax.nn.sigmoid((x2 @ router_w).astype(jnp.float32))
    sel = (aff + router_bias.astype(jnp.float32)).reshape(-1, N_GROUPS, EXPERTS_PER_GROUP)
    grp_score = jnp.sum(lax.top_k(sel, 2)[0], -1)
    g_idx = jnp.argmax(grp_score, -1)
    in_grp = jnp.take_along_axis(sel, g_idx[:, None, None], axis=1)[:, 0]
    _, local = lax.top_k(in_grp, TOP_K)
    e_idx = g_idx[:, None] * EXPERTS_PER_GROUP + local
    w = jnp.take_along_axis(aff, e_idx, -1)
    w = w / jnp.sum(w, -1, keepdims=True)
    gates = jnp.sum(jax.nn.one_hot(e_idx, N_EXPERTS, dtype=jnp.float32) * w[..., None], axis=1)

    def step(acc, ew):
        g_e, wg_e, wu_e, wd_e = ew
        h = jax.nn.silu(x2 @ wg_e) * (x2 @ wu_e)
        return acc + (h @ wd_e) * g_e[:, None].astype(x2.dtype), None

    out, _ = lax.scan(step, jnp.zeros_like(x2), (gates.T, w_gate, w_up, w_down))
    return out.reshape(bsz, seq, d)


def setup_inputs(seed: int = 0) -> dict:
    key = jax.random.key(seed)
    ks = jax.random.split(key, 32)
    nrm = jax.random.normal
    D, f32 = D_MODEL, jnp.float32
    w_kv_k = nrm(ks[9], (D, QK_WIDTH), f32) * D ** -0.5
    w_kv_v = nrm(ks[10], (D, V_WIDTH), f32) * D ** -0.5 * BETA
    return {
        "x": nrm(ks[0], (BATCH, SEQ, D), f32),
        "a_w_pw1": nrm(ks[1], (N_A, D, 2 * D), f32) * D ** -0.5,
        "a_b_pw1": 0.02 * nrm(ks[2], (N_A, 2 * D), f32),
        "a_w_dw": nrm(ks[3], (N_A, CONV_WIDTH, D), f32) * CONV_WIDTH ** -0.5,
        "a_b_dw": 0.02 * nrm(ks[4], (N_A, D), f32),
        "a_ln_g": 1.0 + 0.02 * nrm(ks[5], (N_A, D), f32),
        "a_ln_b": 0.02 * nrm(ks[6], (N_A, D), f32),
        "a_w_pw2": nrm(ks[7], (N_A, D, D), f32) * D ** -0.5 * BETA,
        "a_b_pw2": 0.02 * nrm(ks[8], (N_A, D), f32),
        "w_kv": jnp.concatenate([w_kv_k, w_kv_v], axis=1),
        "b_w_q": nrm(ks[11], (N_B, D, QK_WIDTH), f32) * D ** -0.5,
        "b_lambda": 0.1 * nrm(ks[12], (N_B, 4, HEAD_DIM), f32),
        "b_subln_g": 1.0 + 0.02 * nrm(ks[13], (N_B, V_DIM), f32),
        "b_w_o": nrm(ks[14], (N_B, V_WIDTH, D), f32) * V_WIDTH ** -0.5 * BETA,
        "rel_bias": 0.5 * nrm(ks[15], (N_BUCKETS, N_HEADS), f32),
        "ln_mix_g": 1.0 + 0.02 * nrm(ks[16], (DEPTH, D), f32),
        "ln_mix_b": 0.02 * nrm(ks[17], (DEPTH, D), f32),
        "ln_ffn_g": 1.0 + 0.02 * nrm(ks[18], (DEPTH, D), f32),
        "ln_ffn_b": 0.02 * nrm(ks[19], (DEPTH, D), f32),
        "router_w": nrm(ks[20], (D, N_EXPERTS), f32) * D ** -0.5,
        "router_bias": 0.01 * nrm(ks[21], (N_EXPERTS,), f32),
        "moe_w_gate": nrm(ks[22], (DEPTH, N_EXPERTS, D, D_EXPERT), f32) * D ** -0.5,
        "moe_w_up": nrm(ks[23], (DEPTH, N_EXPERTS, D, D_EXPERT), f32) * D ** -0.5,
        "moe_w_down": nrm(ks[24], (DEPTH, N_EXPERTS, D_EXPERT, D), f32) * D_EXPERT ** -0.5 * BETA,
    }


def reference(x, a_w_pw1, a_b_pw1, a_w_dw, a_b_dw, a_ln_g, a_ln_b, a_w_pw2, a_b_pw2,
              w_kv, b_w_q, b_lambda, b_subln_g, b_w_o, rel_bias,
              ln_mix_g, ln_mix_b, ln_ffn_g, ln_ffn_b,
              router_w, router_bias, moe_w_gate, moe_w_up, moe_w_down):
    bsz, seq, _ = x.shape
    k1 = k2 = v = None
    for l in range(DEPTH):
        if l < N_A:
            mix = conformer_conv(x, a_w_pw1[l], a_b_pw1[l], a_w_dw[l], a_b_dw[l],
                                 a_ln_g[l], a_ln_b[l], a_w_pw2[l], a_b_pw2[l])
        else:
            if l == N_A:
                kv = x @ w_kv
                k = kv[..., :QK_WIDTH].reshape(bsz, seq, 2, N_HEADS, HEAD_DIM)
                k = jnp.transpose(k, (2, 0, 3, 1, 4))
                k1, k2 = k[0], k[1]
                v = jnp.transpose(kv[..., QK_WIDTH:].reshape(bsz, seq, N_HEADS, V_DIM), (0, 2, 1, 3))
            j = l - N_A
            mix = diff_attention(x, k1, k2, v, b_w_q[j], b_lambda[j], b_subln_g[j], b_w_o[j], rel_bias, l)
        x = layer_norm(ALPHA * x + mix, ln_mix_g[l], ln_mix_b[l])
        ffn = grouped_moe(x, router_w, router_bias, moe_w_gate[l], moe_w_up[l], moe_w_down[l])
        x = layer_norm(ALPHA * x + ffn, ln_ffn_g[l], ln_ffn_b[l])
    return x
```

```python
import functools
import math

import jax
import jax.numpy as jnp
from jax import lax
from jax.experimental import pallas as pl
from jax.experimental.pallas import tpu as pltpu

D_MODEL = 1024
BATCH = 2
SEQ = 8192
DEPTH = 4
N_A = DEPTH // 2
CONV_WIDTH = 31
N_HEADS = 8
HEAD_DIM = 64
V_DIM = 2 * HEAD_DIM
QK_WIDTH = 2 * N_HEADS * HEAD_DIM
V_WIDTH = N_HEADS * V_DIM
N_BUCKETS = 32
MAX_DISTANCE = 128
N_EXPERTS = 16
N_GROUPS = 4
EXPERTS_PER_GROUP = N_EXPERTS // N_GROUPS
D_EXPERT = 512
ALPHA = (2.0 * DEPTH) ** 0.25
LN_EPS = 1e-5

T = BATCH * SEQ
F32 = jnp.float32
BF16 = jnp.bfloat16
LOG2E = 1.4426950408889634
NEG_BIG = -1e30
LANE = 128
CONV_HALO = 32
N_PAIRS = N_HEADS // 2

TM_GLU = 512
TM_CONV = 256
TM_ROUTER = 1024
TM_MOE = 1024
TM_PROJ = 512
TQ = 512
TK = 512


def _params(sem, vmem_mb=None):
    kw = dict(dimension_semantics=sem)
    if vmem_mb is not None:
        kw["vmem_limit_bytes"] = vmem_mb * 1024 * 1024
    return pltpu.CompilerParams(**kw)


def _layer_norm(v, g, b):
    mu = jnp.mean(v, axis=-1, keepdims=True)
    d = v - mu
    var = jnp.mean(d * d, axis=-1, keepdims=True)
    return d * lax.rsqrt(var + LN_EPS) * g + b


def _sigmoid(v):
    return 1.0 / (1.0 + jnp.exp(-v))


def _glu_kernel(x_ref, w_ref, b_ref, o_ref):
    xb = x_ref[...].astype(BF16)
    a = jnp.dot(xb, w_ref[:, :D_MODEL], preferred_element_type=F32) + b_ref[:, :D_MODEL]
    gate = jnp.dot(xb, w_ref[:, D_MODEL:], preferred_element_type=F32) + b_ref[:, D_MODEL:]
    o_ref[...] = a * _sigmoid(gate)


def _glu(x, w, b):
    return pl.pallas_call(
        _glu_kernel,
        grid=(T // TM_GLU,),
        in_specs=[
            pl.BlockSpec((TM_GLU, D_MODEL), lambda i: (i, 0)),
            pl.BlockSpec((D_MODEL, 2 * D_MODEL), lambda i: (0, 0)),
            pl.BlockSpec((1, 2 * D_MODEL), lambda i: (0, 0)),
        ],
        out_specs=pl.BlockSpec((TM_GLU, D_MODEL), lambda i: (i, 0)),
        out_shape=jax.ShapeDtypeStruct((T, D_MODEL), F32),
        compiler_params=_params(("parallel",)),
        name="glu_front",
    )(x, w, b)


CONV_ROWS = 32
CONV_LANES = 256


def _conv_kernel(g_ref, halo_ref, x_ref, wdw_ref, bdw_ref, lng_ref, lnb_ref, w2_ref, b2_ref,
                 mg_ref, mb_ref, o_ref, buf_ref, cv_ref):
    i = pl.program_id(0)
    first = (i % (SEQ // TM_CONV)) == 0
    buf_ref[0:CONV_HALO, :] = jnp.where(first, 0.0, halo_ref[...])
    buf_ref[CONV_HALO:, :] = g_ref[...]
    base = CONV_HALO - (CONV_WIDTH - 1)
    for r in range(TM_CONV // CONV_ROWS):
        r0 = r * CONV_ROWS
        for c in range(D_MODEL // CONV_LANES):
            cs = slice(c * CONV_LANES, (c + 1) * CONV_LANES)
            acc = jnp.broadcast_to(bdw_ref[:, cs], (CONV_ROWS, CONV_LANES))
            for j in range(CONV_WIDTH):
                acc = acc + wdw_ref[j:j + 1, cs] * buf_ref[r0 + base + j:r0 + base + j + CONV_ROWS, cs]
            cv_ref[r0:r0 + CONV_ROWS, cs] = acc
    h = _layer_norm(cv_ref[...], lng_ref[...], lnb_ref[...])
    h = h * _sigmoid(h)
    mix = jnp.dot(h.astype(BF16), w2_ref[...], preferred_element_type=F32) + b2_ref[...]
    o_ref[...] = _layer_norm(ALPHA * x_ref[...] + mix, mg_ref[...], mb_ref[...])


def _conv_back(g, x, wdw, bdw, lng, lnb, w2, b2, mg, mb):
    row = lambda i: (i, 0)
    fixed = lambda i: (0, 0)
    vec = pl.BlockSpec((1, D_MODEL), fixed)
    halo_blocks = TM_CONV // CONV_HALO
    return pl.pallas_call(
        _conv_kernel,
        grid=(T // TM_CONV,),
        in_specs=[
            pl.BlockSpec((TM_CONV, D_MODEL), row),
            pl.BlockSpec((CONV_HALO, D_MODEL), lambda i: (jnp.maximum(i * halo_blocks - 1, 0), 0)),
            pl.BlockSpec((TM_CONV, D_MODEL), row),
            pl.BlockSpec((CONV_HALO, D_MODEL), fixed),
            vec, vec, vec,
            pl.BlockSpec((D_MODEL, D_MODEL), fixed),
            vec, vec, vec,
        ],
        out_specs=pl.BlockSpec((TM_CONV, D_MODEL), row),
        out_shape=jax.ShapeDtypeStruct((T, D_MODEL), F32),
        scratch_shapes=[
            pltpu.VMEM((TM_CONV + CONV_HALO, D_MODEL), F32),
            pltpu.VMEM((TM_CONV, D_MODEL), F32),
        ],
        compiler_params=_params(("parallel",)),
        name="conv_back",
    )(g, g, x, wdw, bdw, lng, lnb, w2, b2, mg, mb)


def _ranks_before(vals):
    n = len(vals)
    ranks = []
    for j in range(n):
        r = jnp.zeros_like(vals[j])
        for i in range(n):
            if i == j:
                continue
            before = (vals[i] > vals[j]) | ((vals[i] == vals[j]) & (i < j))
            r = r + before.astype(F32)
        ranks.append(r)
    return ranks


def _router_kernel(x_ref, wrt_ref, rb_ref, gt_ref):
    logits = lax.dot_general(wrt_ref[...], x_ref[...], (((1,), (1,)), ((), ())),
                             precision=lax.Precision.HIGHEST, preferred_element_type=F32)
    aff = _sigmoid(logits)
    sel = aff + rb_ref[...]
    aff_rows = [aff[e:e + 1, :] for e in range(N_EXPERTS)]
    sel_rows = [sel[e:e + 1, :] for e in range(N_EXPERTS)]
    in_top2 = []
    scores = []
    for g in range(N_GROUPS):
        members = sel_rows[g * EXPERTS_PER_GROUP:(g + 1) * EXPERTS_PER_GROUP]
        ranks = _ranks_before(members)
        top = [r < 2.0 for r in ranks]
        in_top2.extend(top)
        s = jnp.zeros_like(members[0])
        for v, t in zip(members, top):
            s = s + jnp.where(t, v, 0.0)
        scores.append(s)
    g_ranks = _ranks_before(scores)
    w_rows = []
    for e in range(N_EXPERTS):
        chosen = (g_ranks[e // EXPERTS_PER_GROUP] < 1.0) & in_top2[e]
        w_rows.append(jnp.where(chosen, aff_rows[e], 0.0))
    denom = w_rows[0]
    for e in range(1, N_EXPERTS):
        denom = denom + w_rows[e]
    for e in range(N_EXPERTS):
        gt_ref[e:e + 1, :] = w_rows[e] / denom


def _router(x, wrt, rb):
    return pl.pallas_call(
        _router_kernel,
        grid=(T // TM_ROUTER,),
        in_specs=[
            pl.BlockSpec((TM_ROUTER, D_MODEL), lambda i: (i, 0)),
            pl.BlockSpec((N_EXPERTS, D_MODEL), lambda i: (0, 0)),
            pl.BlockSpec((N_EXPERTS, 1), lambda i: (0, 0)),
        ],
        out_specs=pl.BlockSpec((N_EXPERTS, TM_ROUTER), lambda i: (0, i)),
        out_shape=jax.ShapeDtypeStruct((N_EXPERTS, T), F32),
        compiler_params=_params(("parallel",)),
        name="router",
    )(x, wrt, rb)


def _moe_kernel(x_ref, g_ref, wg_ref, wu_ref, wd_ref, lg_ref, lb_ref, o_ref, xb_ref, acc_ref):
    e = pl.program_id(1)

    @pl.when(e == 0)
    def _():
        xb_ref[...] = x_ref[...].astype(BF16)
        acc_ref[...] = jnp.zeros_like(acc_ref)

    xb = xb_ref[...]
    hg = jnp.dot(xb, wg_ref[0], preferred_element_type=F32)
    hu = jnp.dot(xb, wu_ref[0], preferred_element_type=F32)
    lane = lax.broadcasted_iota(jnp.int32, (1, N_EXPERTS), 1)
    gate = jnp.sum(jnp.where(lane == e, g_ref[...], 0.0), axis=1, keepdims=True)
    h = hg * _sigmoid(hg) * hu * gate
    acc_ref[...] += jnp.dot(h.astype(BF16), wd_ref[0], preferred_element_type=F32)

    @pl.when(e == N_EXPERTS - 1)
    def _():
        o_ref[...] = _layer_norm(ALPHA * x_ref[...] + acc_ref[...], lg_ref[...], lb_ref[...])


def _moe(x, gates, wg, wu, wd, lg, lb):
    vec = pl.BlockSpec((1, D_MODEL), lambda i, e: (0, 0))
    return pl.pallas_call(
        _moe_kernel,
        grid=(T // TM_MOE, N_EXPERTS),
        in_specs=[
            pl.BlockSpec((TM_MOE, D_MODEL), lambda i, e: (i, 0)),
            pl.BlockSpec((TM_MOE, N_EXPERTS), lambda i, e: (i, 0)),
            pl.BlockSpec((1, D_MODEL, D_EXPERT), lambda i, e: (e, 0, 0)),
            pl.BlockSpec((1, D_MODEL, D_EXPERT), lambda i, e: (e, 0, 0)),
            pl.BlockSpec((1, D_EXPERT, D_MODEL), lambda i, e: (e, 0, 0)),
            vec, vec,
        ],
        out_specs=pl.BlockSpec((TM_MOE, D_MODEL), lambda i, e: (i, 0)),
        out_shape=jax.ShapeDtypeStruct((T, D_MODEL), F32),
        scratch_shapes=[
            pltpu.VMEM((TM_MOE, D_MODEL), BF16),
            pltpu.VMEM((TM_MOE, D_MODEL), F32),
        ],
        compiler_params=_params(("parallel", "arbitrary"), vmem_mb=56),
        name="moe_dense",
    )(x, gates, wg, wu, wd, lg, lb)


def _proj_kernel(x_ref, w_ref, o_ref, *, scale):
    y = jnp.dot(x_ref[...].astype(BF16), w_ref[...], preferred_element_type=F32)
    if scale != 1.0:
        y = y * scale
    o_ref[...] = y.astype(BF16)


def _proj(x, w, scale=1.0):
    n = w.shape[1]
    return pl.pallas_call(
        functools.partial(_proj_kernel, scale=scale),
        grid=(T // TM_PROJ,),
        in_specs=[
            pl.BlockSpec((TM_PROJ, D_MODEL), lambda i: (i, 0)),
            pl.BlockSpec((D_MODEL, n), lambda i: (0, 0)),
        ],
        out_specs=pl.BlockSpec((TM_PROJ, n), lambda i: (i, 0)),
        out_shape=jax.ShapeDtypeStruct((T, n), BF16),
        compiler_params=_params(("parallel",)),
        name="proj",
    )(x, w)


def _proj_ln_kernel(a_ref, x_ref, w_ref, g_ref, b_ref, o_ref):
    mix = jnp.dot(a_ref[...], w_ref[...], preferred_element_type=F32)
    o_ref[...] = _layer_norm(ALPHA * x_ref[...] + mix, g_ref[...], b_ref[...])


def _proj_ln(a, x, w, g, b):
    vec = pl.BlockSpec((1, D_MODEL), lambda i: (0, 0))
    return pl.pallas_call(
        _proj_ln_kernel,
        grid=(T // TM_PROJ,),
        in_specs=[
            pl.BlockSpec((TM_PROJ, V_WIDTH), lambda i: (i, 0)),
            pl.BlockSpec((TM_PROJ, D_MODEL), lambda i: (i, 0)),
            pl.BlockSpec((V_WIDTH, D_MODEL), lambda i: (0, 0)),
            vec, vec,
        ],
        out_specs=pl.BlockSpec((TM_PROJ, D_MODEL), lambda i: (i, 0)),
        out_shape=jax.ShapeDtypeStruct((T, D_MODEL), F32),
        compiler_params=_params(("parallel",)),
        name="proj_ln",
    )(a, x, w, g, b)


def _bias_kernel(rb_ref, o_ref):
    h = pl.program_id(0)
    max_exact = N_BUCKETS // 2
    far = rb_ref[N_BUCKETS - 1, h]
    r = lax.broadcasted_iota(jnp.int32, (TQ, TK), 0)
    c = lax.broadcasted_iota(jnp.int32, (TQ, TK), 1)
    for kind in range(2):
        rel = kind * TK + r - c
        n = jnp.maximum(rel, 0)
        nf = jnp.maximum(n, 1).astype(F32)
        large = max_exact + (jnp.log(nf / max_exact) / math.log(MAX_DISTANCE / max_exact)
                             * (N_BUCKETS - max_exact)).astype(jnp.int32)
        large = jnp.minimum(large, N_BUCKETS - 1)
        bucket = jnp.where(n < max_exact, n, large)
        bias = jnp.zeros((TQ, TK), F32)
        for b in range(N_BUCKETS):
            bias = jnp.where(bucket == b, rb_ref[b, h] - far, bias)
        o_ref[0, kind] = jnp.where(rel >= 0, bias * LOG2E, NEG_BIG)


def _bias_tiles(rel_bias):
    return pl.pallas_call(
        _bias_kernel,
        grid=(N_HEADS,),
        in_specs=[pl.BlockSpec(memory_space=pltpu.SMEM)],
        out_specs=pl.BlockSpec((1, 2, TQ, TK), lambda h: (h, 0, 0, 0)),
        out_shape=jax.ShapeDtypeStruct((N_HEADS, 2, TQ, TK), F32),
        compiler_params=_params(("parallel",)),
        name="bias_tiles",
    )(rel_bias)


def _attn_kernel(q1_ref, q2_ref, k1_ref, k2_ref, v_ref, bias_ref, lam_ref, sg_ref, o_ref,
                 m_ref, l_ref, acc_ref, *, lambda_init):
    qi = pl.program_id(2)
    lane = lax.broadcasted_iota(jnp.int32, (1, LANE), 1)
    lo = lane < HEAD_DIM
    q1 = q1_ref[...]
    q2 = q2_ref[...]
    zero = jnp.zeros_like(q1)
    qs = [jnp.where(lo, q1, zero), jnp.where(lo, q2, zero),
          jnp.where(lo, zero, q1), jnp.where(lo, zero, q2)]

    m_ref[...] = jnp.full(m_ref.shape, NEG_BIG, F32)
    l_ref[...] = jnp.zeros(l_ref.shape, F32)
    acc_ref[...] = jnp.zeros(acc_ref.shape, F32)

    def tile(j, kind):
        k0 = pl.multiple_of(j * TK, TK)
        ks = [k1_ref[pl.ds(k0, TK), :], k2_ref[pl.ds(k0, TK), :]]
        for slot in range(4):
            head, comp = slot // 2, slot % 2
            s = lax.dot_general(qs[slot], ks[comp], (((1,), (1,)), ((), ())),
                                preferred_element_type=F32)
            if kind is not None:
                s = s + bias_ref[head, kind]
            m_prev = m_ref[slot]
            m_new = jnp.maximum(m_prev, jnp.max(s, axis=1, keepdims=True))
            alpha = jnp.exp2(m_prev - m_new)
            p = jnp.exp2(s - m_new)
            l_ref[slot] = alpha * l_ref[slot] + jnp.sum(p, axis=1, keepdims=True)
            v = v_ref[pl.ds(k0, TK), head * V_DIM:(head + 1) * V_DIM]
            acc_ref[slot] = alpha * acc_ref[slot] + jnp.dot(p.astype(BF16), v,
                                                            preferred_element_type=F32)
            m_ref[slot] = m_new

    def far_body(j, carry):
        tile(j, None)
        return carry

    lax.fori_loop(0, jnp.maximum(qi - 1, 0), far_body, 0)

    @pl.when(qi >= 1)
    def _():
        tile(qi - 1, 1)

    tile(qi, 0)

    lp = lam_ref[...]
    lam = (jnp.exp(jnp.sum(lp[0:1] * lp[1:2], axis=1, keepdims=True))
           - jnp.exp(jnp.sum(lp[2:3] * lp[3:4], axis=1, keepdims=True)) + lambda_init)
    for head in range(2):
        a1 = acc_ref[2 * head] / l_ref[2 * head]
        a2 = acc_ref[2 * head + 1] / l_ref[2 * head + 1]
        of = a1 - lam * a2
        of = of * lax.rsqrt(jnp.mean(of * of, axis=-1, keepdims=True) + LN_EPS) * sg_ref[...]
        of = of * (1.0 - lambda_init)
        o_ref[:, head * V_DIM:(head + 1) * V_DIM] = of.astype(o_ref.dtype)


def _attention(q, kv, bias, lam_params, subln_g, lambda_init):
    nq = SEQ // TQ
    qk_blocks = QK_WIDTH // 2 // LANE
    return pl.pallas_call(
        functools.partial(_attn_kernel, lambda_init=lambda_init),
        grid=(BATCH, N_PAIRS, nq),
        in_specs=[
            pl.BlockSpec((TQ, LANE), lambda b, p, i: (b * nq + i, p)),
            pl.BlockSpec((TQ, LANE), lambda b, p, i: (b * nq + i, qk_blocks + p)),
            pl.BlockSpec((SEQ, LANE), lambda b, p, i: (b, p)),
            pl.BlockSpec((SEQ, LANE), lambda b, p, i: (b, qk_blocks + p)),
            pl.BlockSpec((SEQ, 2 * V_DIM), lambda b, p, i: (b, QK_WIDTH // (2 * V_DIM) + p)),
            pl.BlockSpec((2, 2, TQ, TK), lambda b, p, i: (p, 0, 0, 0)),
            pl.BlockSpec((4, HEAD_DIM), lambda b, p, i: (0, 0)),
            pl.BlockSpec((1, V_DIM), lambda b, p, i: (0, 0)),
        ],
        out_specs=pl.BlockSpec((TQ, 2 * V_DIM), lambda b, p, i: (b * nq + i, p)),
        out_shape=jax.ShapeDtypeStruct((T, V_WIDTH), BF16),
        scratch_shapes=[
            pltpu.VMEM((4, TQ, 1), F32),
            pltpu.VMEM((4, TQ, 1), F32),
            pltpu.VMEM((4, TQ, V_DIM), F32),
        ],
        compiler_params=_params(("parallel", "parallel", "arbitrary"), vmem_mb=56),
        name="diff_attention",
    )(q, q, kv, kv, kv, bias, lam_params, subln_g)


def kernel(x, a_w_pw1, a_b_pw1, a_w_dw, a_b_dw, a_ln_g, a_ln_b, a_w_pw2, a_b_pw2, w_kv, b_w_q,
           b_lambda, b_subln_g, b_w_o, rel_bias, ln_mix_g, ln_mix_b, ln_ffn_g, ln_ffn_b,
           router_w, router_bias, moe_w_gate, moe_w_up, moe_w_down):
    x = x.reshape(T, D_MODEL)
    row = lambda v: v.reshape(1, -1)
    wrt = router_w.T
    rb = router_bias.reshape(N_EXPERTS, 1)
    kv = None
    bias = None
    for l in range(DEPTH):
        if l < N_A:
            g = _glu(x, a_w_pw1[l].astype(BF16), row(a_b_pw1[l]))
            wdw = jnp.pad(a_w_dw[l], ((0, CONV_HALO - CONV_WIDTH), (0, 0)))
            x = _conv_back(g, x, wdw, row(a_b_dw[l]), row(a_ln_g[l]), row(a_ln_b[l]),
                           a_w_pw2[l].astype(BF16), row(a_b_pw2[l]),
                           row(ln_mix_g[l]), row(ln_mix_b[l]))
        else:
            if l == N_A:
                kv = _proj(x, w_kv.astype(BF16))
                bias = _bias_tiles(rel_bias)
            j = l - N_A
            lambda_init = 0.8 - 0.6 * math.exp(-0.3 * l)
            q = _proj(x, b_w_q[j].astype(BF16), scale=HEAD_DIM ** -0.5 * LOG2E)
            o = _attention(q, kv, bias, b_lambda[j], row(b_subln_g[j]), lambda_init)
            x = _proj_ln(o, x, b_w_o[j].astype(BF16), row(ln_mix_g[l]), row(ln_mix_b[l]))
        gates = _router(x, wrt, rb).T
        x = _moe(x, gates, moe_w_gate[l].astype(BF16), moe_w_up[l].astype(BF16),
                 moe_w_down[l].astype(BF16), row(ln_ffn_g[l]), row(ln_ffn_b[l]))
    return x.reshape(BATCH, SEQ, D_MODEL)
```

```python
import functools
import math

import jax
import jax.numpy as jnp
from jax import lax
from jax.experimental import pallas as pl
from jax.experimental.pallas import tpu as pltpu

D_MODEL = 1024
BATCH = 2
SEQ = 8192
DEPTH = 4
N_A = DEPTH // 2
CONV_WIDTH = 31
N_HEADS = 8
HEAD_DIM = 64
V_DIM = 2 * HEAD_DIM
QK_WIDTH = 2 * N_HEADS * HEAD_DIM
V_WIDTH = N_HEADS * V_DIM
N_BUCKETS = 32
MAX_DISTANCE = 128
N_EXPERTS = 16
N_GROUPS = 4
EXPERTS_PER_GROUP = N_EXPERTS // N_GROUPS
D_EXPERT = 512
ALPHA = (2.0 * DEPTH) ** 0.25
LN_EPS = 1e-5

T = BATCH * SEQ
F32 = jnp.float32
BF16 = jnp.bfloat16
LOG2E = 1.4426950408889634
NEG_BIG = -1e30
LANE = 128
CONV_HALO = 32
N_PAIRS = N_HEADS // 2

TM_GLU = 512
TM_CONV = 256
TM_ROUTER = 1024
TM_MOE = 1024
TM_PROJ = 512
TQ = 512
TK = 512


def _params(sem, vmem_mb=None):
    kw = dict(dimension_semantics=sem)
    if vmem_mb is not None:
        kw["vmem_limit_bytes"] = vmem_mb * 1024 * 1024
    return pltpu.CompilerParams(**kw)


def _layer_norm(v, g, b):
    mu = jnp.mean(v, axis=-1, keepdims=True)
    d = v - mu
    var = jnp.mean(d * d, axis=-1, keepdims=True)
    return d * lax.rsqrt(var + LN_EPS) * g + b


def _sigmoid(v):
    return 1.0 / (1.0 + jnp.exp(-v))


def _glu_kernel(x_ref, w_ref, b_ref, o_ref):
    xb = x_ref[...].astype(BF16)
    a = jnp.dot(xb, w_ref[:, :D_MODEL], preferred_element_type=F32) + b_ref[:, :D_MODEL]
    gate = jnp.dot(xb, w_ref[:, D_MODEL:], preferred_element_type=F32) + b_ref[:, D_MODEL:]
    o_ref[...] = a * _sigmoid(gate)


def _glu(x, w, b):
    return pl.pallas_call(
        _glu_kernel,
        grid=(T // TM_GLU,),
        in_specs=[
            pl.BlockSpec((TM_GLU, D_MODEL), lambda i: (i, 0)),
            pl.BlockSpec((D_MODEL, 2 * D_MODEL), lambda i: (0, 0)),
            pl.BlockSpec((1, 2 * D_MODEL), lambda i: (0, 0)),
        ],
        out_specs=pl.BlockSpec((TM_GLU, D_MODEL), lambda i: (i, 0)),
        out_shape=jax.ShapeDtypeStruct((T, D_MODEL), F32),
        compiler_params=_params(("parallel",)),
        name="glu_front",
    )(x, w, b)


CONV_ROWS = 32
CONV_LANES = 256


def _conv_kernel(g_ref, halo_ref, x_ref, wdw_ref, bdw_ref, lng_ref, lnb_ref, w2_ref, b2_ref,
                 mg_ref, mb_ref, o_ref, buf_ref, cv_ref):
    i = pl.program_id(0)
    first = (i % (SEQ // TM_CONV)) == 0
    buf_ref[0:CONV_HALO, :] = jnp.where(first, 0.0, halo_ref[...])
    buf_ref[CONV_HALO:, :] = g_ref[...]
    base = CONV_HALO - (CONV_WIDTH - 1)
    for r in range(TM_CONV // CONV_ROWS):
        r0 = r * CONV_ROWS
        for c in range(D_MODEL // CONV_LANES):
            cs = slice(c * CONV_LANES, (c + 1) * CONV_LANES)
            acc = jnp.broadcast_to(bdw_ref[:, cs], (CONV_ROWS, CONV_LANES))
            for j in range(CONV_WIDTH):
                acc = acc + wdw_ref[j:j + 1, cs] * buf_ref[r0 + base + j:r0 + base + j + CONV_ROWS, cs]
            cv_ref[r0:r0 + CONV_ROWS, cs] = acc
    h = _layer_norm(cv_ref[...], lng_ref[...], lnb_ref[...])
    h = h * _sigmoid(h)
    mix = jnp.dot(h.astype(BF16), w2_ref[...], preferred_element_type=F32) + b2_ref[...]
    o_ref[...] = _layer_norm(ALPHA * x_ref[...] + mix, mg_ref[...], mb_ref[...])


def _conv_back(g, x, wdw, bdw, lng, lnb, w2, b2, mg, mb):
    row = lambda i: (i, 0)
    fixed = lambda i: (0, 0)
    vec = pl.BlockSpec((1, D_MODEL), fixed)
    halo_blocks = TM_CONV // CONV_HALO
    return pl.pallas_call(
        _conv_kernel,
        grid=(T // TM_CONV,),
        in_specs=[
            pl.BlockSpec((TM_CONV, D_MODEL), row),
            pl.BlockSpec((CONV_HALO, D_MODEL), lambda i: (jnp.maximum(i * halo_blocks - 1, 0), 0)),
            pl.BlockSpec((TM_CONV, D_MODEL), row),
            pl.BlockSpec((CONV_HALO, D_MODEL), fixed),
            vec, vec, vec,
            pl.BlockSpec((D_MODEL, D_MODEL), fixed),
            vec, vec, vec,
        ],
        out_specs=pl.BlockSpec((TM_CONV, D_MODEL), row),
        out_shape=jax.ShapeDtypeStruct((T, D_MODEL), F32),
        scratch_shapes=[
            pltpu.VMEM((TM_CONV + CONV_HALO, D_MODEL), F32),
            pltpu.VMEM((TM_CONV, D_MODEL), F32),
        ],
        compiler_params=_params(("parallel",)),
        name="conv_back",
    )(g, g, x, wdw, bdw, lng, lnb, w2, b2, mg, mb)


def _ranks_before(vals):
    n = len(vals)
    ranks = []
    for j in range(n):
        r = jnp.zeros_like(vals[j])
        for i in range(n):
            if i == j:
                continue
            before = (vals[i] > vals[j]) | ((vals[i] == vals[j]) & (i < j))
            r = r + before.astype(F32)
        ranks.append(r)
    return ranks


def _router_kernel(x_ref, wrt_ref, rb_ref, gt_ref):
    logits = lax.dot_general(wrt_ref[...], x_ref[...], (((1,), (1,)), ((), ())),
                             precision=lax.Precision.HIGHEST, preferred_element_type=F32)
    aff = _sigmoid(logits)
    sel = aff + rb_ref[...]
    aff_rows = [aff[e:e + 1, :] for e in range(N_EXPERTS)]
    sel_rows = [sel[e:e + 1, :] for e in range(N_EXPERTS)]
    in_top2 = []
    scores = []
    for g in range(N_GROUPS):
        members = sel_rows[g * EXPERTS_PER_GROUP:(g + 1) * EXPERTS_PER_GROUP]
        ranks = _ranks_before(members)
        top = [r < 2.0 for r in ranks]
        in_top2.extend(top)
        s = jnp.zeros_like(members[0])
        for v, t in zip(members, top):
            s = s + jnp.where(t, v, 0.0)
        scores.append(s)
    g_ranks = _ranks_before(scores)
    w_rows = []
    for e in range(N_EXPERTS):
        chosen = (g_ranks[e // EXPERTS_PER_GROUP] < 1.0) & in_top2[e]
        w_rows.append(jnp.where(chosen, aff_rows[e], 0.0))
    denom = w_rows[0]
    for e in range(1, N_EXPERTS):
        denom = denom + w_rows[e]
    for e in range(N_EXPERTS):
        gt_ref[e:e + 1, :] = w_rows[e] / denom


def _router(x, wrt, rb):
    return pl.pallas_call(
        _router_kernel,
        grid=(T // TM_ROUTER,),
        in_specs=[
            pl.BlockSpec((TM_ROUTER, D_MODEL), lambda i: (i, 0)),
            pl.BlockSpec((N_EXPERTS, D_MODEL), lambda i: (0, 0)),
            pl.BlockSpec((N_EXPERTS, 1), lambda i: (0, 0)),
        ],
        out_specs=pl.BlockSpec((N_EXPERTS, TM_ROUTER), lambda i: (0, i)),
        out_shape=jax.ShapeDtypeStruct((N_EXPERTS, T), F32),
        compiler_params=_params(("parallel",)),
        name="router",
    )(x, wrt, rb)


def _moe_kernel(x_ref, g_ref, wg_ref, wu_ref, wd_ref, lg_ref, lb_ref, o_ref, xb_ref, acc_ref):
    e = pl.program_id(1)

    @pl.when(e == 0)
    def _():
        xb_ref[...] = x_ref[...].astype(BF16)
        acc_ref[...] = jnp.zeros_like(acc_ref)

    xb = xb_ref[...]
    hg = jnp.dot(xb, wg_ref[0], preferred_element_type=F32)
    hu = jnp.dot(xb, wu_ref[0], preferred_element_type=F32)
    lane = lax.broadcasted_iota(jnp.int32, (1, N_EXPERTS), 1)
    gate = jnp.sum(jnp.where(lane == e, g_ref[...], 0.0), axis=1, keepdims=True)
    h = hg * _sigmoid(hg) * hu * gate
    acc_ref[...] += jnp.dot(h.astype(BF16), wd_ref[0], preferred_element_type=F32)

    @pl.when(e == N_EXPERTS - 1)
    def _():
        o_ref[...] = _layer_norm(ALPHA * x_ref[...] + acc_ref[...], lg_ref[...], lb_ref[...])


def _moe(x, gates, wg, wu, wd, lg, lb):
    vec = pl.BlockSpec((1, D_MODEL), lambda i, e: (0, 0))
    return pl.pallas_call(
        _moe_kernel,
        grid=(T // TM_MOE, N_EXPERTS),
        in_specs=[
            pl.BlockSpec((TM_MOE, D_MODEL), lambda i, e: (i, 0)),
            pl.BlockSpec((TM_MOE, N_EXPERTS), lambda i, e: (i, 0)),
            pl.BlockSpec((1, D_MODEL, D_EXPERT), lambda i, e: (e, 0, 0)),
            pl.BlockSpec((1, D_MODEL, D_EXPERT), lambda i, e: (e, 0, 0)),
            pl.BlockSpec((1, D_EXPERT, D_MODEL), lambda i, e: (e, 0, 0)),
            vec, vec,
        ],
        out_specs=pl.BlockSpec((TM_MOE, D_MODEL), lambda i, e: (i, 0)),
        out_shape=jax.ShapeDtypeStruct((T, D_MODEL), F32),
        scratch_shapes=[
            pltpu.VMEM((TM_MOE, D_MODEL), BF16),
            pltpu.VMEM((TM_MOE, D_MODEL), F32),
        ],
        compiler_params=_params(("parallel", "arbitrary"), vmem_mb=56),
        name="moe_dense",
    )(x, gates, wg, wu, wd, lg, lb)


_NT = (((1,), (1,)), ((), ()))


def _q_proj_kernel(x_ref, wt_ref, o_ref, *, scale):
    y = lax.dot_general(wt_ref[...], x_ref[...].astype(BF16), _NT, preferred_element_type=F32)
    o_ref[...] = (y * scale).astype(BF16)


def _q_proj(x, wt, scale):
    return pl.pallas_call(
        functools.partial(_q_proj_kernel, scale=scale),
        grid=(T // TM_PROJ,),
        in_specs=[
            pl.BlockSpec((TM_PROJ, D_MODEL), lambda i: (i, 0)),
            pl.BlockSpec((QK_WIDTH, D_MODEL), lambda i: (0, 0)),
        ],
        out_specs=pl.BlockSpec((QK_WIDTH, TM_PROJ), lambda i: (0, i)),
        out_shape=jax.ShapeDtypeStruct((QK_WIDTH, T), BF16),
        compiler_params=_params(("parallel",)),
        name="q_proj",
    )(x, wt)


def _kv_proj_kernel(x_ref, wk_ref, wvt_ref, k_ref, vt_ref):
    xb = x_ref[...].astype(BF16)
    k_ref[...] = jnp.dot(xb, wk_ref[...], preferred_element_type=F32).astype(BF16)
    vt_ref[0] = lax.dot_general(wvt_ref[...], xb, _NT, preferred_element_type=F32).astype(BF16)


def _kv_proj(x, wk, wvt):
    return pl.pallas_call(
        _kv_proj_kernel,
        grid=(T // TK,),
        in_specs=[
            pl.BlockSpec((TK, D_MODEL), lambda i: (i, 0)),
            pl.BlockSpec((D_MODEL, QK_WIDTH), lambda i: (0, 0)),
            pl.BlockSpec((V_WIDTH, D_MODEL), lambda i: (0, 0)),
        ],
        out_specs=[
            pl.BlockSpec((TK, QK_WIDTH), lambda i: (i, 0)),
            pl.BlockSpec((1, V_WIDTH, TK), lambda i: (i, 0, 0)),
        ],
        out_shape=[
            jax.ShapeDtypeStruct((T, QK_WIDTH), BF16),
            jax.ShapeDtypeStruct((T // TK, V_WIDTH, TK), BF16),
        ],
        compiler_params=_params(("parallel",)),
        name="kv_proj",
    )(x, wk, wvt)


def _proj_ln_kernel(a_ref, x_ref, w_ref, g_ref, b_ref, o_ref):
    mix = jnp.dot(a_ref[...], w_ref[...], preferred_element_type=F32)
    o_ref[...] = _layer_norm(ALPHA * x_ref[...] + mix, g_ref[...], b_ref[...])


def _proj_ln(a, x, w, g, b):
    vec = pl.BlockSpec((1, D_MODEL), lambda i: (0, 0))
    return pl.pallas_call(
        _proj_ln_kernel,
        grid=(T // TM_PROJ,),
        in_specs=[
            pl.BlockSpec((TM_PROJ, V_WIDTH), lambda i: (i, 0)),
            pl.BlockSpec((TM_PROJ, D_MODEL), lambda i: (i, 0)),
            pl.BlockSpec((V_WIDTH, D_MODEL), lambda i: (0, 0)),
            vec, vec,
        ],
        out_specs=pl.BlockSpec((TM_PROJ, D_MODEL), lambda i: (i, 0)),
        out_shape=jax.ShapeDtypeStruct((T, D_MODEL), F32),
        compiler_params=_params(("parallel",)),
        name="proj_ln",
    )(a, x, w, g, b)


def _bias_kernel(rb_ref, o_ref):
    h = pl.program_id(0)
    max_exact = N_BUCKETS // 2
    far = rb_ref[N_BUCKETS - 1, h]
    c = lax.broadcasted_iota(jnp.int32, (TK, TQ), 0)
    r = lax.broadcasted_iota(jnp.int32, (TK, TQ), 1)
    for kind in range(2):
        rel = kind * TK + r - c
        n = jnp.maximum(rel, 0)
        nf = jnp.maximum(n, 1).astype(F32)
        large = max_exact + (jnp.log(nf / max_exact) / math.log(MAX_DISTANCE / max_exact)
                             * (N_BUCKETS - max_exact)).astype(jnp.int32)
        large = jnp.minimum(large, N_BUCKETS - 1)
        bucket = jnp.where(n < max_exact, n, large)
        bias = jnp.zeros((TK, TQ), F32)
        for b in range(N_BUCKETS):
            bias = jnp.where(bucket == b, rb_ref[b, h] - far, bias)
        o_ref[0, kind] = jnp.where(rel >= 0, bias * LOG2E, NEG_BIG)


def _bias_tiles(rel_bias):
    return pl.pallas_call(
        _bias_kernel,
        grid=(N_HEADS,),
        in_specs=[pl.BlockSpec(memory_space=pltpu.SMEM)],
        out_specs=pl.BlockSpec((1, 2, TK, TQ), lambda h: (h, 0, 0, 0)),
        out_shape=jax.ShapeDtypeStruct((N_HEADS, 2, TK, TQ), F32),
        compiler_params=_params(("parallel",)),
        name="bias_tiles",
    )(rel_bias)


def _attn_kernel(q1t_ref, q2t_ref, k1_ref, k2_ref, vt_ref, bias_ref, lam_ref, sg_ref, o_ref,
                 m_ref, l_ref, acc_ref, *, lambda_init):
    qi = pl.program_id(2)
    dim = lax.broadcasted_iota(jnp.int32, (LANE, 1), 0)
    lo = dim < HEAD_DIM
    q1 = q1t_ref[...]
    q2 = q2t_ref[...]
    zero = jnp.zeros_like(q1)
    qs = [jnp.where(lo, q1, zero), jnp.where(lo, q2, zero),
          jnp.where(lo, zero, q1), jnp.where(lo, zero, q2)]

    m_ref[...] = jnp.full(m_ref.shape, NEG_BIG, F32)
    l_ref[...] = jnp.zeros(l_ref.shape, F32)
    acc_ref[...] = jnp.zeros(acc_ref.shape, F32)

    def tile(j, kind):
        k0 = pl.multiple_of(j * TK, TK)
        ks = [k1_ref[pl.ds(k0, TK), :], k2_ref[pl.ds(k0, TK), :]]
        for slot in range(4):
            head, comp = slot // 2, slot % 2
            st = jnp.dot(ks[comp], qs[slot], preferred_element_type=F32)
            if kind is not None:
                st = st + bias_ref[head, kind]
            m_prev = m_ref[slot]
            m_new = jnp.maximum(m_prev, jnp.max(st, axis=0, keepdims=True))
            alpha = jnp.exp2(m_prev - m_new)
            p = jnp.exp2(st - m_new)
            l_ref[slot] = alpha * l_ref[slot] + jnp.sum(p, axis=0, keepdims=True)
            vt = vt_ref[j, head * V_DIM:(head + 1) * V_DIM, :]
            acc_ref[slot] = alpha * acc_ref[slot] + jnp.dot(vt, p.astype(BF16),
                                                            preferred_element_type=F32)
            m_ref[slot] = m_new

    def far_body(j, carry):
        tile(j, None)
        return carry

    lax.fori_loop(0, jnp.maximum(qi - 1, 0), far_body, 0)

    @pl.when(qi >= 1)
    def _():
        tile(qi - 1, 1)

    tile(qi, 0)

    lp = lam_ref[...]
    lam = (jnp.exp(jnp.sum(lp[0:1] * lp[1:2], axis=1, keepdims=True))
           - jnp.exp(jnp.sum(lp[2:3] * lp[3:4], axis=1, keepdims=True)) + lambda_init)
    for head in range(2):
        a1 = acc_ref[2 * head] / l_ref[2 * head]
        a2 = acc_ref[2 * head + 1] / l_ref[2 * head + 1]
        of = (a1 - lam * a2).T
        of = of * lax.rsqrt(jnp.mean(of * of, axis=-1, keepdims=True) + LN_EPS) * sg_ref[...]
        of = of * (1.0 - lambda_init)
        o_ref[:, head * V_DIM:(head + 1) * V_DIM] = of.astype(o_ref.dtype)


def _attention(qt, k, vt, bias, lam_params, subln_g, lambda_init):
    nq = SEQ // TQ
    nk = SEQ // TK
    qk_blocks = QK_WIDTH // 2 // LANE
    return pl.pallas_call(
        functools.partial(_attn_kernel, lambda_init=lambda_init),
        grid=(BATCH, N_PAIRS, nq),
        in_specs=[
            pl.BlockSpec((LANE, TQ), lambda b, p, i: (p, b * nq + i)),
            pl.BlockSpec((LANE, TQ), lambda b, p, i: (qk_blocks + p, b * nq + i)),
            pl.BlockSpec((SEQ, LANE), lambda b, p, i: (b, p)),
            pl.BlockSpec((SEQ, LANE), lambda b, p, i: (b, qk_blocks + p)),
            pl.BlockSpec((nk, 2 * V_DIM, TK), lambda b, p, i: (b, p, 0)),
            pl.BlockSpec((2, 2, TK, TQ), lambda b, p, i: (p, 0, 0, 0)),
            pl.BlockSpec((4, HEAD_DIM), lambda b, p, i: (0, 0)),
            pl.BlockSpec((1, V_DIM), lambda b, p, i: (0, 0)),
        ],
        out_specs=pl.BlockSpec((TQ, 2 * V_DIM), lambda b, p, i: (b * nq + i, p)),
        out_shape=jax.ShapeDtypeStruct((T, V_WIDTH), BF16),
        scratch_shapes=[
            pltpu.VMEM((4, 1, TQ), F32),
            pltpu.VMEM((4, 1, TQ), F32),
            pltpu.VMEM((4, V_DIM, TQ), F32),
        ],
        compiler_params=_params(("parallel", "parallel", "arbitrary"), vmem_mb=56),
        name="diff_attention",
    )(qt, qt, k, k, vt, bias, lam_params, subln_g)


def kernel(x, a_w_pw1, a_b_pw1, a_w_dw, a_b_dw, a_ln_g, a_ln_b, a_w_pw2, a_b_pw2, w_kv, b_w_q,
           b_lambda, b_subln_g, b_w_o, rel_bias, ln_mix_g, ln_mix_b, ln_ffn_g, ln_ffn_b,
           router_w, router_bias, moe_w_gate, moe_w_up, moe_w_down):
    x = x.reshape(T, D_MODEL)
    row = lambda v: v.reshape(1, -1)
    wrt = router_w.T
    rb = router_bias.reshape(N_EXPERTS, 1)
    k_all = vt_all = bias = None
    for l in range(DEPTH):
        if l < N_A:
            g = _glu(x, a_w_pw1[l].astype(BF16), row(a_b_pw1[l]))
            wdw = jnp.pad(a_w_dw[l], ((0, CONV_HALO - CONV_WIDTH), (0, 0)))
            x = _conv_back(g, x, wdw, row(a_b_dw[l]), row(a_ln_g[l]), row(a_ln_b[l]),
                           a_w_pw2[l].astype(BF16), row(a_b_pw2[l]),
                           row(ln_mix_g[l]), row(ln_mix_b[l]))
        else:
            if l == N_A:
                k_all, vt_all = _kv_proj(x, w_kv[:, :QK_WIDTH].astype(BF16),
                                         w_kv[:, QK_WIDTH:].T.astype(BF16))
                bias = _bias_tiles(rel_bias)
            j = l - N_A
            lambda_init = 0.8 - 0.6 * math.exp(-0.3 * l)
            qt = _q_proj(x, b_w_q[j].T.astype(BF16), HEAD_DIM ** -0.5 * LOG2E)
            o = _attention(qt, k_all, vt_all, bias, b_lambda[j], row(b_subln_g[j]), lambda_init)
            x = _proj_ln(o, x, b_w_o[j].astype(BF16), row(ln_mix_g[l]), row(ln_mix_b[l]))
        gates = _router(x, wrt, rb).T
        x = _moe(x, gates, moe_w_gate[l].astype(BF16), moe_w_up[l].astype(BF16),
                 moe_w_down[l].astype(BF16), row(ln_ffn_g[l]), row(ln_ffn_b[l]))
    return x.reshape(BATCH, SEQ, D_MODEL)
```

```python
import functools
import math

import jax
import jax.numpy as jnp
from jax import lax
from jax.experimental import pallas as pl
from jax.experimental.pallas import tpu as pltpu

D_MODEL = 1024
BATCH = 2
SEQ = 8192
DEPTH = 4
N_A = DEPTH // 2
CONV_WIDTH = 31
N_HEADS = 8
HEAD_DIM = 64
V_DIM = 2 * HEAD_DIM
QK_WIDTH = 2 * N_HEADS * HEAD_DIM
V_WIDTH = N_HEADS * V_DIM
N_BUCKETS = 32
MAX_DISTANCE = 128
N_EXPERTS = 16
N_GROUPS = 4
EXPERTS_PER_GROUP = N_EXPERTS // N_GROUPS
D_EXPERT = 512
ALPHA = (2.0 * DEPTH) ** 0.25
LN_EPS = 1e-5

T = BATCH * SEQ
F32 = jnp.float32
BF16 = jnp.bfloat16
LOG2E = 1.4426950408889634
NEG_BIG = -1e30
LANE = 128
CONV_HALO = 32
N_PAIRS = N_HEADS // 2

TM_GLU = 512
TM_CONV = 256
TM_ROUTER = 1024
TM_MOE = 1024
TM_PROJ = 512
TQ = 512
TK = 512


def _params(sem, vmem_mb=None):
    kw = dict(dimension_semantics=sem)
    if vmem_mb is not None:
        kw["vmem_limit_bytes"] = vmem_mb * 1024 * 1024
    return pltpu.CompilerParams(**kw)


def _layer_norm(v, g, b):
    mu = jnp.mean(v, axis=-1, keepdims=True)
    d = v - mu
    var = jnp.mean(d * d, axis=-1, keepdims=True)
    return d * lax.rsqrt(var + LN_EPS) * g + b


def _sigmoid(v):
    return 1.0 / (1.0 + jnp.exp(-v))


def _glu_kernel(x_ref, w_ref, b_ref, o_ref):
    xb = x_ref[...].astype(BF16)
    a = jnp.dot(xb, w_ref[:, :D_MODEL], preferred_element_type=F32) + b_ref[:, :D_MODEL]
    gate = jnp.dot(xb, w_ref[:, D_MODEL:], preferred_element_type=F32) + b_ref[:, D_MODEL:]
    o_ref[...] = a * _sigmoid(gate)


def _glu(x, w, b):
    return pl.pallas_call(
        _glu_kernel,
        grid=(T // TM_GLU,),
        in_specs=[
            pl.BlockSpec((TM_GLU, D_MODEL), lambda i: (i, 0)),
            pl.BlockSpec((D_MODEL, 2 * D_MODEL), lambda i: (0, 0)),
            pl.BlockSpec((1, 2 * D_MODEL), lambda i: (0, 0)),
        ],
        out_specs=pl.BlockSpec((TM_GLU, D_MODEL), lambda i: (i, 0)),
        out_shape=jax.ShapeDtypeStruct((T, D_MODEL), F32),
        compiler_params=_params(("parallel",)),
        name="glu_front",
    )(x, w, b)


CONV_ROWS = 32
CONV_LANES = 256


def _conv_kernel(g_ref, halo_ref, x_ref, wdw_ref, bdw_ref, lng_ref, lnb_ref, w2_ref, b2_ref,
                 mg_ref, mb_ref, o_ref, buf_ref, cv_ref):
    i = pl.program_id(0)
    first = (i % (SEQ // TM_CONV)) == 0
    buf_ref[0:CONV_HALO, :] = jnp.where(first, 0.0, halo_ref[...])
    buf_ref[CONV_HALO:, :] = g_ref[...]
    base = CONV_HALO - (CONV_WIDTH - 1)
    for r in range(TM_CONV // CONV_ROWS):
        r0 = r * CONV_ROWS
        for c in range(D_MODEL // CONV_LANES):
            cs = slice(c * CONV_LANES, (c + 1) * CONV_LANES)
            acc = jnp.broadcast_to(bdw_ref[:, cs], (CONV_ROWS, CONV_LANES))
            for j in range(CONV_WIDTH):
                acc = acc + wdw_ref[j:j + 1, cs] * buf_ref[r0 + base + j:r0 + base + j + CONV_ROWS, cs]
            cv_ref[r0:r0 + CONV_ROWS, cs] = acc
    h = _layer_norm(cv_ref[...], lng_ref[...], lnb_ref[...])
    h = h * _sigmoid(h)
    mix = jnp.dot(h.astype(BF16), w2_ref[...], preferred_element_type=F32) + b2_ref[...]
    o_ref[...] = _layer_norm(ALPHA * x_ref[...] + mix, mg_ref[...], mb_ref[...])


def _conv_back(g, x, wdw, bdw, lng, lnb, w2, b2, mg, mb):
    row = lambda i: (i, 0)
    fixed = lambda i: (0, 0)
    vec = pl.BlockSpec((1, D_MODEL), fixed)
    halo_blocks = TM_CONV // CONV_HALO
    return pl.pallas_call(
        _conv_kernel,
        grid=(T // TM_CONV,),
        in_specs=[
            pl.BlockSpec((TM_CONV, D_MODEL), row),
            pl.BlockSpec((CONV_HALO, D_MODEL), lambda i: (jnp.maximum(i * halo_blocks - 1, 0), 0)),
            pl.BlockSpec((TM_CONV, D_MODEL), row),
            pl.BlockSpec((CONV_HALO, D_MODEL), fixed),
            vec, vec, vec,
            pl.BlockSpec((D_MODEL, D_MODEL), fixed),
            vec, vec, vec,
        ],
        out_specs=pl.BlockSpec((TM_CONV, D_MODEL), row),
        out_shape=jax.ShapeDtypeStruct((T, D_MODEL), F32),
        scratch_shapes=[
            pltpu.VMEM((TM_CONV + CONV_HALO, D_MODEL), F32),
            pltpu.VMEM((TM_CONV, D_MODEL), F32),
        ],
        compiler_params=_params(("parallel",)),
        name="conv_back",
    )(g, g, x, wdw, bdw, lng, lnb, w2, b2, mg, mb)


def _ranks_before(vals):
    n = len(vals)
    ranks = []
    for j in range(n):
        r = jnp.zeros_like(vals[j])
        for i in range(n):
            if i == j:
                continue
            before = (vals[i] > vals[j]) | ((vals[i] == vals[j]) & (i < j))
            r = r + before.astype(F32)
        ranks.append(r)
    return ranks


PAIRS = [(a, b) for a in range(EXPERTS_PER_GROUP) for b in range(a + 1, EXPERTS_PER_GROUP)]
N_CLASSES = N_GROUPS * len(PAIRS)
CLASS_ROWS = 32
CLASS_EXPERTS = [(g * EXPERTS_PER_GROUP + a, g * EXPERTS_PER_GROUP + b)
                 for g in range(N_GROUPS) for (a, b) in PAIRS]
TM_G = 256
NT_MAX = -(-(T + N_CLASSES * (TM_G - 1)) // TM_G)
P_MAX = NT_MAX * TM_G
XG_W = D_MODEL + LANE
TD = 256
NT_MIN = T // TM_G
N_ZERO_TILES = N_CLASSES + NT_MAX - NT_MIN


def _route_kernel(x_ref, wrt_ref, rb_ref, xg_ref, cls_ref, pos_ref, tot_ref, tri_ref, carry_ref):
    i = pl.program_id(0)

    @pl.when(i == 0)
    def _():
        r = lax.broadcasted_iota(jnp.int32, (TM_ROUTER, TM_ROUTER), 0)
        c = lax.broadcasted_iota(jnp.int32, (TM_ROUTER, TM_ROUTER), 1)
        tri_ref[...] = jnp.where(r < c, 1.0, 0.0).astype(BF16)
        carry_ref[...] = jnp.zeros_like(carry_ref)

    x = x_ref[...]
    logits = lax.dot_general(wrt_ref[...], x, _NT, precision=lax.Precision.HIGHEST,
                             preferred_element_type=F32)
    aff = _sigmoid(logits)
    sel = aff + rb_ref[...]
    aff_rows = [aff[e:e + 1, :] for e in range(N_EXPERTS)]
    sel_rows = [sel[e:e + 1, :] for e in range(N_EXPERTS)]
    in_top2 = []
    scores = []
    for g in range(N_GROUPS):
        members = sel_rows[g * EXPERTS_PER_GROUP:(g + 1) * EXPERTS_PER_GROUP]
        ranks = _ranks_before(members)
        top = [r < 2.0 for r in ranks]
        in_top2.extend(top)
        s = jnp.zeros_like(members[0])
        for v, t in zip(members, top):
            s = s + jnp.where(t, v, 0.0)
        scores.append(s)
    g_ranks = _ranks_before(scores)
    w_rows = []
    for e in range(N_EXPERTS):
        chosen = (g_ranks[e // EXPERTS_PER_GROUP] < 1.0) & in_top2[e]
        w_rows.append(jnp.where(chosen, aff_rows[e], 0.0))
    denom = w_rows[0]
    for e in range(1, N_EXPERTS):
        denom = denom + w_rows[e]
    inv = 1.0 / denom

    masks = []
    wa = jnp.zeros_like(denom)
    wb = jnp.zeros_like(denom)
    for c, (ea, eb) in enumerate(CLASS_EXPERTS):
        m = (g_ranks[c // len(PAIRS)] < 1.0) & in_top2[ea] & in_top2[eb]
        masks.append(m.astype(F32))
        wa = wa + jnp.where(m, aff_rows[ea], 0.0)
        wb = wb + jnp.where(m, aff_rows[eb], 0.0)
    zero_row = jnp.zeros_like(denom)
    onehot = jnp.concatenate(masks + [zero_row] * (CLASS_ROWS - N_CLASSES), axis=0)

    before = jnp.dot(onehot.astype(BF16), tri_ref[...], preferred_element_type=F32)
    carry = carry_ref[...]
    class_id = lax.broadcasted_iota(jnp.int32, (CLASS_ROWS, 1), 0).astype(F32)
    pos_ref[...] = jnp.sum(onehot * (before + carry), axis=0, keepdims=True).astype(jnp.int32)
    cls_ref[...] = jnp.sum(onehot * class_id, axis=0, keepdims=True).astype(jnp.int32)
    carry = carry + jnp.sum(onehot, axis=1, keepdims=True)
    carry_ref[...] = carry
    tot_ref[...] = carry

    gates = jnp.concatenate([wa * inv, wb * inv] + [zero_row] * (LANE - 2), axis=0)
    xg_ref[:, :D_MODEL] = x
    xg_ref[:, D_MODEL:] = gates.T


def _route(x, wrt, rb):
    return pl.pallas_call(
        _route_kernel,
        grid=(T // TM_ROUTER,),
        in_specs=[
            pl.BlockSpec((TM_ROUTER, D_MODEL), lambda i: (i, 0)),
            pl.BlockSpec((N_EXPERTS, D_MODEL), lambda i: (0, 0)),
            pl.BlockSpec((N_EXPERTS, 1), lambda i: (0, 0)),
        ],
        out_specs=[
            pl.BlockSpec((TM_ROUTER, XG_W), lambda i: (i, 0)),
            pl.BlockSpec((1, TM_ROUTER), lambda i: (0, i)),
            pl.BlockSpec((1, TM_ROUTER), lambda i: (0, i)),
            pl.BlockSpec((CLASS_ROWS, 1), lambda i: (0, 0)),
        ],
        out_shape=[
            jax.ShapeDtypeStruct((T, XG_W), F32),
            jax.ShapeDtypeStruct((1, T), jnp.int32),
            jax.ShapeDtypeStruct((1, T), jnp.int32),
            jax.ShapeDtypeStruct((CLASS_ROWS, 1), F32),
        ],
        scratch_shapes=[
            pltpu.VMEM((TM_ROUTER, TM_ROUTER), BF16),
            pltpu.VMEM((CLASS_ROWS, 1), F32),
        ],
        compiler_params=_params(("arbitrary",)),
        name="route",
    )(x, wrt, rb)


def _row_copies(src_hbm, dst_hbm, dest_ref, base, sem):
    return [pltpu.make_async_copy(src_hbm.at[base + r], dst_hbm.at[dest_ref[base + r]], sem)
            for r in range(TD)]


def _dispatch_kernel(dest_ref, zstart_ref, xg_hbm, xs_hbm, zbuf, zsem, sem):
    i = pl.program_id(0)
    last = pl.num_programs(0) - 1

    @pl.when(i == 0)
    def _():
        zbuf[...] = jnp.zeros_like(zbuf)

        def zero_tile(c):
            start = pl.multiple_of(zstart_ref[c], TM_G)
            return pltpu.make_async_copy(zbuf, xs_hbm.at[pl.ds(start, TM_G)], zsem)

        for c in range(N_ZERO_TILES):
            @pl.when(zstart_ref[c] >= 0)
            def _():
                zero_tile(c).start()
        for c in range(N_ZERO_TILES):
            @pl.when(zstart_ref[c] >= 0)
            def _():
                zero_tile(c).wait()

    for parity in range(2):
        @pl.when(i % 2 == parity)
        def _():
            for cp in _row_copies(xg_hbm, xs_hbm, dest_ref, i * TD, sem.at[parity]):
                cp.start()

            @pl.when(i > 0)
            def _():
                for cp in _row_copies(xg_hbm, xs_hbm, dest_ref, (i - 1) * TD, sem.at[1 - parity]):
                    cp.wait()

            @pl.when(i == last)
            def _():
                for cp in _row_copies(xg_hbm, xs_hbm, dest_ref, i * TD, sem.at[parity]):
                    cp.wait()


def _dispatch(dest, zstart, xg):
    return pl.pallas_call(
        _dispatch_kernel,
        grid_spec=pltpu.PrefetchScalarGridSpec(
            num_scalar_prefetch=2,
            grid=(T // TD,),
            in_specs=[pl.BlockSpec(memory_space=pl.ANY)],
            out_specs=pl.BlockSpec(memory_space=pl.ANY),
            scratch_shapes=[
                pltpu.VMEM((TM_G, XG_W), F32),
                pltpu.SemaphoreType.DMA,
                pltpu.SemaphoreType.DMA((2,)),
            ],
        ),
        out_shape=jax.ShapeDtypeStruct((P_MAX, XG_W), F32),
        compiler_params=_params(("arbitrary",)),
        name="dispatch",
    )(dest, zstart, xg)


def _experts_kernel(te_ref, nt_ref, xs_ref, wg_ref, wu_ref, wd_ref, ys_ref, xb_ref, acc_ref):
    i = pl.program_id(0)
    j = pl.program_id(1)

    @pl.when(i < nt_ref[0])
    def _():
        @pl.when(j == 0)
        def _():
            xb_ref[...] = xs_ref[:, :D_MODEL].astype(BF16)

        xb = xb_ref[...]
        hg = jnp.dot(xb, wg_ref[0], preferred_element_type=F32)
        hu = jnp.dot(xb, wu_ref[0], preferred_element_type=F32)
        gates = xs_ref[:, D_MODEL:]
        gate = jnp.where((i + j) % 2 == 0, gates[:, 0:1], gates[:, 1:2])
        h = hg * _sigmoid(hg) * hu * gate
        y = jnp.dot(h.astype(BF16), wd_ref[0], preferred_element_type=F32)

        @pl.when(j == 0)
        def _():
            acc_ref[...] = y

        @pl.when(j == 1)
        def _():
            ys_ref[...] = acc_ref[...] + y

    @pl.when((i >= nt_ref[0]) & (j == 1))
    def _():
        ys_ref[...] = jnp.zeros_like(ys_ref)


def _experts(tile_e, n_tiles, xs, wg, wu, wd):
    def row_map(i, j, te, nt):
        return (jnp.minimum(i, nt[0] - 1), 0)

    def out_map(i, j, te, nt):
        return (i, 0)

    def w_map(i, j, te, nt):
        return (te[((i + j) % 2) * NT_MAX + i], 0, 0)

    return pl.pallas_call(
        _experts_kernel,
        grid_spec=pltpu.PrefetchScalarGridSpec(
            num_scalar_prefetch=2,
            grid=(NT_MAX, 2),
            in_specs=[
                pl.BlockSpec((TM_G, XG_W), row_map),
                pl.BlockSpec((1, D_MODEL, D_EXPERT), w_map),
                pl.BlockSpec((1, D_MODEL, D_EXPERT), w_map),
                pl.BlockSpec((1, D_EXPERT, D_MODEL), w_map),
            ],
            out_specs=pl.BlockSpec((TM_G, D_MODEL), out_map),
            scratch_shapes=[
                pltpu.VMEM((TM_G, D_MODEL), BF16),
                pltpu.VMEM((TM_G, D_MODEL), F32),
            ],
        ),
        out_shape=jax.ShapeDtypeStruct((P_MAX, D_MODEL), F32),
        compiler_params=_params(("arbitrary", "arbitrary")),
        name="experts",
    )(tile_e, n_tiles, xs, wg, wu, wd)


def _gather_copies(ys_hbm, buf, dest_ref, base, slot, sem):
    return [pltpu.make_async_copy(ys_hbm.at[dest_ref[base + r]], buf.at[slot, r], sem.at[slot])
            for r in range(TD)]


def _combine_kernel(dest_ref, x_ref, ys_hbm, lg_ref, lb_ref, o_ref, buf, sem):
    i = pl.program_id(0)
    last = pl.num_programs(0) - 1

    @pl.when(i == 0)
    def _():
        for cp in _gather_copies(ys_hbm, buf, dest_ref, 0, 0, sem):
            cp.start()

    for parity in range(2):
        @pl.when(i % 2 == parity)
        def _():
            @pl.when(i < last)
            def _():
                for cp in _gather_copies(ys_hbm, buf, dest_ref, (i + 1) * TD, 1 - parity, sem):
                    cp.start()

            for cp in _gather_copies(ys_hbm, buf, dest_ref, i * TD, parity, sem):
                cp.wait()
            o_ref[...] = _layer_norm(ALPHA * x_ref[...] + buf[parity], lg_ref[...], lb_ref[...])


def _combine(dest, x, ys, lg, lb):
    vec = pl.BlockSpec((1, D_MODEL), lambda i, d: (0, 0))
    return pl.pallas_call(
        _combine_kernel,
        grid_spec=pltpu.PrefetchScalarGridSpec(
            num_scalar_prefetch=1,
            grid=(T // TD,),
            in_specs=[
                pl.BlockSpec((TD, D_MODEL), lambda i, d: (i, 0)),
                pl.BlockSpec(memory_space=pl.ANY),
                vec, vec,
            ],
            out_specs=pl.BlockSpec((TD, D_MODEL), lambda i, d: (i, 0)),
            scratch_shapes=[
                pltpu.VMEM((2, TD, D_MODEL), F32),
                pltpu.SemaphoreType.DMA((2,)),
            ],
        ),
        out_shape=jax.ShapeDtypeStruct((T, D_MODEL), F32),
        compiler_params=_params(("arbitrary",)),
        name="combine",
    )(dest, x, ys, lg, lb)


def _moe(x, wrt, rb, wg, wu, wd, lg, lb):
    xg, cls, pos, tot = _route(x, wrt, rb)
    counts = tot[:N_CLASSES, 0].astype(jnp.int32)
    padded = (counts + TM_G - 1) // TM_G * TM_G
    ends = jnp.cumsum(padded)
    starts = ends - padded
    dest = starts[cls[0]] + pos[0]
    n_tiles = ends[-1] // TM_G
    tail = jnp.arange(NT_MIN, NT_MAX)
    zstart = jnp.concatenate([jnp.where(padded > 0, ends - TM_G, -1),
                              jnp.where(tail >= n_tiles, tail * TM_G, -1)])
    tile = jnp.minimum(jnp.arange(NT_MAX), n_tiles - 1)
    tile_cls = jnp.sum(tile[:, None] * TM_G >= ends[None, :], axis=1)
    pair = jnp.asarray(CLASS_EXPERTS, jnp.int32)[tile_cls]
    live = (jnp.arange(NT_MAX) < n_tiles)[:, None]
    resident = pair[n_tiles - 1, n_tiles % 2]
    tile_e = jnp.where(live, pair, resident).T.reshape(-1)
    xs = _dispatch(dest, zstart, xg)
    ys = _experts(tile_e, n_tiles.reshape(1), xs, wg, wu, wd)
    return _combine(dest, x, ys, lg, lb)


_NT = (((1,), (1,)), ((), ()))


def _q_proj_kernel(x_ref, wt_ref, o_ref, *, scale):
    y = lax.dot_general(wt_ref[...], x_ref[...].astype(BF16), _NT, preferred_element_type=F32)
    o_ref[...] = (y * scale).astype(BF16)


def _q_proj(x, wt, scale):
    return pl.pallas_call(
        functools.partial(_q_proj_kernel, scale=scale),
        grid=(T // TM_PROJ,),
        in_specs=[
            pl.BlockSpec((TM_PROJ, D_MODEL), lambda i: (i, 0)),
            pl.BlockSpec((QK_WIDTH, D_MODEL), lambda i: (0, 0)),
        ],
        out_specs=pl.BlockSpec((QK_WIDTH, TM_PROJ), lambda i: (0, i)),
        out_shape=jax.ShapeDtypeStruct((QK_WIDTH, T), BF16),
        compiler_params=_params(("parallel",)),
        name="q_proj",
    )(x, wt)


def _kv_proj_kernel(x_ref, wk_ref, wvt_ref, k_ref, vt_ref):
    xb = x_ref[...].astype(BF16)
    k_ref[...] = jnp.dot(xb, wk_ref[...], preferred_element_type=F32).astype(BF16)
    vt_ref[0] = lax.dot_general(wvt_ref[...], xb, _NT, preferred_element_type=F32).astype(BF16)


def _kv_proj(x, wk, wvt):
    return pl.pallas_call(
        _kv_proj_kernel,
        grid=(T // TK,),
        in_specs=[
            pl.BlockSpec((TK, D_MODEL), lambda i: (i, 0)),
            pl.BlockSpec((D_MODEL, QK_WIDTH), lambda i: (0, 0)),
            pl.BlockSpec((V_WIDTH, D_MODEL), lambda i: (0, 0)),
        ],
        out_specs=[
            pl.BlockSpec((TK, QK_WIDTH), lambda i: (i, 0)),
            pl.BlockSpec((1, V_WIDTH, TK), lambda i: (i, 0, 0)),
        ],
        out_shape=[
            jax.ShapeDtypeStruct((T, QK_WIDTH), BF16),
            jax.ShapeDtypeStruct((T // TK, V_WIDTH, TK), BF16),
        ],
        compiler_params=_params(("parallel",)),
        name="kv_proj",
    )(x, wk, wvt)


def _proj_ln_kernel(a_ref, x_ref, w_ref, g_ref, b_ref, o_ref):
    mix = jnp.dot(a_ref[...], w_ref[...], preferred_element_type=F32)
    o_ref[...] = _layer_norm(ALPHA * x_ref[...] + mix, g_ref[...], b_ref[...])


def _proj_ln(a, x, w, g, b):
    vec = pl.BlockSpec((1, D_MODEL), lambda i: (0, 0))
    return pl.pallas_call(
        _proj_ln_kernel,
        grid=(T // TM_PROJ,),
        in_specs=[
            pl.BlockSpec((TM_PROJ, V_WIDTH), lambda i: (i, 0)),
            pl.BlockSpec((TM_PROJ, D_MODEL), lambda i: (i, 0)),
            pl.BlockSpec((V_WIDTH, D_MODEL), lambda i: (0, 0)),
            vec, vec,
        ],
        out_specs=pl.BlockSpec((TM_PROJ, D_MODEL), lambda i: (i, 0)),
        out_shape=jax.ShapeDtypeStruct((T, D_MODEL), F32),
        compiler_params=_params(("parallel",)),
        name="proj_ln",
    )(a, x, w, g, b)


def _bias_kernel(rb_ref, o_ref):
    h = pl.program_id(0)
    max_exact = N_BUCKETS // 2
    far = rb_ref[N_BUCKETS - 1, h]
    c = lax.broadcasted_iota(jnp.int32, (TK, TQ), 0)
    r = lax.broadcasted_iota(jnp.int32, (TK, TQ), 1)
    for kind in range(2):
        rel = kind * TK + r - c
        n = jnp.maximum(rel, 0)
        nf = jnp.maximum(n, 1).astype(F32)
        large = max_exact + (jnp.log(nf / max_exact) / math.log(MAX_DISTANCE / max_exact)
                             * (N_BUCKETS - max_exact)).astype(jnp.int32)
        large = jnp.minimum(large, N_BUCKETS - 1)
        bucket = jnp.where(n < max_exact, n, large)
        bias = jnp.zeros((TK, TQ), F32)
        for b in range(N_BUCKETS):
            bias = jnp.where(bucket == b, rb_ref[b, h] - far, bias)
        o_ref[0, kind] = jnp.where(rel >= 0, bias * LOG2E, NEG_BIG)


def _bias_tiles(rel_bias):
    return pl.pallas_call(
        _bias_kernel,
        grid=(N_HEADS,),
        in_specs=[pl.BlockSpec(memory_space=pltpu.SMEM)],
        out_specs=pl.BlockSpec((1, 2, TK, TQ), lambda h: (h, 0, 0, 0)),
        out_shape=jax.ShapeDtypeStruct((N_HEADS, 2, TK, TQ), F32),
        compiler_params=_params(("parallel",)),
        name="bias_tiles",
    )(rel_bias)


def _attn_kernel(q1t_ref, q2t_ref, k1_ref, k2_ref, vt_ref, bias_ref, lam_ref, sg_ref, o_ref,
                 m_ref, l_ref, acc_ref, *, lambda_init):
    qi = pl.program_id(2)
    dim = lax.broadcasted_iota(jnp.int32, (LANE, 1), 0)
    lo = dim < HEAD_DIM
    q1 = q1t_ref[...]
    q2 = q2t_ref[...]
    zero = jnp.zeros_like(q1)
    qs = [jnp.where(lo, q1, zero), jnp.where(lo, q2, zero),
          jnp.where(lo, zero, q1), jnp.where(lo, zero, q2)]

    m_ref[...] = jnp.full(m_ref.shape, NEG_BIG, F32)
    l_ref[...] = jnp.zeros(l_ref.shape, F32)
    acc_ref[...] = jnp.zeros(acc_ref.shape, F32)

    def tile(j, kind):
        k0 = pl.multiple_of(j * TK, TK)
        ks = [k1_ref[pl.ds(k0, TK), :], k2_ref[pl.ds(k0, TK), :]]
        for slot in range(4):
            head, comp = slot // 2, slot % 2
            st = jnp.dot(ks[comp], qs[slot], preferred_element_type=F32)
            if kind is not None:
                st = st + bias_ref[head, kind]
            m_prev = m_ref[slot]
            m_new = jnp.maximum(m_prev, jnp.max(st, axis=0, keepdims=True))
            alpha = jnp.exp2(m_prev - m_new)
            p = jnp.exp2(st - m_new)
            l_ref[slot] = alpha * l_ref[slot] + jnp.sum(p, axis=0, keepdims=True)
            vt = vt_ref[j, head * V_DIM:(head + 1) * V_DIM, :]
            acc_ref[slot] = alpha * acc_ref[slot] + jnp.dot(vt, p.astype(BF16),
                                                            preferred_element_type=F32)
            m_ref[slot] = m_new

    def far_body(j, carry):
        tile(j, None)
        return carry

    lax.fori_loop(0, jnp.maximum(qi - 1, 0), far_body, 0)

    @pl.when(qi >= 1)
    def _():
        tile(qi - 1, 1)

    tile(qi, 0)

    lp = lam_ref[...]
    lam = (jnp.exp(jnp.sum(lp[0:1] * lp[1:2], axis=1, keepdims=True))
           - jnp.exp(jnp.sum(lp[2:3] * lp[3:4], axis=1, keepdims=True)) + lambda_init)
    for head in range(2):
        a1 = acc_ref[2 * head] / l_ref[2 * head]
        a2 = acc_ref[2 * head + 1] / l_ref[2 * head + 1]
        of = (a1 - lam * a2).T
        of = of * lax.rsqrt(jnp.mean(of * of, axis=-1, keepdims=True) + LN_EPS) * sg_ref[...]
        of = of * (1.0 - lambda_init)
        o_ref[:, head * V_DIM:(head + 1) * V_DIM] = of.astype(o_ref.dtype)


def _attention(qt, k, vt, bias, lam_params, subln_g, lambda_init):
    nq = SEQ // TQ
    nk = SEQ // TK
    qk_blocks = QK_WIDTH // 2 // LANE
    return pl.pallas_call(
        functools.partial(_attn_kernel, lambda_init=lambda_init),
        grid=(BATCH, N_PAIRS, nq),
        in_specs=[
            pl.BlockSpec((LANE, TQ), lambda b, p, i: (p, b * nq + i)),
            pl.BlockSpec((LANE, TQ), lambda b, p, i: (qk_blocks + p, b * nq + i)),
            pl.BlockSpec((SEQ, LANE), lambda b, p, i: (b, p)),
            pl.BlockSpec((SEQ, LANE), lambda b, p, i: (b, qk_blocks + p)),
            pl.BlockSpec((nk, 2 * V_DIM, TK), lambda b, p, i: (b, p, 0)),
            pl.BlockSpec((2, 2, TK, TQ), lambda b, p, i: (p, 0, 0, 0)),
            pl.BlockSpec((4, HEAD_DIM), lambda b, p, i: (0, 0)),
            pl.BlockSpec((1, V_DIM), lambda b, p, i: (0, 0)),
        ],
        out_specs=pl.BlockSpec((TQ, 2 * V_DIM), lambda b, p, i: (b * nq + i, p)),
        out_shape=jax.ShapeDtypeStruct((T, V_WIDTH), BF16),
        scratch_shapes=[
            pltpu.VMEM((4, 1, TQ), F32),
            pltpu.VMEM((4, 1, TQ), F32),
            pltpu.VMEM((4, V_DIM, TQ), F32),
        ],
        compiler_params=_params(("parallel", "parallel", "arbitrary"), vmem_mb=56),
        name="diff_attention",
    )(qt, qt, k, k, vt, bias, lam_params, subln_g)


def kernel(x, a_w_pw1, a_b_pw1, a_w_dw, a_b_dw, a_ln_g, a_ln_b, a_w_pw2, a_b_pw2, w_kv, b_w_q,
           b_lambda, b_subln_g, b_w_o, rel_bias, ln_mix_g, ln_mix_b, ln_ffn_g, ln_ffn_b,
           router_w, router_bias, moe_w_gate, moe_w_up, moe_w_down):
    x = x.reshape(T, D_MODEL)
    row = lambda v: v.reshape(1, -1)
    wrt = router_w.T
    rb = router_bias.reshape(N_EXPERTS, 1)
    k_all = vt_all = bias = None
    for l in range(DEPTH):
        if l < N_A:
            g = _glu(x, a_w_pw1[l].astype(BF16), row(a_b_pw1[l]))
            wdw = jnp.pad(a_w_dw[l], ((0, CONV_HALO - CONV_WIDTH), (0, 0)))
            x = _conv_back(g, x, wdw, row(a_b_dw[l]), row(a_ln_g[l]), row(a_ln_b[l]),
                           a_w_pw2[l].astype(BF16), row(a_b_pw2[l]),
                           row(ln_mix_g[l]), row(ln_mix_b[l]))
        else:
            if l == N_A:
                k_all, vt_all = _kv_proj(x, w_kv[:, :QK_WIDTH].astype(BF16),
                                         w_kv[:, QK_WIDTH:].T.astype(BF16))
                bias = _bias_tiles(rel_bias)
            j = l - N_A
            lambda_init = 0.8 - 0.6 * math.exp(-0.3 * l)
            qt = _q_proj(x, b_w_q[j].T.astype(BF16), HEAD_DIM ** -0.5 * LOG2E)
            o = _attention(qt, k_all, vt_all, bias, b_lambda[j], row(b_subln_g[j]), lambda_init)
            x = _proj_ln(o, x, b_w_o[j].astype(BF16), row(ln_mix_g[l]), row(ln_mix_b[l]))
        x = _moe(x, wrt, rb, moe_w_gate[l].astype(BF16), moe_w_up[l].astype(BF16),
                 moe_w_down[l].astype(BF16), row(ln_ffn_g[l]), row(ln_ffn_b[l]))
    return x.reshape(BATCH, SEQ, D_MODEL)
```

```python
import functools
import math

import jax
import jax.numpy as jnp
from jax import lax
from jax.experimental import pallas as pl
from jax.experimental.pallas import tpu as pltpu

D_MODEL = 1024
BATCH = 2
SEQ = 8192
DEPTH = 4
N_A = DEPTH // 2
CONV_WIDTH = 31
N_HEADS = 8
HEAD_DIM = 64
V_DIM = 2 * HEAD_DIM
QK_WIDTH = 2 * N_HEADS * HEAD_DIM
V_WIDTH = N_HEADS * V_DIM
N_BUCKETS = 32
MAX_DISTANCE = 128
N_EXPERTS = 16
N_GROUPS = 4
EXPERTS_PER_GROUP = N_EXPERTS // N_GROUPS
D_EXPERT = 512
ALPHA = (2.0 * DEPTH) ** 0.25
LN_EPS = 1e-5

T = BATCH * SEQ
F32 = jnp.float32
BF16 = jnp.bfloat16
LOG2E = 1.4426950408889634
NEG_BIG = -1e30
LANE = 128
CONV_HALO = 32
N_PAIRS = N_HEADS // 2

TM_GLU = 512
TM_CONV = 256
TM_ROUTER = 1024
TM_MOE = 1024
TM_PROJ = 512
TQ = 512
TK = 512


def _params(sem, vmem_mb=None, flags=None):
    kw = dict(dimension_semantics=sem)
    if vmem_mb is not None:
        kw["vmem_limit_bytes"] = vmem_mb * 1024 * 1024
    if flags is not None:
        kw["flags"] = flags
    return pltpu.CompilerParams(**kw)


def _layer_norm(v, g, b):
    mu = jnp.mean(v, axis=-1, keepdims=True)
    d = v - mu
    var = jnp.mean(d * d, axis=-1, keepdims=True)
    return d * lax.rsqrt(var + LN_EPS) * g + b


def _sigmoid(v):
    return 1.0 / (1.0 + jnp.exp(-v))


def _glu_kernel(x_ref, w_ref, b_ref, o_ref):
    xb = x_ref[...].astype(BF16)
    a = jnp.dot(xb, w_ref[:, :D_MODEL], preferred_element_type=F32) + b_ref[:, :D_MODEL]
    gate = jnp.dot(xb, w_ref[:, D_MODEL:], preferred_element_type=F32) + b_ref[:, D_MODEL:]
    o_ref[...] = a * _sigmoid(gate)


def _glu(x, w, b):
    return pl.pallas_call(
        _glu_kernel,
        grid=(T // TM_GLU,),
        in_specs=[
            pl.BlockSpec((TM_GLU, D_MODEL), lambda i: (i, 0)),
            pl.BlockSpec((D_MODEL, 2 * D_MODEL), lambda i: (0, 0)),
            pl.BlockSpec((1, 2 * D_MODEL), lambda i: (0, 0)),
        ],
        out_specs=pl.BlockSpec((TM_GLU, D_MODEL), lambda i: (i, 0)),
        out_shape=jax.ShapeDtypeStruct((T, D_MODEL), F32),
        compiler_params=_params(("parallel",)),
        name="glu_front",
    )(x, w, b)


CONV_ROWS = 32
CONV_LANES = 256


def _conv_kernel(g_ref, halo_ref, x_ref, wdw_ref, bdw_ref, lng_ref, lnb_ref, w2_ref, b2_ref,
                 mg_ref, mb_ref, o_ref, buf_ref, cv_ref):
    i = pl.program_id(0)
    first = (i % (SEQ // TM_CONV)) == 0
    buf_ref[0:CONV_HALO, :] = jnp.where(first, 0.0, halo_ref[...])
    buf_ref[CONV_HALO:, :] = g_ref[...]
    base = CONV_HALO - (CONV_WIDTH - 1)
    for r in range(TM_CONV // CONV_ROWS):
        r0 = r * CONV_ROWS
        for c in range(D_MODEL // CONV_LANES):
            cs = slice(c * CONV_LANES, (c + 1) * CONV_LANES)
            acc = jnp.broadcast_to(bdw_ref[:, cs], (CONV_ROWS, CONV_LANES))
            for j in range(CONV_WIDTH):
                acc = acc + wdw_ref[j:j + 1, cs] * buf_ref[r0 + base + j:r0 + base + j + CONV_ROWS, cs]
            cv_ref[r0:r0 + CONV_ROWS, cs] = acc
    h = _layer_norm(cv_ref[...], lng_ref[...], lnb_ref[...])
    h = h * _sigmoid(h)
    mix = jnp.dot(h.astype(BF16), w2_ref[...], preferred_element_type=F32) + b2_ref[...]
    o_ref[...] = _layer_norm(ALPHA * x_ref[...] + mix, mg_ref[...], mb_ref[...])


def _conv_back(g, x, wdw, bdw, lng, lnb, w2, b2, mg, mb):
    row = lambda i: (i, 0)
    fixed = lambda i: (0, 0)
    vec = pl.BlockSpec((1, D_MODEL), fixed)
    halo_blocks = TM_CONV // CONV_HALO
    return pl.pallas_call(
        _conv_kernel,
        grid=(T // TM_CONV,),
        in_specs=[
            pl.BlockSpec((TM_CONV, D_MODEL), row),
            pl.BlockSpec((CONV_HALO, D_MODEL), lambda i: (jnp.maximum(i * halo_blocks - 1, 0), 0)),
            pl.BlockSpec((TM_CONV, D_MODEL), row),
            pl.BlockSpec((CONV_HALO, D_MODEL), fixed),
            vec, vec, vec,
            pl.BlockSpec((D_MODEL, D_MODEL), fixed),
            vec, vec, vec,
        ],
        out_specs=pl.BlockSpec((TM_CONV, D_MODEL), row),
        out_shape=jax.ShapeDtypeStruct((T, D_MODEL), F32),
        scratch_shapes=[
            pltpu.VMEM((TM_CONV + CONV_HALO, D_MODEL), F32),
            pltpu.VMEM((TM_CONV, D_MODEL), F32),
        ],
        compiler_params=_params(("parallel",)),
        name="conv_back",
    )(g, g, x, wdw, bdw, lng, lnb, w2, b2, mg, mb)


def _ranks_before(vals):
    n = len(vals)
    ranks = []
    for j in range(n):
        r = jnp.zeros_like(vals[j])
        for i in range(n):
            if i == j:
                continue
            before = (vals[i] > vals[j]) | ((vals[i] == vals[j]) & (i < j))
            r = r + before.astype(F32)
        ranks.append(r)
    return ranks


PAIRS = [(a, b) for a in range(EXPERTS_PER_GROUP) for b in range(a + 1, EXPERTS_PER_GROUP)]
N_CLASSES = N_GROUPS * len(PAIRS)
CLASS_ROWS = 32
CLASS_EXPERTS = [(g * EXPERTS_PER_GROUP + a, g * EXPERTS_PER_GROUP + b)
                 for g in range(N_GROUPS) for (a, b) in PAIRS]
TM_G = 256
NT_MAX = -(-(T + N_CLASSES * (TM_G - 1)) // TM_G)
P_MAX = NT_MAX * TM_G
XG_W = D_MODEL + LANE
GATE_ROWS = 8
TD = 256
NT_MIN = T // TM_G
N_ZERO_TILES = N_CLASSES + NT_MAX - NT_MIN


def _route_kernel(x_ref, wrt_ref, rb_ref, gw_ref, cls_ref, pos_ref, tot_ref, tri_ref, carry_ref):
    i = pl.program_id(0)

    @pl.when(i == 0)
    def _():
        r = lax.broadcasted_iota(jnp.int32, (TM_ROUTER, TM_ROUTER), 0)
        c = lax.broadcasted_iota(jnp.int32, (TM_ROUTER, TM_ROUTER), 1)
        tri_ref[...] = jnp.where(r < c, 1.0, 0.0).astype(BF16)
        carry_ref[...] = jnp.zeros_like(carry_ref)

    x = x_ref[...]
    logits = lax.dot_general(wrt_ref[...], x, _NT, precision=lax.Precision.HIGHEST,
                             preferred_element_type=F32)
    aff = _sigmoid(logits)
    sel = aff + rb_ref[...]
    aff_rows = [aff[e:e + 1, :] for e in range(N_EXPERTS)]
    sel_rows = [sel[e:e + 1, :] for e in range(N_EXPERTS)]
    in_top2 = []
    scores = []
    for g in range(N_GROUPS):
        members = sel_rows[g * EXPERTS_PER_GROUP:(g + 1) * EXPERTS_PER_GROUP]
        ranks = _ranks_before(members)
        top = [r < 2.0 for r in ranks]
        in_top2.extend(top)
        s = jnp.zeros_like(members[0])
        for v, t in zip(members, top):
            s = s + jnp.where(t, v, 0.0)
        scores.append(s)
    g_ranks = _ranks_before(scores)
    w_rows = []
    for e in range(N_EXPERTS):
        chosen = (g_ranks[e // EXPERTS_PER_GROUP] < 1.0) & in_top2[e]
        w_rows.append(jnp.where(chosen, aff_rows[e], 0.0))
    denom = w_rows[0]
    for e in range(1, N_EXPERTS):
        denom = denom + w_rows[e]
    inv = 1.0 / denom

    masks = []
    wa = jnp.zeros_like(denom)
    wb = jnp.zeros_like(denom)
    for c, (ea, eb) in enumerate(CLASS_EXPERTS):
        m = (g_ranks[c // len(PAIRS)] < 1.0) & in_top2[ea] & in_top2[eb]
        masks.append(m.astype(F32))
        wa = wa + jnp.where(m, aff_rows[ea], 0.0)
        wb = wb + jnp.where(m, aff_rows[eb], 0.0)
    zero_row = jnp.zeros_like(denom)
    onehot = jnp.concatenate(masks + [zero_row] * (CLASS_ROWS - N_CLASSES), axis=0)

    before = jnp.dot(onehot.astype(BF16), tri_ref[...], preferred_element_type=F32)
    carry = carry_ref[...]
    class_id = lax.broadcasted_iota(jnp.int32, (CLASS_ROWS, 1), 0).astype(F32)
    pos_ref[...] = jnp.sum(onehot * (before + carry), axis=0, keepdims=True).astype(jnp.int32)
    cls_ref[...] = jnp.sum(onehot * class_id, axis=0, keepdims=True).astype(jnp.int32)
    carry = carry + jnp.sum(onehot, axis=1, keepdims=True)
    carry_ref[...] = carry
    tot_ref[...] = carry

    gw_ref[...] = jnp.concatenate([wa * inv, wb * inv] + [zero_row] * (GATE_ROWS - 2), axis=0)


def _route(x, wrt, rb):
    return pl.pallas_call(
        _route_kernel,
        grid=(T // TM_ROUTER,),
        in_specs=[
            pl.BlockSpec((TM_ROUTER, D_MODEL), lambda i: (i, 0)),
            pl.BlockSpec((N_EXPERTS, D_MODEL), lambda i: (0, 0)),
            pl.BlockSpec((N_EXPERTS, 1), lambda i: (0, 0)),
        ],
        out_specs=[
            pl.BlockSpec((GATE_ROWS, TM_ROUTER), lambda i: (0, i)),
            pl.BlockSpec((1, TM_ROUTER), lambda i: (0, i)),
            pl.BlockSpec((1, TM_ROUTER), lambda i: (0, i)),
            pl.BlockSpec((CLASS_ROWS, 1), lambda i: (0, 0)),
        ],
        out_shape=[
            jax.ShapeDtypeStruct((GATE_ROWS, T), F32),
            jax.ShapeDtypeStruct((1, T), jnp.int32),
            jax.ShapeDtypeStruct((1, T), jnp.int32),
            jax.ShapeDtypeStruct((CLASS_ROWS, 1), F32),
        ],
        scratch_shapes=[
            pltpu.VMEM((TM_ROUTER, TM_ROUTER), BF16),
            pltpu.VMEM((CLASS_ROWS, 1), F32),
        ],
        compiler_params=_params(("arbitrary",)),
        name="route",
    )(x, wrt, rb)


def _row_copies(pay, slot, dst_hbm, dest_ref, base, sem):
    return [pltpu.make_async_copy(pay.at[slot, r], dst_hbm.at[dest_ref[base + r]], sem.at[slot])
            for r in range(TD)]


def _dispatch_kernel(dest_ref, zstart_ref, x_ref, gw_ref, xs_hbm, pay, zbuf, zsem, sem):
    i = pl.program_id(0)
    last = pl.num_programs(0) - 1

    @pl.when(i == 0)
    def _():
        zbuf[...] = jnp.zeros_like(zbuf)

        def zero_tile(c):
            start = pl.multiple_of(zstart_ref[c], TM_G)
            return pltpu.make_async_copy(zbuf, xs_hbm.at[pl.ds(start, TM_G)], zsem)

        for c in range(N_ZERO_TILES):
            @pl.when(zstart_ref[c] >= 0)
            def _():
                zero_tile(c).start()
        for c in range(N_ZERO_TILES):
            @pl.when(zstart_ref[c] >= 0)
            def _():
                zero_tile(c).wait()

    for parity in range(2):
        @pl.when(i % 2 == parity)
        def _():
            pay[parity, :, :D_MODEL] = x_ref[...]
            gates = jnp.concatenate(
                [gw_ref[...], jnp.zeros((LANE - GATE_ROWS, TD), F32)], axis=0)
            pay[parity, :, D_MODEL:] = gates.T
            for cp in _row_copies(pay, parity, xs_hbm, dest_ref, i * TD, sem):
                cp.start()

            @pl.when(i > 0)
            def _():
                for cp in _row_copies(pay, 1 - parity, xs_hbm, dest_ref, (i - 1) * TD, sem):
                    cp.wait()

            @pl.when(i == last)
            def _():
                for cp in _row_copies(pay, parity, xs_hbm, dest_ref, i * TD, sem):
                    cp.wait()


def _dispatch(dest, zstart, x, gw):
    return pl.pallas_call(
        _dispatch_kernel,
        grid_spec=pltpu.PrefetchScalarGridSpec(
            num_scalar_prefetch=2,
            grid=(T // TD,),
            in_specs=[
                pl.BlockSpec((TD, D_MODEL), lambda i, d, z: (i, 0)),
                pl.BlockSpec((GATE_ROWS, TD), lambda i, d, z: (0, i)),
            ],
            out_specs=pl.BlockSpec(memory_space=pl.ANY),
            scratch_shapes=[
                pltpu.VMEM((2, TD, XG_W), F32),
                pltpu.VMEM((TM_G, XG_W), F32),
                pltpu.SemaphoreType.DMA,
                pltpu.SemaphoreType.DMA((2,)),
            ],
        ),
        out_shape=jax.ShapeDtypeStruct((P_MAX, XG_W), F32),
        compiler_params=_params(("arbitrary",)),
        name="dispatch",
    )(dest, zstart, x, gw)


def _experts_kernel(te_ref, nt_ref, xs_ref, wg_ref, wu_ref, wd_ref, ys_ref, xb_ref, acc_ref):
    i = pl.program_id(0)
    j = pl.program_id(1)

    @pl.when(i < nt_ref[0])
    def _():
        @pl.when(j == 0)
        def _():
            xb_ref[...] = xs_ref[:, :D_MODEL].astype(BF16)

        xb = xb_ref[...]
        hg = jnp.dot(xb, wg_ref[0], preferred_element_type=F32)
        hu = jnp.dot(xb, wu_ref[0], preferred_element_type=F32)
        gates = xs_ref[:, D_MODEL:]
        gate = jnp.where((i + j) % 2 == 0, gates[:, 0:1], gates[:, 1:2])
        h = hg * _sigmoid(hg) * hu * gate
        y = jnp.dot(h.astype(BF16), wd_ref[0], preferred_element_type=F32)

        @pl.when(j == 0)
        def _():
            acc_ref[...] = y

        @pl.when(j == 1)
        def _():
            ys_ref[...] = acc_ref[...] + y

    @pl.when((i >= nt_ref[0]) & (j == 1))
    def _():
        ys_ref[...] = jnp.zeros_like(ys_ref)


def _experts(tile_e, n_tiles, xs, wg, wu, wd):
    def row_map(i, j, te, nt):
        return (jnp.minimum(i, nt[0] - 1), 0)

    def out_map(i, j, te, nt):
        return (i, 0)

    def w_map(i, j, te, nt):
        return (te[((i + j) % 2) * NT_MAX + i], 0, 0)

    return pl.pallas_call(
        _experts_kernel,
        grid_spec=pltpu.PrefetchScalarGridSpec(
            num_scalar_prefetch=2,
            grid=(NT_MAX, 2),
            in_specs=[
                pl.BlockSpec((TM_G, XG_W), row_map),
                pl.BlockSpec((1, D_MODEL, D_EXPERT), w_map),
                pl.BlockSpec((1, D_MODEL, D_EXPERT), w_map),
                pl.BlockSpec((1, D_EXPERT, D_MODEL), w_map),
            ],
            out_specs=pl.BlockSpec((TM_G, D_MODEL), out_map),
            scratch_shapes=[
                pltpu.VMEM((TM_G, D_MODEL), BF16),
                pltpu.VMEM((TM_G, D_MODEL), F32),
            ],
        ),
        out_shape=jax.ShapeDtypeStruct((P_MAX, D_MODEL), F32),
        compiler_params=_params(("arbitrary", "arbitrary")),
        name="experts",
    )(tile_e, n_tiles, xs, wg, wu, wd)


def _gather_copies(ys_hbm, buf, dest_ref, base, slot, sem):
    return [pltpu.make_async_copy(ys_hbm.at[dest_ref[base + r]], buf.at[slot, r], sem.at[slot])
            for r in range(TD)]


def _combine_kernel(dest_ref, x_ref, ys_hbm, lg_ref, lb_ref, o_ref, buf, sem):
    i = pl.program_id(0)
    last = pl.num_programs(0) - 1

    @pl.when(i == 0)
    def _():
        for cp in _gather_copies(ys_hbm, buf, dest_ref, 0, 0, sem):
            cp.start()

    for parity in range(2):
        @pl.when(i % 2 == parity)
        def _():
            @pl.when(i < last)
            def _():
                for cp in _gather_copies(ys_hbm, buf, dest_ref, (i + 1) * TD, 1 - parity, sem):
                    cp.start()

            for cp in _gather_copies(ys_hbm, buf, dest_ref, i * TD, parity, sem):
                cp.wait()
            o_ref[...] = _layer_norm(ALPHA * x_ref[...] + buf[parity], lg_ref[...], lb_ref[...])


def _combine(dest, x, ys, lg, lb):
    vec = pl.BlockSpec((1, D_MODEL), lambda i, d: (0, 0))
    return pl.pallas_call(
        _combine_kernel,
        grid_spec=pltpu.PrefetchScalarGridSpec(
            num_scalar_prefetch=1,
            grid=(T // TD,),
            in_specs=[
                pl.BlockSpec((TD, D_MODEL), lambda i, d: (i, 0)),
                pl.BlockSpec(memory_space=pl.ANY),
                vec, vec,
            ],
            out_specs=pl.BlockSpec((TD, D_MODEL), lambda i, d: (i, 0)),
            scratch_shapes=[
                pltpu.VMEM((2, TD, D_MODEL), F32),
                pltpu.SemaphoreType.DMA((2,)),
            ],
        ),
        out_shape=jax.ShapeDtypeStruct((T, D_MODEL), F32),
        compiler_params=_params(("arbitrary",)),
        name="combine",
    )(dest, x, ys, lg, lb)


def _moe(x, wrt, rb, wg, wu, wd, lg, lb):
    gw, cls, pos, tot = _route(x, wrt, rb)
    counts = tot[:N_CLASSES, 0].astype(jnp.int32)
    padded = (counts + TM_G - 1) // TM_G * TM_G
    ends = jnp.cumsum(padded)
    starts = ends - padded
    dest = starts[cls[0]] + pos[0]
    n_tiles = ends[-1] // TM_G
    tail = jnp.arange(NT_MIN, NT_MAX)
    zstart = jnp.concatenate([jnp.where(padded > 0, ends - TM_G, -1),
                              jnp.where(tail >= n_tiles, tail * TM_G, -1)])
    tile = jnp.minimum(jnp.arange(NT_MAX), n_tiles - 1)
    tile_cls = jnp.sum(tile[:, None] * TM_G >= ends[None, :], axis=1)
    pair = jnp.asarray(CLASS_EXPERTS, jnp.int32)[tile_cls]
    live = (jnp.arange(NT_MAX) < n_tiles)[:, None]
    resident = pair[n_tiles - 1, n_tiles % 2]
    tile_e = jnp.where(live, pair, resident).T.reshape(-1)
    xs = _dispatch(dest, zstart, x, gw)
    ys = _experts(tile_e, n_tiles.reshape(1), xs, wg, wu, wd)
    return _combine(dest, x, ys, lg, lb)


_NT = (((1,), (1,)), ((), ()))


def _q_proj_kernel(x_ref, wt_ref, o_ref, *, scale):
    y = lax.dot_general(wt_ref[...], x_ref[...].astype(BF16), _NT, preferred_element_type=F32)
    o_ref[...] = (y * scale).astype(BF16)


def _q_proj(x, wt, scale):
    return pl.pallas_call(
        functools.partial(_q_proj_kernel, scale=scale),
        grid=(T // TM_PROJ,),
        in_specs=[
            pl.BlockSpec((TM_PROJ, D_MODEL), lambda i: (i, 0)),
            pl.BlockSpec((QK_WIDTH, D_MODEL), lambda i: (0, 0)),
        ],
        out_specs=pl.BlockSpec((QK_WIDTH, TM_PROJ), lambda i: (0, i)),
        out_shape=jax.ShapeDtypeStruct((QK_WIDTH, T), BF16),
        compiler_params=_params(("parallel",)),
        name="q_proj",
    )(x, wt)


def _kv_proj_kernel(x_ref, wk_ref, wvt_ref, k_ref, vt_ref):
    xb = x_ref[...].astype(BF16)
    k_ref[...] = jnp.dot(xb, wk_ref[...], preferred_element_type=F32).astype(BF16)
    vt_ref[0] = lax.dot_general(wvt_ref[...], xb, _NT, preferred_element_type=F32).astype(BF16)


def _kv_proj(x, wk, wvt):
    return pl.pallas_call(
        _kv_proj_kernel,
        grid=(T // TK,),
        in_specs=[
            pl.BlockSpec((TK, D_MODEL), lambda i: (i, 0)),
            pl.BlockSpec((D_MODEL, QK_WIDTH), lambda i: (0, 0)),
            pl.BlockSpec((V_WIDTH, D_MODEL), lambda i: (0, 0)),
        ],
        out_specs=[
            pl.BlockSpec((TK, QK_WIDTH), lambda i: (i, 0)),
            pl.BlockSpec((1, V_WIDTH, TK), lambda i: (i, 0, 0)),
        ],
        out_shape=[
            jax.ShapeDtypeStruct((T, QK_WIDTH), BF16),
            jax.ShapeDtypeStruct((T // TK, V_WIDTH, TK), BF16),
        ],
        compiler_params=_params(("parallel",)),
        name="kv_proj",
    )(x, wk, wvt)


def _proj_ln_kernel(a_ref, x_ref, w_ref, g_ref, b_ref, o_ref):
    mix = jnp.dot(a_ref[...], w_ref[...], preferred_element_type=F32)
    o_ref[...] = _layer_norm(ALPHA * x_ref[...] + mix, g_ref[...], b_ref[...])


def _proj_ln(a, x, w, g, b):
    vec = pl.BlockSpec((1, D_MODEL), lambda i: (0, 0))
    return pl.pallas_call(
        _proj_ln_kernel,
        grid=(T // TM_PROJ,),
        in_specs=[
            pl.BlockSpec((TM_PROJ, V_WIDTH), lambda i: (i, 0)),
            pl.BlockSpec((TM_PROJ, D_MODEL), lambda i: (i, 0)),
            pl.BlockSpec((V_WIDTH, D_MODEL), lambda i: (0, 0)),
            vec, vec,
        ],
        out_specs=pl.BlockSpec((TM_PROJ, D_MODEL), lambda i: (i, 0)),
        out_shape=jax.ShapeDtypeStruct((T, D_MODEL), F32),
        compiler_params=_params(("parallel",)),
        name="proj_ln",
    )(a, x, w, g, b)


def _bias_kernel(rb_ref, o_ref):
    h = pl.program_id(0)
    max_exact = N_BUCKETS // 2
    far = rb_ref[N_BUCKETS - 1, h]
    c = lax.broadcasted_iota(jnp.int32, (TK, TQ), 0)
    r = lax.broadcasted_iota(jnp.int32, (TK, TQ), 1)
    for kind in range(2):
        rel = kind * TK + r - c
        n = jnp.maximum(rel, 0)
        nf = jnp.maximum(n, 1).astype(F32)
        large = max_exact + (jnp.log(nf / max_exact) / math.log(MAX_DISTANCE / max_exact)
                             * (N_BUCKETS - max_exact)).astype(jnp.int32)
        large = jnp.minimum(large, N_BUCKETS - 1)
        bucket = jnp.where(n < max_exact, n, large)
        bias = jnp.zeros((TK, TQ), F32)
        for b in range(N_BUCKETS):
            bias = jnp.where(bucket == b, rb_ref[b, h] - far, bias)
        o_ref[0, kind] = jnp.where(rel >= 0, bias * LOG2E, NEG_BIG)


def _bias_tiles(rel_bias):
    return pl.pallas_call(
        _bias_kernel,
        grid=(N_HEADS,),
        in_specs=[pl.BlockSpec(memory_space=pltpu.SMEM)],
        out_specs=pl.BlockSpec((1, 2, TK, TQ), lambda h: (h, 0, 0, 0)),
        out_shape=jax.ShapeDtypeStruct((N_HEADS, 2, TK, TQ), F32),
        compiler_params=_params(("parallel",)),
        name="bias_tiles",
    )(rel_bias)


def _attn_kernel(q1t_ref, q2t_ref, k1_ref, k2_ref, vt_ref, bias_ref, lam_ref, sg_ref, o_ref,
                 s_ref, mx_ref, m_ref, l_ref, acc_ref, *, lambda_init):
    qi = pl.program_id(2)
    dim = lax.broadcasted_iota(jnp.int32, (LANE, 1), 0)
    lo = dim < HEAD_DIM
    q1 = q1t_ref[...]
    q2 = q2t_ref[...]
    zero = jnp.zeros_like(q1)
    qs = [jnp.where(lo, q1, zero), jnp.where(lo, q2, zero),
          jnp.where(lo, zero, q1), jnp.where(lo, zero, q2)]

    m_ref[...] = jnp.full(m_ref.shape, NEG_BIG, F32)
    l_ref[...] = jnp.zeros(l_ref.shape, F32)
    acc_ref[...] = jnp.zeros(acc_ref.shape, F32)

    def scores(j, slot):
        k0 = pl.multiple_of(j * TK, TK)
        k_ref = k1_ref if slot % 2 == 0 else k2_ref
        st = jnp.dot(k_ref[pl.ds(k0, TK), :], qs[slot], preferred_element_type=F32)
        s_ref[slot] = st
        mx_ref[slot] = jnp.max(st, axis=0, keepdims=True)

    def update(j, slot, kind):
        head = slot // 2
        st = s_ref[slot]
        if kind is None:
            mx = mx_ref[slot]
        else:
            st = st + bias_ref[head, kind]
            mx = jnp.max(st, axis=0, keepdims=True)
        m_prev = m_ref[slot]
        m_new = jnp.maximum(m_prev, mx)
        alpha = jnp.exp2(m_prev - m_new)
        p = jnp.exp2(st - m_new)
        l_ref[slot] = alpha * l_ref[slot] + jnp.sum(p, axis=0, keepdims=True)
        vt = vt_ref[j, head * V_DIM:(head + 1) * V_DIM, :]
        acc_ref[slot] = alpha * acc_ref[slot] + jnp.dot(vt, p.astype(BF16),
                                                        preferred_element_type=F32)
        m_ref[slot] = m_new

    def tile(j, kind, nxt):
        for slot in range(4):
            if slot < 3:
                scores(j, slot + 1)
            elif nxt is not None:
                scores(nxt, 0)
            update(j, slot, kind)

    scores(0, 0)

    def far_body(j, carry):
        tile(j, None, j + 1)
        return carry

    lax.fori_loop(0, jnp.maximum(qi - 1, 0), far_body, 0)

    @pl.when(qi >= 1)
    def _():
        tile(qi - 1, 1, qi)

    tile(qi, 0, None)

    lp = lam_ref[...]
    lam = (jnp.exp(jnp.sum(lp[0:1] * lp[1:2], axis=1, keepdims=True))
           - jnp.exp(jnp.sum(lp[2:3] * lp[3:4], axis=1, keepdims=True)) + lambda_init)
    for head in range(2):
        a1 = acc_ref[2 * head] / l_ref[2 * head]
        a2 = acc_ref[2 * head + 1] / l_ref[2 * head + 1]
        of = (a1 - lam * a2).T
        of = of * lax.rsqrt(jnp.mean(of * of, axis=-1, keepdims=True) + LN_EPS) * sg_ref[...]
        of = of * (1.0 - lambda_init)
        o_ref[:, head * V_DIM:(head + 1) * V_DIM] = of.astype(o_ref.dtype)


def _attention(qt, k, vt, bias, lam_params, subln_g, lambda_init):
    nq = SEQ // TQ
    nk = SEQ // TK
    qk_blocks = QK_WIDTH // 2 // LANE
    return pl.pallas_call(
        functools.partial(_attn_kernel, lambda_init=lambda_init),
        grid=(BATCH, N_PAIRS, nq),
        in_specs=[
            pl.BlockSpec((LANE, TQ), lambda b, p, i: (p, b * nq + i)),
            pl.BlockSpec((LANE, TQ), lambda b, p, i: (qk_blocks + p, b * nq + i)),
            pl.BlockSpec((SEQ, LANE), lambda b, p, i: (b, p)),
            pl.BlockSpec((SEQ, LANE), lambda b, p, i: (b, qk_blocks + p)),
            pl.BlockSpec((nk, 2 * V_DIM, TK), lambda b, p, i: (b, p, 0)),
            pl.BlockSpec((2, 2, TK, TQ), lambda b, p, i: (p, 0, 0, 0)),
            pl.BlockSpec((4, HEAD_DIM), lambda b, p, i: (0, 0)),
            pl.BlockSpec((1, V_DIM), lambda b, p, i: (0, 0)),
        ],
        out_specs=pl.BlockSpec((TQ, 2 * V_DIM), lambda b, p, i: (b * nq + i, p)),
        out_shape=jax.ShapeDtypeStruct((T, V_WIDTH), BF16),
        scratch_shapes=[
            pltpu.VMEM((4, TK, TQ), F32),
            pltpu.VMEM((4, 1, TQ), F32),
            pltpu.VMEM((4, 1, TQ), F32),
            pltpu.VMEM((4, 1, TQ), F32),
            pltpu.VMEM((4, V_DIM, TQ), F32),
        ],
        compiler_params=_params(("parallel", "parallel", "arbitrary"), vmem_mb=56),
        name="diff_attention",
    )(qt, qt, k, k, vt, bias, lam_params, subln_g)


def kernel(x, a_w_pw1, a_b_pw1, a_w_dw, a_b_dw, a_ln_g, a_ln_b, a_w_pw2, a_b_pw2, w_kv, b_w_q,
           b_lambda, b_subln_g, b_w_o, rel_bias, ln_mix_g, ln_mix_b, ln_ffn_g, ln_ffn_b,
           router_w, router_bias, moe_w_gate, moe_w_up, moe_w_down):
    x = x.reshape(T, D_MODEL)
    row = lambda v: v.reshape(1, -1)
    wrt = router_w.T
    rb = router_bias.reshape(N_EXPERTS, 1)
    k_all = vt_all = bias = None
    for l in range(DEPTH):
        if l < N_A:
            g = _glu(x, a_w_pw1[l].astype(BF16), row(a_b_pw1[l]))
            wdw = jnp.pad(a_w_dw[l], ((0, CONV_HALO - CONV_WIDTH), (0, 0)))
            x = _conv_back(g, x, wdw, row(a_b_dw[l]), row(a_ln_g[l]), row(a_ln_b[l]),
                           a_w_pw2[l].astype(BF16), row(a_b_pw2[l]),
                           row(ln_mix_g[l]), row(ln_mix_b[l]))
        else:
            if l == N_A:
                k_all, vt_all = _kv_proj(x, w_kv[:, :QK_WIDTH].astype(BF16),
                                         w_kv[:, QK_WIDTH:].T.astype(BF16))
                bias = _bias_tiles(rel_bias)
            j = l - N_A
            lambda_init = 0.8 - 0.6 * math.exp(-0.3 * l)
            qt = _q_proj(x, b_w_q[j].T.astype(BF16), HEAD_DIM ** -0.5 * LOG2E)
            o = _attention(qt, k_all, vt_all, bias, b_lambda[j], row(b_subln_g[j]), lambda_init)
            x = _proj_ln(o, x, b_w_o[j].astype(BF16), row(ln_mix_g[l]), row(ln_mix_b[l]))
        x = _moe(x, wrt, rb, moe_w_gate[l].astype(BF16), moe_w_up[l].astype(BF16),
                 moe_w_down[l].astype(BF16), row(ln_ffn_g[l]), row(ln_ffn_b[l]))
    return x.reshape(BATCH, SEQ, D_MODEL)
```

```python
import functools
import math

import jax
import jax.numpy as jnp
from jax import lax
from jax.experimental import pallas as pl
from jax.experimental.pallas import tpu as pltpu

D_MODEL = 1024
BATCH = 2
SEQ = 8192
DEPTH = 4
N_A = DEPTH // 2
CONV_WIDTH = 31
N_HEADS = 8
HEAD_DIM = 64
V_DIM = 2 * HEAD_DIM
QK_WIDTH = 2 * N_HEADS * HEAD_DIM
V_WIDTH = N_HEADS * V_DIM
N_BUCKETS = 32
MAX_DISTANCE = 128
N_EXPERTS = 16
N_GROUPS = 4
EXPERTS_PER_GROUP = N_EXPERTS // N_GROUPS
D_EXPERT = 512
ALPHA = (2.0 * DEPTH) ** 0.25
LN_EPS = 1e-5

T = BATCH * SEQ
F32 = jnp.float32
BF16 = jnp.bfloat16
LOG2E = 1.4426950408889634
NEG_BIG = -1e30
LANE = 128
CONV_HALO = 32
N_PAIRS = N_HEADS // 2

TM_GLU = 512
TM_CONV = 256
TM_ROUTER = 1024
TM_MOE = 1024
TM_PROJ = 512
TQ = 512
TK = 512


def _params(sem, vmem_mb=None, flags=None):
    kw = dict(dimension_semantics=sem)
    if vmem_mb is not None:
        kw["vmem_limit_bytes"] = vmem_mb * 1024 * 1024
    if flags is not None:
        kw["flags"] = flags
    return pltpu.CompilerParams(**kw)


def _layer_norm(v, g, b):
    mu = jnp.mean(v, axis=-1, keepdims=True)
    d = v - mu
    var = jnp.mean(d * d, axis=-1, keepdims=True)
    return d * lax.rsqrt(var + LN_EPS) * g + b


def _sigmoid(v):
    return 1.0 / (1.0 + jnp.exp(-v))


def _glu_kernel(x_ref, w_ref, b_ref, o_ref):
    xb = x_ref[...].astype(BF16)
    a = jnp.dot(xb, w_ref[:, :D_MODEL], preferred_element_type=F32) + b_ref[:, :D_MODEL]
    gate = jnp.dot(xb, w_ref[:, D_MODEL:], preferred_element_type=F32) + b_ref[:, D_MODEL:]
    o_ref[...] = a * _sigmoid(gate)


def _glu(x, w, b):
    return pl.pallas_call(
        _glu_kernel,
        grid=(T // TM_GLU,),
        in_specs=[
            pl.BlockSpec((TM_GLU, D_MODEL), lambda i: (i, 0)),
            pl.BlockSpec((D_MODEL, 2 * D_MODEL), lambda i: (0, 0)),
            pl.BlockSpec((1, 2 * D_MODEL), lambda i: (0, 0)),
        ],
        out_specs=pl.BlockSpec((TM_GLU, D_MODEL), lambda i: (i, 0)),
        out_shape=jax.ShapeDtypeStruct((T, D_MODEL), F32),
        compiler_params=_params(("parallel",)),
        name="glu_front",
    )(x, w, b)


SUBLANES = 8
CONV_BASE = CONV_HALO - (CONV_WIDTH - 1)


def _conv_kernel(g_ref, halo_ref, x_ref, wdw_ref, bdw_ref, lng_ref, lnb_ref, w2_ref, b2_ref,
                 mg_ref, mb_ref, o_ref, buf_ref, cv_ref):
    i = pl.program_id(0)
    first = (i % (SEQ // TM_CONV)) == 0
    buf_ref[0:CONV_HALO, :] = jnp.where(first, 0.0, halo_ref[...])
    buf_ref[CONV_HALO:CONV_HALO + TM_CONV, :] = g_ref[...]
    buf_ref[CONV_HALO + TM_CONV:, :] = jnp.zeros((SUBLANES, D_MODEL), F32)
    sub = lax.broadcasted_iota(jnp.int32, (SUBLANES, LANE), 0)
    n_q = (CONV_BASE + CONV_WIDTH - 1) // SUBLANES + 1
    for c in range(D_MODEL // LANE):
        cs = slice(c * LANE, (c + 1) * LANE)
        w_b = [jnp.broadcast_to(wdw_ref[j:j + 1, cs], (SUBLANES, LANE)) for j in range(CONV_WIDTH)]
        bias = jnp.broadcast_to(bdw_ref[:, cs], (SUBLANES, LANE))

        def partials(v):
            tiles = [buf_ref[pl.ds(v + SUBLANES * q, SUBLANES), cs] for q in range(n_q)]
            out = []
            for s in range(SUBLANES):
                a = None
                for q in range(n_q):
                    j = SUBLANES * q + s - CONV_BASE
                    if 0 <= j < CONV_WIDTH:
                        term = w_b[j] * tiles[q]
                        a = term if a is None else a + term
                out.append(a)
            return tuple(out)

        def row_body(t, prev):
            r = pl.multiple_of(t * SUBLANES, SUBLANES)
            cur = partials(pl.multiple_of(r + SUBLANES, SUBLANES))
            acc = bias + prev[0]
            for s in range(1, SUBLANES):
                mixed = jnp.where(sub >= s, prev[s], cur[s])
                acc = acc + pltpu.roll(mixed, SUBLANES - s, axis=0)
            cv_ref[pl.ds(r, SUBLANES), cs] = acc
            return cur

        lax.fori_loop(0, TM_CONV // SUBLANES, row_body, partials(0), unroll=2)
    h = _layer_norm(cv_ref[...], lng_ref[...], lnb_ref[...])
    h = h * _sigmoid(h)
    mix = jnp.dot(h.astype(BF16), w2_ref[...], preferred_element_type=F32) + b2_ref[...]
    o_ref[...] = _layer_norm(ALPHA * x_ref[...] + mix, mg_ref[...], mb_ref[...])


def _conv_back(g, x, wdw, bdw, lng, lnb, w2, b2, mg, mb):
    row = lambda i: (i, 0)
    fixed = lambda i: (0, 0)
    vec = pl.BlockSpec((1, D_MODEL), fixed)
    halo_blocks = TM_CONV // CONV_HALO
    return pl.pallas_call(
        _conv_kernel,
        grid=(T // TM_CONV,),
        in_specs=[
            pl.BlockSpec((TM_CONV, D_MODEL), row),
            pl.BlockSpec((CONV_HALO, D_MODEL), lambda i: (jnp.maximum(i * halo_blocks - 1, 0), 0)),
            pl.BlockSpec((TM_CONV, D_MODEL), row),
            pl.BlockSpec((CONV_HALO, D_MODEL), fixed),
            vec, vec, vec,
            pl.BlockSpec((D_MODEL, D_MODEL), fixed),
            vec, vec, vec,
        ],
        out_specs=pl.BlockSpec((TM_CONV, D_MODEL), row),
        out_shape=jax.ShapeDtypeStruct((T, D_MODEL), F32),
        scratch_shapes=[
            pltpu.VMEM((TM_CONV + CONV_HALO + SUBLANES, D_MODEL), F32),
            pltpu.VMEM((TM_CONV, D_MODEL), F32),
        ],
        compiler_params=_params(("parallel",)),
        name="conv_back",
    )(g, g, x, wdw, bdw, lng, lnb, w2, b2, mg, mb)


def _ranks_before(vals):
    n = len(vals)
    ranks = []
    for j in range(n):
        r = jnp.zeros_like(vals[j])
        for i in range(n):
            if i == j:
                continue
            before = (vals[i] > vals[j]) | ((vals[i] == vals[j]) & (i < j))
            r = r + before.astype(F32)
        ranks.append(r)
    return ranks


PAIRS = [(a, b) for a in range(EXPERTS_PER_GROUP) for b in range(a + 1, EXPERTS_PER_GROUP)]
N_CLASSES = N_GROUPS * len(PAIRS)
CLASS_ROWS = 32
CLASS_EXPERTS = [(g * EXPERTS_PER_GROUP + a, g * EXPERTS_PER_GROUP + b)
                 for g in range(N_GROUPS) for (a, b) in PAIRS]
TM_G = 256
NT_MAX = -(-(T + N_CLASSES * (TM_G - 1)) // TM_G)
P_MAX = NT_MAX * TM_G
XG_W = D_MODEL + LANE
GATE_ROWS = 8
TD = 256
NT_MIN = T // TM_G
N_ZERO_TILES = N_CLASSES + NT_MAX - NT_MIN


def _route_kernel(x_ref, wrt_ref, rb_ref, gw_ref, cls_ref, pos_ref, tot_ref, tri_ref, carry_ref):
    i = pl.program_id(0)

    @pl.when(i == 0)
    def _():
        r = lax.broadcasted_iota(jnp.int32, (TM_ROUTER, TM_ROUTER), 0)
        c = lax.broadcasted_iota(jnp.int32, (TM_ROUTER, TM_ROUTER), 1)
        tri_ref[...] = jnp.where(r < c, 1.0, 0.0).astype(BF16)
        carry_ref[...] = jnp.zeros_like(carry_ref)

    x = x_ref[...]
    logits = lax.dot_general(wrt_ref[...], x, _NT, precision=lax.Precision.HIGHEST,
                             preferred_element_type=F32)
    aff = _sigmoid(logits)
    sel = aff + rb_ref[...]
    aff_rows = [aff[e:e + 1, :] for e in range(N_EXPERTS)]
    sel_rows = [sel[e:e + 1, :] for e in range(N_EXPERTS)]
    in_top2 = []
    scores = []
    for g in range(N_GROUPS):
        members = sel_rows[g * EXPERTS_PER_GROUP:(g + 1) * EXPERTS_PER_GROUP]
        ranks = _ranks_before(members)
        top = [r < 2.0 for r in ranks]
        in_top2.extend(top)
        s = jnp.zeros_like(members[0])
        for v, t in zip(members, top):
            s = s + jnp.where(t, v, 0.0)
        scores.append(s)
    g_ranks = _ranks_before(scores)
    w_rows = []
    for e in range(N_EXPERTS):
        chosen = (g_ranks[e // EXPERTS_PER_GROUP] < 1.0) & in_top2[e]
        w_rows.append(jnp.where(chosen, aff_rows[e], 0.0))
    denom = w_rows[0]
    for e in range(1, N_EXPERTS):
        denom = denom + w_rows[e]
    inv = 1.0 / denom

    masks = []
    wa = jnp.zeros_like(denom)
    wb = jnp.zeros_like(denom)
    for c, (ea, eb) in enumerate(CLASS_EXPERTS):
        m = (g_ranks[c // len(PAIRS)] < 1.0) & in_top2[ea] & in_top2[eb]
        masks.append(m.astype(F32))
        wa = wa + jnp.where(m, aff_rows[ea], 0.0)
        wb = wb + jnp.where(m, aff_rows[eb], 0.0)
    zero_row = jnp.zeros_like(denom)
    onehot = jnp.concatenate(masks + [zero_row] * (CLASS_ROWS - N_CLASSES), axis=0)

    before = jnp.dot(onehot.astype(BF16), tri_ref[...], preferred_element_type=F32)
    carry = carry_ref[...]
    class_id = lax.broadcasted_iota(jnp.int32, (CLASS_ROWS, 1), 0).astype(F32)
    pos_ref[...] = jnp.sum(onehot * (before + carry), axis=0, keepdims=True).astype(jnp.int32)
    cls_ref[...] = jnp.sum(onehot * class_id, axis=0, keepdims=True).astype(jnp.int32)
    carry = carry + jnp.sum(onehot, axis=1, keepdims=True)
    carry_ref[...] = carry
    tot_ref[...] = carry

    gw_ref[...] = jnp.concatenate([wa * inv, wb * inv] + [zero_row] * (GATE_ROWS - 2), axis=0)


def _route(x, wrt, rb):
    return pl.pallas_call(
        _route_kernel,
        grid=(T // TM_ROUTER,),
        in_specs=[
            pl.BlockSpec((TM_ROUTER, D_MODEL), lambda i: (i, 0)),
            pl.BlockSpec((N_EXPERTS, D_MODEL), lambda i: (0, 0)),
            pl.BlockSpec((N_EXPERTS, 1), lambda i: (0, 0)),
        ],
        out_specs=[
            pl.BlockSpec((GATE_ROWS, TM_ROUTER), lambda i: (0, i)),
            pl.BlockSpec((1, TM_ROUTER), lambda i: (0, i)),
            pl.BlockSpec((1, TM_ROUTER), lambda i: (0, i)),
            pl.BlockSpec((CLASS_ROWS, 1), lambda i: (0, 0)),
        ],
        out_shape=[
            jax.ShapeDtypeStruct((GATE_ROWS, T), F32),
            jax.ShapeDtypeStruct((1, T), jnp.int32),
            jax.ShapeDtypeStruct((1, T), jnp.int32),
            jax.ShapeDtypeStruct((CLASS_ROWS, 1), F32),
        ],
        scratch_shapes=[
            pltpu.VMEM((TM_ROUTER, TM_ROUTER), BF16),
            pltpu.VMEM((CLASS_ROWS, 1), F32),
        ],
        compiler_params=_params(("arbitrary",)),
        name="route",
    )(x, wrt, rb)


def _row_copies(pay, slot, dst_hbm, dest_ref, base, sem):
    return [pltpu.make_async_copy(pay.at[slot, r], dst_hbm.at[dest_ref[base + r]], sem.at[slot])
            for r in range(TD)]


def _dispatch_kernel(dest_ref, zstart_ref, x_ref, gw_ref, xs_hbm, pay, zbuf, zsem, sem):
    i = pl.program_id(0)
    last = pl.num_programs(0) - 1

    @pl.when(i == 0)
    def _():
        zbuf[...] = jnp.zeros_like(zbuf)

        def zero_tile(c):
            start = pl.multiple_of(zstart_ref[c], TM_G)
            return pltpu.make_async_copy(zbuf, xs_hbm.at[pl.ds(start, TM_G)], zsem)

        for c in range(N_ZERO_TILES):
            @pl.when(zstart_ref[c] >= 0)
            def _():
                zero_tile(c).start()
        for c in range(N_ZERO_TILES):
            @pl.when(zstart_ref[c] >= 0)
            def _():
                zero_tile(c).wait()

    for parity in range(2):
        @pl.when(i % 2 == parity)
        def _():
            pay[parity, :, :D_MODEL] = x_ref[...]
            gates = jnp.concatenate(
                [gw_ref[...], jnp.zeros((LANE - GATE_ROWS, TD), F32)], axis=0)
            pay[parity, :, D_MODEL:] = gates.T
            for cp in _row_copies(pay, parity, xs_hbm, dest_ref, i * TD, sem):
                cp.start()

            @pl.when(i > 0)
            def _():
                for cp in _row_copies(pay, 1 - parity, xs_hbm, dest_ref, (i - 1) * TD, sem):
                    cp.wait()

            @pl.when(i == last)
            def _():
                for cp in _row_copies(pay, parity, xs_hbm, dest_ref, i * TD, sem):
                    cp.wait()


def _dispatch(dest, zstart, x, gw):
    return pl.pallas_call(
        _dispatch_kernel,
        grid_spec=pltpu.PrefetchScalarGridSpec(
            num_scalar_prefetch=2,
            grid=(T // TD,),
            in_specs=[
                pl.BlockSpec((TD, D_MODEL), lambda i, d, z: (i, 0)),
                pl.BlockSpec((GATE_ROWS, TD), lambda i, d, z: (0, i)),
            ],
            out_specs=pl.BlockSpec(memory_space=pl.ANY),
            scratch_shapes=[
                pltpu.VMEM((2, TD, XG_W), F32),
                pltpu.VMEM((TM_G, XG_W), F32),
                pltpu.SemaphoreType.DMA,
                pltpu.SemaphoreType.DMA((2,)),
            ],
        ),
        out_shape=jax.ShapeDtypeStruct((P_MAX, XG_W), F32),
        compiler_params=_params(("arbitrary",)),
        name="dispatch",
    )(dest, zstart, x, gw)


def _experts_kernel(te_ref, nt_ref, xs_ref, wga_ref, wua_ref, wda_ref, wgb_ref, wub_ref, wdb_ref,
                    ys_ref):
    i = pl.program_id(0)

    @pl.when(i < nt_ref[0])
    def _():
        xb = xs_ref[:, :D_MODEL].astype(BF16)
        gates = xs_ref[:, D_MODEL:]
        y = None
        for which, (wg_ref, wu_ref, wd_ref) in enumerate(((wga_ref, wua_ref, wda_ref),
                                                          (wgb_ref, wub_ref, wdb_ref))):
            hg = jnp.dot(xb, wg_ref[0], preferred_element_type=F32)
            hu = jnp.dot(xb, wu_ref[0], preferred_element_type=F32)
            h = hg * _sigmoid(hg) * hu * gates[:, which:which + 1]
            part = jnp.dot(h.astype(BF16), wd_ref[0], preferred_element_type=F32)
            y = part if y is None else y + part
        ys_ref[...] = y

    @pl.when(i >= nt_ref[0])
    def _():
        ys_ref[...] = jnp.zeros_like(ys_ref)


def _experts(tile_e, n_tiles, xs, wg, wu, wd):
    def row_map(i, te, nt):
        return (jnp.minimum(i, nt[0] - 1), 0)

    def w_map(which):
        return lambda i, te, nt: (te[which * NT_MAX + i], 0, 0)

    up_spec = lambda which: pl.BlockSpec((1, D_MODEL, D_EXPERT), w_map(which))
    down_spec = lambda which: pl.BlockSpec((1, D_EXPERT, D_MODEL), w_map(which))
    return pl.pallas_call(
        _experts_kernel,
        grid_spec=pltpu.PrefetchScalarGridSpec(
            num_scalar_prefetch=2,
            grid=(NT_MAX,),
            in_specs=[
                pl.BlockSpec((TM_G, XG_W), row_map),
                up_spec(0), up_spec(0), down_spec(0),
                up_spec(1), up_spec(1), down_spec(1),
            ],
            out_specs=pl.BlockSpec((TM_G, D_MODEL), lambda i, te, nt: (i, 0)),
        ),
        out_shape=jax.ShapeDtypeStruct((P_MAX, D_MODEL), F32),
        compiler_params=_params(("arbitrary",)),
        name="experts",
    )(tile_e, n_tiles, xs, wg, wu, wd, wg, wu, wd)


def _gather_copies(ys_hbm, buf, dest_ref, base, slot, sem):
    return [pltpu.make_async_copy(ys_hbm.at[dest_ref[base + r]], buf.at[slot, r], sem.at[slot])
            for r in range(TD)]


def _combine_kernel(dest_ref, x_ref, ys_hbm, lg_ref, lb_ref, o_ref, buf, sem):
    i = pl.program_id(0)
    last = pl.num_programs(0) - 1

    @pl.when(i == 0)
    def _():
        for cp in _gather_copies(ys_hbm, buf, dest_ref, 0, 0, sem):
            cp.start()

    for parity in range(2):
        @pl.when(i % 2 == parity)
        def _():
            @pl.when(i < last)
            def _():
                for cp in _gather_copies(ys_hbm, buf, dest_ref, (i + 1) * TD, 1 - parity, sem):
                    cp.start()

            for cp in _gather_copies(ys_hbm, buf, dest_ref, i * TD, parity, sem):
                cp.wait()
            o_ref[...] = _layer_norm(ALPHA * x_ref[...] + buf[parity], lg_ref[...], lb_ref[...])


def _combine(dest, x, ys, lg, lb):
    vec = pl.BlockSpec((1, D_MODEL), lambda i, d: (0, 0))
    return pl.pallas_call(
        _combine_kernel,
        grid_spec=pltpu.PrefetchScalarGridSpec(
            num_scalar_prefetch=1,
            grid=(T // TD,),
            in_specs=[
                pl.BlockSpec((TD, D_MODEL), lambda i, d: (i, 0)),
                pl.BlockSpec(memory_space=pl.ANY),
                vec, vec,
            ],
            out_specs=pl.BlockSpec((TD, D_MODEL), lambda i, d: (i, 0)),
            scratch_shapes=[
                pltpu.VMEM((2, TD, D_MODEL), F32),
                pltpu.SemaphoreType.DMA((2,)),
            ],
        ),
        out_shape=jax.ShapeDtypeStruct((T, D_MODEL), F32),
        compiler_params=_params(("arbitrary",)),
        name="combine",
    )(dest, x, ys, lg, lb)


def _moe(x, wrt, rb, wg, wu, wd, lg, lb):
    gw, cls, pos, tot = _route(x, wrt, rb)
    counts = tot[:N_CLASSES, 0].astype(jnp.int32)
    padded = (counts + TM_G - 1) // TM_G * TM_G
    ends = jnp.cumsum(padded)
    starts = ends - padded
    dest = starts[cls[0]] + pos[0]
    n_tiles = ends[-1] // TM_G
    tail = jnp.arange(NT_MIN, NT_MAX)
    zstart = jnp.concatenate([jnp.where(padded > 0, ends - TM_G, -1),
                              jnp.where(tail >= n_tiles, tail * TM_G, -1)])
    tile = jnp.minimum(jnp.arange(NT_MAX), n_tiles - 1)
    tile_cls = jnp.sum(tile[:, None] * TM_G >= ends[None, :], axis=1)
    tile_e = jnp.asarray(CLASS_EXPERTS, jnp.int32)[tile_cls].T.reshape(-1)
    xs = _dispatch(dest, zstart, x, gw)
    ys = _experts(tile_e, n_tiles.reshape(1), xs, wg, wu, wd)
    return _combine(dest, x, ys, lg, lb)


_NT = (((1,), (1,)), ((), ()))


def _q_proj_kernel(x_ref, wt_ref, o_ref, *, scale):
    y = lax.dot_general(wt_ref[...], x_ref[...].astype(BF16), _NT, preferred_element_type=F32)
    o_ref[...] = (y * scale).astype(BF16)


def _q_proj(x, wt, scale):
    return pl.pallas_call(
        functools.partial(_q_proj_kernel, scale=scale),
        grid=(T // TM_PROJ,),
        in_specs=[
            pl.BlockSpec((TM_PROJ, D_MODEL), lambda i: (i, 0)),
            pl.BlockSpec((QK_WIDTH, D_MODEL), lambda i: (0, 0)),
        ],
        out_specs=pl.BlockSpec((QK_WIDTH, TM_PROJ), lambda i: (0, i)),
        out_shape=jax.ShapeDtypeStruct((QK_WIDTH, T), BF16),
        compiler_params=_params(("parallel",)),
        name="q_proj",
    )(x, wt)


def _kv_proj_kernel(x_ref, wk_ref, wvt_ref, k_ref, vt_ref):
    xb = x_ref[...].astype(BF16)
    k_ref[...] = jnp.dot(xb, wk_ref[...], preferred_element_type=F32).astype(BF16)
    vt_ref[0] = lax.dot_general(wvt_ref[...], xb, _NT, preferred_element_type=F32).astype(BF16)


def _kv_proj(x, wk, wvt):
    return pl.pallas_call(
        _kv_proj_kernel,
        grid=(T // TK,),
        in_specs=[
            pl.BlockSpec((TK, D_MODEL), lambda i: (i, 0)),
            pl.BlockSpec((D_MODEL, QK_WIDTH), lambda i: (0, 0)),
            pl.BlockSpec((V_WIDTH, D_MODEL), lambda i: (0, 0)),
        ],
        out_specs=[
            pl.BlockSpec((TK, QK_WIDTH), lambda i: (i, 0)),
            pl.BlockSpec((1, V_WIDTH, TK), lambda i: (i, 0, 0)),
        ],
        out_shape=[
            jax.ShapeDtypeStruct((T, QK_WIDTH), BF16),
            jax.ShapeDtypeStruct((T // TK, V_WIDTH, TK), BF16),
        ],
        compiler_params=_params(("parallel",)),
        name="kv_proj",
    )(x, wk, wvt)


def _proj_ln_kernel(a_ref, x_ref, w_ref, g_ref, b_ref, o_ref):
    mix = jnp.dot(a_ref[...], w_ref[...], preferred_element_type=F32)
    o_ref[...] = _layer_norm(ALPHA * x_ref[...] + mix, g_ref[...], b_ref[...])


def _proj_ln(a, x, w, g, b):
    vec = pl.BlockSpec((1, D_MODEL), lambda i: (0, 0))
    return pl.pallas_call(
        _proj_ln_kernel,
        grid=(T // TM_PROJ,),
        in_specs=[
            pl.BlockSpec((TM_PROJ, V_WIDTH), lambda i: (i, 0)),
            pl.BlockSpec((TM_PROJ, D_MODEL), lambda i: (i, 0)),
            pl.BlockSpec((V_WIDTH, D_MODEL), lambda i: (0, 0)),
            vec, vec,
        ],
        out_specs=pl.BlockSpec((TM_PROJ, D_MODEL), lambda i: (i, 0)),
        out_shape=jax.ShapeDtypeStruct((T, D_MODEL), F32),
        compiler_params=_params(("parallel",)),
        name="proj_ln",
    )(a, x, w, g, b)


def _bias_kernel(rb_ref, o_ref):
    h = pl.program_id(0)
    max_exact = N_BUCKETS // 2
    far = rb_ref[N_BUCKETS - 1, h]
    c = lax.broadcasted_iota(jnp.int32, (TK, TQ), 0)
    r = lax.broadcasted_iota(jnp.int32, (TK, TQ), 1)
    for kind in range(2):
        rel = kind * TK + r - c
        n = jnp.maximum(rel, 0)
        nf = jnp.maximum(n, 1).astype(F32)
        large = max_exact + (jnp.log(nf / max_exact) / math.log(MAX_DISTANCE / max_exact)
                             * (N_BUCKETS - max_exact)).astype(jnp.int32)
        large = jnp.minimum(large, N_BUCKETS - 1)
        bucket = jnp.where(n < max_exact, n, large)
        bias = jnp.zeros((TK, TQ), F32)
        for b in range(N_BUCKETS):
            bias = jnp.where(bucket == b, rb_ref[b, h] - far, bias)
        o_ref[0, kind] = jnp.where(rel >= 0, bias * LOG2E, NEG_BIG)


def _bias_tiles(rel_bias):
    return pl.pallas_call(
        _bias_kernel,
        grid=(N_HEADS,),
        in_specs=[pl.BlockSpec(memory_space=pltpu.SMEM)],
        out_specs=pl.BlockSpec((1, 2, TK, TQ), lambda h: (h, 0, 0, 0)),
        out_shape=jax.ShapeDtypeStruct((N_HEADS, 2, TK, TQ), F32),
        compiler_params=_params(("parallel",)),
        name="bias_tiles",
    )(rel_bias)


def _attn_kernel(q1t_ref, q2t_ref, k1_ref, k2_ref, vt_ref, bias_ref, lam_ref, sg_ref, o_ref,
                 s_ref, mx_ref, m_ref, l_ref, acc_ref, *, lambda_init):
    qi = pl.program_id(2)
    dim = lax.broadcasted_iota(jnp.int32, (LANE, 1), 0)
    lo = dim < HEAD_DIM
    q1 = q1t_ref[...]
    q2 = q2t_ref[...]
    zero = jnp.zeros_like(q1)
    qs = [jnp.where(lo, q1, zero), jnp.where(lo, q2, zero),
          jnp.where(lo, zero, q1), jnp.where(lo, zero, q2)]

    m_ref[...] = jnp.full(m_ref.shape, NEG_BIG, F32)
    l_ref[...] = jnp.zeros(l_ref.shape, F32)
    acc_ref[...] = jnp.zeros(acc_ref.shape, F32)

    def scores(j, slot):
        k0 = pl.multiple_of(j * TK, TK)
        k_ref = k1_ref if slot % 2 == 0 else k2_ref
        st = jnp.dot(k_ref[pl.ds(k0, TK), :], qs[slot], preferred_element_type=F32)
        s_ref[slot] = st
        mx_ref[slot] = jnp.max(st, axis=0, keepdims=True)

    def update(j, slot, kind):
        head = slot // 2
        st = s_ref[slot]
        if kind is None:
            mx = mx_ref[slot]
        else:
            st = st + bias_ref[head, kind]
            mx = jnp.max(st, axis=0, keepdims=True)
        m_prev = m_ref[slot]
        m_new = jnp.maximum(m_prev, mx)
        alpha = jnp.exp2(m_prev - m_new)
        p = jnp.exp2(st - m_new)
        l_ref[slot] = alpha * l_ref[slot] + jnp.sum(p, axis=0, keepdims=True)
        vt = vt_ref[j, head * V_DIM:(head + 1) * V_DIM, :]
        acc_ref[slot] = alpha * acc_ref[slot] + jnp.dot(vt, p.astype(BF16),
                                                        preferred_element_type=F32)
        m_ref[slot] = m_new

    def tile(j, kind, nxt):
        for slot in range(4):
            ahead = slot + 2
            if ahead < 4:
                scores(j, ahead)
            elif nxt is not None:
                scores(nxt, ahead - 4)
            update(j, slot, kind)

    scores(0, 0)
    scores(0, 1)

    def far_body(j, carry):
        tile(j, None, j + 1)
        return carry

    lax.fori_loop(0, jnp.maximum(qi - 1, 0), far_body, 0)

    @pl.when(qi >= 1)
    def _():
        tile(qi - 1, 1, qi)

    tile(qi, 0, None)

    lp = lam_ref[...]
    lam = (jnp.exp(jnp.sum(lp[0:1] * lp[1:2], axis=1, keepdims=True))
           - jnp.exp(jnp.sum(lp[2:3] * lp[3:4], axis=1, keepdims=True)) + lambda_init)
    for head in range(2):
        a1 = acc_ref[2 * head] / l_ref[2 * head]
        a2 = acc_ref[2 * head + 1] / l_ref[2 * head + 1]
        of = (a1 - lam * a2).T
        of = of * lax.rsqrt(jnp.mean(of * of, axis=-1, keepdims=True) + LN_EPS) * sg_ref[...]
        of = of * (1.0 - lambda_init)
        o_ref[:, head * V_DIM:(head + 1) * V_DIM] = of.astype(o_ref.dtype)


def _attention(qt, k, vt, bias, lam_params, subln_g, lambda_init):
    nq = SEQ // TQ
    nk = SEQ // TK
    qk_blocks = QK_WIDTH // 2 // LANE
    return pl.pallas_call(
        functools.partial(_attn_kernel, lambda_init=lambda_init),
        grid=(BATCH, N_PAIRS, nq),
        in_specs=[
            pl.BlockSpec((LANE, TQ), lambda b, p, i: (p, b * nq + i)),
            pl.BlockSpec((LANE, TQ), lambda b, p, i: (qk_blocks + p, b * nq + i)),
            pl.BlockSpec((SEQ, LANE), lambda b, p, i: (b, p)),
            pl.BlockSpec((SEQ, LANE), lambda b, p, i: (b, qk_blocks + p)),
            pl.BlockSpec((nk, 2 * V_DIM, TK), lambda b, p, i: (b, p, 0)),
            pl.BlockSpec((2, 2, TK, TQ), lambda b, p, i: (p, 0, 0, 0)),
            pl.BlockSpec((4, HEAD_DIM), lambda b, p, i: (0, 0)),
            pl.BlockSpec((1, V_DIM), lambda b, p, i: (0, 0)),
        ],
        out_specs=pl.BlockSpec((TQ, 2 * V_DIM), lambda b, p, i: (b * nq + i, p)),
        out_shape=jax.ShapeDtypeStruct((T, V_WIDTH), BF16),
        scratch_shapes=[
            pltpu.VMEM((4, TK, TQ), F32),
            pltpu.VMEM((4, 1, TQ), F32),
            pltpu.VMEM((4, 1, TQ), F32),
            pltpu.VMEM((4, 1, TQ), F32),
            pltpu.VMEM((4, V_DIM, TQ), F32),
        ],
        compiler_params=_params(("parallel", "parallel", "arbitrary"), vmem_mb=56),
        name="diff_attention",
    )(qt, qt, k, k, vt, bias, lam_params, subln_g)


def kernel(x, a_w_pw1, a_b_pw1, a_w_dw, a_b_dw, a_ln_g, a_ln_b, a_w_pw2, a_b_pw2, w_kv, b_w_q,
           b_lambda, b_subln_g, b_w_o, rel_bias, ln_mix_g, ln_mix_b, ln_ffn_g, ln_ffn_b,
           router_w, router_bias, moe_w_gate, moe_w_up, moe_w_down):
    x = x.reshape(T, D_MODEL)
    row = lambda v: v.reshape(1, -1)
    wrt = router_w.T
    rb = router_bias.reshape(N_EXPERTS, 1)
    k_all = vt_all = bias = None
    for l in range(DEPTH):
        if l < N_A:
            g = _glu(x, a_w_pw1[l].astype(BF16), row(a_b_pw1[l]))
            wdw = jnp.pad(a_w_dw[l], ((0, CONV_HALO - CONV_WIDTH), (0, 0)))
            x = _conv_back(g, x, wdw, row(a_b_dw[l]), row(a_ln_g[l]), row(a_ln_b[l]),
                           a_w_pw2[l].astype(BF16), row(a_b_pw2[l]),
                           row(ln_mix_g[l]), row(ln_mix_b[l]))
        else:
            if l == N_A:
                k_all, vt_all = _kv_proj(x, w_kv[:, :QK_WIDTH].astype(BF16),
                                         w_kv[:, QK_WIDTH:].T.astype(BF16))
                bias = _bias_tiles(rel_bias)
            j = l - N_A
            lambda_init = 0.8 - 0.6 * math.exp(-0.3 * l)
            qt = _q_proj(x, b_w_q[j].T.astype(BF16), HEAD_DIM ** -0.5 * LOG2E)
            o = _attention(qt, k_all, vt_all, bias, b_lambda[j], row(b_subln_g[j]), lambda_init)
            x = _proj_ln(o, x, b_w_o[j].astype(BF16), row(ln_mix_g[l]), row(ln_mix_b[l]))
        x = _moe(x, wrt, rb, moe_w_gate[l].astype(BF16), moe_w_up[l].astype(BF16),
                 moe_w_down[l].astype(BF16), row(ln_ffn_g[l]), row(ln_ffn_b[l]))
    return x.reshape(BATCH, SEQ, D_MODEL)
```

```python
import functools
import math

import jax
import jax.numpy as jnp
from jax import lax
from jax.experimental import pallas as pl
from jax.experimental.pallas import tpu as pltpu

D_MODEL = 1024
BATCH = 2
SEQ = 8192
DEPTH = 4
N_A = DEPTH // 2
CONV_WIDTH = 31
N_HEADS = 8
HEAD_DIM = 64
V_DIM = 2 * HEAD_DIM
QK_WIDTH = 2 * N_HEADS * HEAD_DIM
V_WIDTH = N_HEADS * V_DIM
N_BUCKETS = 32
MAX_DISTANCE = 128
N_EXPERTS = 16
N_GROUPS = 4
EXPERTS_PER_GROUP = N_EXPERTS // N_GROUPS
D_EXPERT = 512
ALPHA = (2.0 * DEPTH) ** 0.25
LN_EPS = 1e-5

T = BATCH * SEQ
F32 = jnp.float32
BF16 = jnp.bfloat16
LOG2E = 1.4426950408889634
NEG_BIG = -1e30
LANE = 128
CONV_HALO = 32
N_PAIRS = N_HEADS // 2

TM_GLU = 512
TM_CONV = 256
TM_ROUTER = 1024
TM_MOE = 1024
TM_PROJ = 512
TQ = 512
TK = 512


def _params(sem, vmem_mb=None, flags=None):
    kw = dict(dimension_semantics=sem)
    if vmem_mb is not None:
        kw["vmem_limit_bytes"] = vmem_mb * 1024 * 1024
    if flags is not None:
        kw["flags"] = flags
    return pltpu.CompilerParams(**kw)


def _layer_norm(v, g, b):
    mu = jnp.mean(v, axis=-1, keepdims=True)
    d = v - mu
    var = jnp.mean(d * d, axis=-1, keepdims=True)
    return d * lax.rsqrt(var + LN_EPS) * g + b


def _sigmoid(v):
    return 1.0 / (1.0 + jnp.exp(-v))


def _glu_kernel(x_ref, w_ref, b_ref, o_ref):
    xb = x_ref[...].astype(BF16)
    a = jnp.dot(xb, w_ref[:, :D_MODEL], preferred_element_type=F32) + b_ref[:, :D_MODEL]
    gate = jnp.dot(xb, w_ref[:, D_MODEL:], preferred_element_type=F32) + b_ref[:, D_MODEL:]
    o_ref[...] = a * _sigmoid(gate)


def _glu(x, w, b):
    return pl.pallas_call(
        _glu_kernel,
        grid=(T // TM_GLU,),
        in_specs=[
            pl.BlockSpec((TM_GLU, D_MODEL), lambda i: (i, 0)),
            pl.BlockSpec((D_MODEL, 2 * D_MODEL), lambda i: (0, 0)),
            pl.BlockSpec((1, 2 * D_MODEL), lambda i: (0, 0)),
        ],
        out_specs=pl.BlockSpec((TM_GLU, D_MODEL), lambda i: (i, 0)),
        out_shape=jax.ShapeDtypeStruct((T, D_MODEL), F32),
        compiler_params=_params(("parallel",)),
        name="glu_front",
    )(x, w, b)


SUBLANES = 8
CONV_BASE = CONV_HALO - (CONV_WIDTH - 1)


def _conv_kernel(g_ref, halo_ref, x_ref, wdw_ref, bdw_ref, lng_ref, lnb_ref, w2_ref, b2_ref,
                 mg_ref, mb_ref, o_ref, buf_ref, cv_ref):
    i = pl.program_id(0)
    first = (i % (SEQ // TM_CONV)) == 0
    buf_ref[0:CONV_HALO, :] = jnp.where(first, 0.0, halo_ref[...])
    buf_ref[CONV_HALO:CONV_HALO + TM_CONV, :] = g_ref[...]
    buf_ref[CONV_HALO + TM_CONV:, :] = jnp.zeros((SUBLANES, D_MODEL), F32)
    sub = lax.broadcasted_iota(jnp.int32, (SUBLANES, LANE), 0)
    n_q = (CONV_BASE + CONV_WIDTH - 1) // SUBLANES + 1
    for c in range(D_MODEL // LANE):
        cs = slice(c * LANE, (c + 1) * LANE)
        w_b = [jnp.broadcast_to(wdw_ref[j:j + 1, cs], (SUBLANES, LANE)) for j in range(CONV_WIDTH)]
        bias = jnp.broadcast_to(bdw_ref[:, cs], (SUBLANES, LANE))

        def partials(v):
            tiles = [buf_ref[pl.ds(v + SUBLANES * q, SUBLANES), cs] for q in range(n_q)]
            out = []
            for s in range(SUBLANES):
                a = None
                for q in range(n_q):
                    j = SUBLANES * q + s - CONV_BASE
                    if 0 <= j < CONV_WIDTH:
                        term = w_b[j] * tiles[q]
                        a = term if a is None else a + term
                out.append(a)
            return tuple(out)

        def row_body(t, prev):
            r = pl.multiple_of(t * SUBLANES, SUBLANES)
            cur = partials(pl.multiple_of(r + SUBLANES, SUBLANES))
            acc = bias + prev[0]
            for s in range(1, SUBLANES):
                mixed = jnp.where(sub >= s, prev[s], cur[s])
                acc = acc + pltpu.roll(mixed, SUBLANES - s, axis=0)
            cv_ref[pl.ds(r, SUBLANES), cs] = acc
            return cur

        lax.fori_loop(0, TM_CONV // SUBLANES, row_body, partials(0), unroll=2)
    h = _layer_norm(cv_ref[...], lng_ref[...], lnb_ref[...])
    h = h * _sigmoid(h)
    mix = jnp.dot(h.astype(BF16), w2_ref[...], preferred_element_type=F32) + b2_ref[...]
    o_ref[...] = _layer_norm(ALPHA * x_ref[...] + mix, mg_ref[...], mb_ref[...])


def _conv_back(g, x, wdw, bdw, lng, lnb, w2, b2, mg, mb):
    row = lambda i: (i, 0)
    fixed = lambda i: (0, 0)
    vec = pl.BlockSpec((1, D_MODEL), fixed)
    halo_blocks = TM_CONV // CONV_HALO
    return pl.pallas_call(
        _conv_kernel,
        grid=(T // TM_CONV,),
        in_specs=[
            pl.BlockSpec((TM_CONV, D_MODEL), row),
            pl.BlockSpec((CONV_HALO, D_MODEL), lambda i: (jnp.maximum(i * halo_blocks - 1, 0), 0)),
            pl.BlockSpec((TM_CONV, D_MODEL), row),
            pl.BlockSpec((CONV_HALO, D_MODEL), fixed),
            vec, vec, vec,
            pl.BlockSpec((D_MODEL, D_MODEL), fixed),
            vec, vec, vec,
        ],
        out_specs=pl.BlockSpec((TM_CONV, D_MODEL), row),
        out_shape=jax.ShapeDtypeStruct((T, D_MODEL), F32),
        scratch_shapes=[
            pltpu.VMEM((TM_CONV + CONV_HALO + SUBLANES, D_MODEL), F32),
            pltpu.VMEM((TM_CONV, D_MODEL), F32),
        ],
        compiler_params=_params(("parallel",)),
        name="conv_back",
    )(g, g, x, wdw, bdw, lng, lnb, w2, b2, mg, mb)


def _ranks_before(vals):
    n = len(vals)
    ranks = []
    for j in range(n):
        r = jnp.zeros_like(vals[j])
        for i in range(n):
            if i == j:
                continue
            before = (vals[i] > vals[j]) | ((vals[i] == vals[j]) & (i < j))
            r = r + before.astype(F32)
        ranks.append(r)
    return ranks


PAIRS = [(a, b) for a in range(EXPERTS_PER_GROUP) for b in range(a + 1, EXPERTS_PER_GROUP)]
N_CLASSES = N_GROUPS * len(PAIRS)
CLASS_ROWS = 32
CLASS_EXPERTS = [(g * EXPERTS_PER_GROUP + a, g * EXPERTS_PER_GROUP + b)
                 for g in range(N_GROUPS) for (a, b) in PAIRS]
TM_G = 256
NT_MAX = -(-(T + N_CLASSES * (TM_G - 1)) // TM_G)
P_MAX = NT_MAX * TM_G
XG_W = D_MODEL + LANE
GATE_ROWS = 8
TD = 256
NT_MIN = T // TM_G
N_ZERO_TILES = N_CLASSES + NT_MAX - NT_MIN


def _route_kernel(x_ref, wrt_ref, rb_ref, gw_ref, cls_ref, pos_ref, tot_ref, tri_ref, carry_ref):
    i = pl.program_id(0)

    @pl.when(i == 0)
    def _():
        r = lax.broadcasted_iota(jnp.int32, (TM_ROUTER, TM_ROUTER), 0)
        c = lax.broadcasted_iota(jnp.int32, (TM_ROUTER, TM_ROUTER), 1)
        tri_ref[...] = jnp.where(r < c, 1.0, 0.0).astype(BF16)
        carry_ref[...] = jnp.zeros_like(carry_ref)

    x = x_ref[...]
    x_hi = x.astype(BF16)
    x_mid = (x - x_hi.astype(F32)).astype(BF16)
    both = lax.dot_general(wrt_ref[...], x_hi, _NT, preferred_element_type=F32)
    logits = (both[:N_EXPERTS] + both[N_EXPERTS:]
              + lax.dot_general(wrt_ref[:N_EXPERTS, :], x_mid, _NT, preferred_element_type=F32))
    aff = _sigmoid(logits)
    sel = aff + rb_ref[...]
    aff_rows = [aff[e:e + 1, :] for e in range(N_EXPERTS)]
    sel_rows = [sel[e:e + 1, :] for e in range(N_EXPERTS)]
    in_top2 = []
    scores = []
    for g in range(N_GROUPS):
        members = sel_rows[g * EXPERTS_PER_GROUP:(g + 1) * EXPERTS_PER_GROUP]
        ranks = _ranks_before(members)
        top = [r < 2.0 for r in ranks]
        in_top2.extend(top)
        s = jnp.zeros_like(members[0])
        for v, t in zip(members, top):
            s = s + jnp.where(t, v, 0.0)
        scores.append(s)
    g_ranks = _ranks_before(scores)
    w_rows = []
    for e in range(N_EXPERTS):
        chosen = (g_ranks[e // EXPERTS_PER_GROUP] < 1.0) & in_top2[e]
        w_rows.append(jnp.where(chosen, aff_rows[e], 0.0))
    denom = w_rows[0]
    for e in range(1, N_EXPERTS):
        denom = denom + w_rows[e]
    inv = 1.0 / denom

    masks = []
    wa = jnp.zeros_like(denom)
    wb = jnp.zeros_like(denom)
    for c, (ea, eb) in enumerate(CLASS_EXPERTS):
        m = (g_ranks[c // len(PAIRS)] < 1.0) & in_top2[ea] & in_top2[eb]
        masks.append(m.astype(F32))
        wa = wa + jnp.where(m, aff_rows[ea], 0.0)
        wb = wb + jnp.where(m, aff_rows[eb], 0.0)
    zero_row = jnp.zeros_like(denom)
    onehot = jnp.concatenate(masks + [zero_row] * (CLASS_ROWS - N_CLASSES), axis=0)

    before = jnp.dot(onehot.astype(BF16), tri_ref[...], preferred_element_type=F32)
    carry = carry_ref[...]
    class_id = lax.broadcasted_iota(jnp.int32, (CLASS_ROWS, 1), 0).astype(F32)
    pos_ref[...] = jnp.sum(onehot * (before + carry), axis=0, keepdims=True).astype(jnp.int32)
    cls_ref[...] = jnp.sum(onehot * class_id, axis=0, keepdims=True).astype(jnp.int32)
    carry = carry + jnp.sum(onehot, axis=1, keepdims=True)
    carry_ref[...] = carry
    tot_ref[...] = carry

    gw_ref[...] = jnp.concatenate([wa * inv, wb * inv] + [zero_row] * (GATE_ROWS - 2), axis=0)


def _route(x, wrt, rb):
    return pl.pallas_call(
        _route_kernel,
        grid=(T // TM_ROUTER,),
        in_specs=[
            pl.BlockSpec((TM_ROUTER, D_MODEL), lambda i: (i, 0)),
            pl.BlockSpec((2 * N_EXPERTS, D_MODEL), lambda i: (0, 0)),
            pl.BlockSpec((N_EXPERTS, 1), lambda i: (0, 0)),
        ],
        out_specs=[
            pl.BlockSpec((GATE_ROWS, TM_ROUTER), lambda i: (0, i)),
            pl.BlockSpec((1, TM_ROUTER), lambda i: (0, i)),
            pl.BlockSpec((1, TM_ROUTER), lambda i: (0, i)),
            pl.BlockSpec((CLASS_ROWS, 1), lambda i: (0, 0)),
        ],
        out_shape=[
            jax.ShapeDtypeStruct((GATE_ROWS, T), F32),
            jax.ShapeDtypeStruct((1, T), jnp.int32),
            jax.ShapeDtypeStruct((1, T), jnp.int32),
            jax.ShapeDtypeStruct((CLASS_ROWS, 1), F32),
        ],
        scratch_shapes=[
            pltpu.VMEM((TM_ROUTER, TM_ROUTER), BF16),
            pltpu.VMEM((CLASS_ROWS, 1), F32),
        ],
        compiler_params=_params(("arbitrary",)),
        name="route",
    )(x, wrt, rb)


def _start_all(copies):
    for r, cp in enumerate(copies):
        cp.start(priority=r % 2)


def _row_copies(pay, slot, dst_hbm, dest_ref, base, sem):
    return [pltpu.make_async_copy(pay.at[slot, r], dst_hbm.at[dest_ref[base + r]], sem.at[slot])
            for r in range(TD)]


def _dispatch_kernel(dest_ref, zstart_ref, x_ref, gw_ref, xs_hbm, pay, zbuf, zsem, sem):
    i = pl.program_id(0)
    last = pl.num_programs(0) - 1

    @pl.when(i == 0)
    def _():
        zbuf[...] = jnp.zeros_like(zbuf)

        def zero_tile(c):
            start = pl.multiple_of(zstart_ref[c], TM_G)
            return pltpu.make_async_copy(zbuf, xs_hbm.at[pl.ds(start, TM_G)], zsem)

        for c in range(N_ZERO_TILES):
            @pl.when(zstart_ref[c] >= 0)
            def _():
                zero_tile(c).start()
        for c in range(N_ZERO_TILES):
            @pl.when(zstart_ref[c] >= 0)
            def _():
                zero_tile(c).wait()

    for parity in range(2):
        @pl.when(i % 2 == parity)
        def _():
            pay[parity, :, :D_MODEL] = x_ref[...]
            gates = jnp.concatenate(
                [gw_ref[...], jnp.zeros((LANE - GATE_ROWS, TD), F32)], axis=0)
            pay[parity, :, D_MODEL:] = gates.T
            _start_all(_row_copies(pay, parity, xs_hbm, dest_ref, i * TD, sem))

            @pl.when(i > 0)
            def _():
                for cp in _row_copies(pay, 1 - parity, xs_hbm, dest_ref, (i - 1) * TD, sem):
                    cp.wait()

            @pl.when(i == last)
            def _():
                for cp in _row_copies(pay, parity, xs_hbm, dest_ref, i * TD, sem):
                    cp.wait()


def _dispatch(dest, zstart, x, gw):
    return pl.pallas_call(
        _dispatch_kernel,
        grid_spec=pltpu.PrefetchScalarGridSpec(
            num_scalar_prefetch=2,
            grid=(T // TD,),
            in_specs=[
                pl.BlockSpec((TD, D_MODEL), lambda i, d, z: (i, 0)),
                pl.BlockSpec((GATE_ROWS, TD), lambda i, d, z: (0, i)),
            ],
            out_specs=pl.BlockSpec(memory_space=pl.ANY),
            scratch_shapes=[
                pltpu.VMEM((2, TD, XG_W), F32),
                pltpu.VMEM((TM_G, XG_W), F32),
                pltpu.SemaphoreType.DMA,
                pltpu.SemaphoreType.DMA((2,)),
            ],
        ),
        out_shape=jax.ShapeDtypeStruct((P_MAX, XG_W), F32),
        compiler_params=_params(("arbitrary",)),
        name="dispatch",
    )(dest, zstart, x, gw)


def _experts_kernel(te_ref, nt_ref, xs_ref, wga_ref, wua_ref, wda_ref, wgb_ref, wub_ref, wdb_ref,
                    ys_ref):
    i = pl.program_id(0)

    @pl.when(i < nt_ref[0])
    def _():
        xb = xs_ref[:, :D_MODEL].astype(BF16)
        gates = xs_ref[:, D_MODEL:]
        y = None
        for which, (wg_ref, wu_ref, wd_ref) in enumerate(((wga_ref, wua_ref, wda_ref),
                                                          (wgb_ref, wub_ref, wdb_ref))):
            hg = jnp.dot(xb, wg_ref[0, 0], preferred_element_type=F32)
            hu = jnp.dot(xb, wu_ref[0, 0], preferred_element_type=F32)
            h = hg * _sigmoid(hg) * hu * gates[:, which:which + 1]
            part = jnp.dot(h.astype(BF16), wd_ref[0, 0], preferred_element_type=F32)
            y = part if y is None else y + part
        ys_ref[...] = y

    @pl.when(i >= nt_ref[0])
    def _():
        ys_ref[...] = jnp.zeros_like(ys_ref)


def _experts(tile_e, n_tiles, xs, wg, wu, wd, layer):
    def row_map(i, te, nt):
        return (jnp.minimum(i, nt[0] - 1), 0)

    def w_map(which):
        return lambda i, te, nt: (layer, te[which * NT_MAX + i], 0, 0)

    up_spec = lambda which: pl.BlockSpec((1, 1, D_MODEL, D_EXPERT), w_map(which))
    down_spec = lambda which: pl.BlockSpec((1, 1, D_EXPERT, D_MODEL), w_map(which))
    return pl.pallas_call(
        _experts_kernel,
        grid_spec=pltpu.PrefetchScalarGridSpec(
            num_scalar_prefetch=2,
            grid=(NT_MAX,),
            in_specs=[
                pl.BlockSpec((TM_G, XG_W), row_map),
                up_spec(0), up_spec(0), down_spec(0),
                up_spec(1), up_spec(1), down_spec(1),
            ],
            out_specs=pl.BlockSpec((TM_G, D_MODEL), lambda i, te, nt: (i, 0)),
        ),
        out_shape=jax.ShapeDtypeStruct((P_MAX, D_MODEL), F32),
        compiler_params=_params(("arbitrary",)),
        name="experts",
    )(tile_e, n_tiles, xs, wg, wu, wd, wg, wu, wd)


def _gather_copies(ys_hbm, buf, dest_ref, base, slot, sem):
    return [pltpu.make_async_copy(ys_hbm.at[dest_ref[base + r]], buf.at[slot, r], sem.at[slot])
            for r in range(TD)]


def _combine_kernel(dest_ref, x_ref, ys_hbm, lg_ref, lb_ref, o_ref, buf, sem):
    i = pl.program_id(0)
    last = pl.num_programs(0) - 1

    @pl.when(i == 0)
    def _():
        _start_all(_gather_copies(ys_hbm, buf, dest_ref, 0, 0, sem))

    for parity in range(2):
        @pl.when(i % 2 == parity)
        def _():
            @pl.when(i < last)
            def _():
                _start_all(_gather_copies(ys_hbm, buf, dest_ref, (i + 1) * TD, 1 - parity, sem))

            for cp in _gather_copies(ys_hbm, buf, dest_ref, i * TD, parity, sem):
                cp.wait()
            o_ref[...] = _layer_norm(ALPHA * x_ref[...] + buf[parity], lg_ref[...], lb_ref[...])


def _combine(dest, x, ys, lg, lb):
    vec = pl.BlockSpec((1, D_MODEL), lambda i, d: (0, 0))
    return pl.pallas_call(
        _combine_kernel,
        grid_spec=pltpu.PrefetchScalarGridSpec(
            num_scalar_prefetch=1,
            grid=(T // TD,),
            in_specs=[
                pl.BlockSpec((TD, D_MODEL), lambda i, d: (i, 0)),
                pl.BlockSpec(memory_space=pl.ANY),
                vec, vec,
            ],
            out_specs=pl.BlockSpec((TD, D_MODEL), lambda i, d: (i, 0)),
            scratch_shapes=[
                pltpu.VMEM((2, TD, D_MODEL), F32),
                pltpu.SemaphoreType.DMA((2,)),
            ],
        ),
        out_shape=jax.ShapeDtypeStruct((T, D_MODEL), F32),
        compiler_params=_params(("arbitrary",)),
        name="combine",
    )(dest, x, ys, lg, lb)


def _moe(x, wrt, rb, wg, wu, wd, layer, lg, lb):
    gw, cls, pos, tot = _route(x, wrt, rb)
    counts = tot[:N_CLASSES, 0].astype(jnp.int32)
    padded = (counts + TM_G - 1) // TM_G * TM_G
    ends = jnp.cumsum(padded)
    starts = ends - padded
    dest = starts[cls[0]] + pos[0]
    n_tiles = ends[-1] // TM_G
    tail = jnp.arange(NT_MIN, NT_MAX)
    zstart = jnp.concatenate([jnp.where(padded > 0, ends - TM_G, -1),
                              jnp.where(tail >= n_tiles, tail * TM_G, -1)])
    tile = jnp.minimum(jnp.arange(NT_MAX), n_tiles - 1)
    tile_cls = jnp.sum(tile[:, None] * TM_G >= ends[None, :], axis=1)
    tile_e = jnp.asarray(CLASS_EXPERTS, jnp.int32)[tile_cls].T.reshape(-1)
    xs = _dispatch(dest, zstart, x, gw)
    ys = _experts(tile_e, n_tiles.reshape(1), xs, wg, wu, wd, layer)
    return _combine(dest, x, ys, lg, lb)


_NT = (((1,), (1,)), ((), ()))


def _q_proj_kernel(x_ref, wt_ref, o_ref, *, scale):
    y = lax.dot_general(wt_ref[...], x_ref[...].astype(BF16), _NT, preferred_element_type=F32)
    o_ref[...] = (y * scale).astype(BF16)


def _q_proj(x, wt, scale):
    return pl.pallas_call(
        functools.partial(_q_proj_kernel, scale=scale),
        grid=(T // TM_PROJ,),
        in_specs=[
            pl.BlockSpec((TM_PROJ, D_MODEL), lambda i: (i, 0)),
            pl.BlockSpec((QK_WIDTH, D_MODEL), lambda i: (0, 0)),
        ],
        out_specs=pl.BlockSpec((QK_WIDTH, TM_PROJ), lambda i: (0, i)),
        out_shape=jax.ShapeDtypeStruct((QK_WIDTH, T), BF16),
        compiler_params=_params(("parallel",)),
        name="q_proj",
    )(x, wt)


def _kv_proj_kernel(x_ref, wk_ref, wvt_ref, k_ref, vt_ref):
    xb = x_ref[...].astype(BF16)
    k_ref[...] = jnp.dot(xb, wk_ref[...], preferred_element_type=F32).astype(BF16)
    vt_ref[0] = lax.dot_general(wvt_ref[...], xb, _NT, preferred_element_type=F32).astype(BF16)


def _kv_proj(x, wk, wvt):
    return pl.pallas_call(
        _kv_proj_kernel,
        grid=(T // TK,),
        in_specs=[
            pl.BlockSpec((TK, D_MODEL), lambda i: (i, 0)),
            pl.BlockSpec((D_MODEL, QK_WIDTH), lambda i: (0, 0)),
            pl.BlockSpec((V_WIDTH, D_MODEL), lambda i: (0, 0)),
        ],
        out_specs=[
            pl.BlockSpec((TK, QK_WIDTH), lambda i: (i, 0)),
            pl.BlockSpec((1, V_WIDTH, TK), lambda i: (i, 0, 0)),
        ],
        out_shape=[
            jax.ShapeDtypeStruct((T, QK_WIDTH), BF16),
            jax.ShapeDtypeStruct((T // TK, V_WIDTH, TK), BF16),
        ],
        compiler_params=_params(("parallel",)),
        name="kv_proj",
    )(x, wk, wvt)


def _proj_ln_kernel(a_ref, x_ref, w_ref, g_ref, b_ref, o_ref):
    mix = jnp.dot(a_ref[...], w_ref[...], preferred_element_type=F32)
    o_ref[...] = _layer_norm(ALPHA * x_ref[...] + mix, g_ref[...], b_ref[...])


def _proj_ln(a, x, w, g, b):
    vec = pl.BlockSpec((1, D_MODEL), lambda i: (0, 0))
    return pl.pallas_call(
        _proj_ln_kernel,
        grid=(T // TM_PROJ,),
        in_specs=[
            pl.BlockSpec((TM_PROJ, V_WIDTH), lambda i: (i, 0)),
            pl.BlockSpec((TM_PROJ, D_MODEL), lambda i: (i, 0)),
            pl.BlockSpec((V_WIDTH, D_MODEL), lambda i: (0, 0)),
            vec, vec,
        ],
        out_specs=pl.BlockSpec((TM_PROJ, D_MODEL), lambda i: (i, 0)),
        out_shape=jax.ShapeDtypeStruct((T, D_MODEL), F32),
        compiler_params=_params(("parallel",)),
        name="proj_ln",
    )(a, x, w, g, b)


def _bias_kernel(rb_ref, o_ref):
    h = pl.program_id(0)
    max_exact = N_BUCKETS // 2
    far = rb_ref[N_BUCKETS - 1, h]
    c = lax.broadcasted_iota(jnp.int32, (TK, TQ), 0)
    r = lax.broadcasted_iota(jnp.int32, (TK, TQ), 1)
    for kind in range(2):
        rel = kind * TK + r - c
        n = jnp.maximum(rel, 0)
        nf = jnp.maximum(n, 1).astype(F32)
        large = max_exact + (jnp.log(nf / max_exact) / math.log(MAX_DISTANCE / max_exact)
                             * (N_BUCKETS - max_exact)).astype(jnp.int32)
        large = jnp.minimum(large, N_BUCKETS - 1)
        bucket = jnp.where(n < max_exact, n, large)
        bias = jnp.zeros((TK, TQ), F32)
        for b in range(N_BUCKETS):
            bias = jnp.where(bucket == b, rb_ref[b, h] - far, bias)
        o_ref[0, kind] = jnp.where(rel >= 0, bias * LOG2E, NEG_BIG)


def _bias_tiles(rel_bias):
    return pl.pallas_call(
        _bias_kernel,
        grid=(N_HEADS,),
        in_specs=[pl.BlockSpec(memory_space=pltpu.SMEM)],
        out_specs=pl.BlockSpec((1, 2, TK, TQ), lambda h: (h, 0, 0, 0)),
        out_shape=jax.ShapeDtypeStruct((N_HEADS, 2, TK, TQ), F32),
        compiler_params=_params(("parallel",)),
        name="bias_tiles",
    )(rel_bias)


def _attn_kernel(q1t_ref, q2t_ref, k1_ref, k2_ref, vt_ref, bias_ref, lam_ref, sg_ref, o_ref,
                 s_ref, mx_ref, m_ref, l_ref, acc_ref, *, lambda_init):
    qi = pl.program_id(2)
    dim = lax.broadcasted_iota(jnp.int32, (LANE, 1), 0)
    lo = dim < HEAD_DIM
    q1 = q1t_ref[...]
    q2 = q2t_ref[...]
    zero = jnp.zeros_like(q1)
    qs = [jnp.where(lo, q1, zero), jnp.where(lo, q2, zero),
          jnp.where(lo, zero, q1), jnp.where(lo, zero, q2)]

    m_ref[...] = jnp.full(m_ref.shape, NEG_BIG, F32)
    l_ref[...] = jnp.zeros(l_ref.shape, F32)
    acc_ref[...] = jnp.zeros(acc_ref.shape, F32)

    def scores(j, slot):
        k0 = pl.multiple_of(j * TK, TK)
        k_ref = k1_ref if slot % 2 == 0 else k2_ref
        st = jnp.dot(k_ref[pl.ds(k0, TK), :], qs[slot], preferred_element_type=F32)
        s_ref[slot] = st
        mx_ref[slot] = jnp.max(st, axis=0, keepdims=True)

    def update(j, slot, kind):
        head = slot // 2
        st = s_ref[slot]
        if kind is None:
            mx = mx_ref[slot]
        else:
            st = st + bias_ref[head, kind]
            mx = jnp.max(st, axis=0, keepdims=True)
        m_prev = m_ref[slot]
        m_new = jnp.maximum(m_prev, mx)
        alpha = jnp.exp2(m_prev - m_new)
        p = jnp.exp2(st - m_new)
        l_ref[slot] = alpha * l_ref[slot] + jnp.sum(p, axis=0, keepdims=True)
        vt = vt_ref[j, head * V_DIM:(head + 1) * V_DIM, :]
        acc_ref[slot] = alpha * acc_ref[slot] + jnp.dot(vt, p.astype(BF16),
                                                        preferred_element_type=F32)
        m_ref[slot] = m_new

    def tile(j, kind, nxt):
        for slot in range(4):
            ahead = slot + 2
            if ahead < 4:
                scores(j, ahead)
            elif nxt is not None:
                scores(nxt, ahead - 4)
            update(j, slot, kind)

    scores(0, 0)
    scores(0, 1)

    def far_body(j, carry):
        tile(j, None, j + 1)
        return carry

    lax.fori_loop(0, jnp.maximum(qi - 1, 0), far_body, 0)

    @pl.when(qi >= 1)
    def _():
        tile(qi - 1, 1, qi)

    tile(qi, 0, None)

    lp = lam_ref[...]
    lam = (jnp.exp(jnp.sum(lp[0:1] * lp[1:2], axis=1, keepdims=True))
           - jnp.exp(jnp.sum(lp[2:3] * lp[3:4], axis=1, keepdims=True)) + lambda_init)
    for head in range(2):
        a1 = acc_ref[2 * head] / l_ref[2 * head]
        a2 = acc_ref[2 * head + 1] / l_ref[2 * head + 1]
        of = (a1 - lam * a2).T
        of = of * lax.rsqrt(jnp.mean(of * of, axis=-1, keepdims=True) + LN_EPS) * sg_ref[...]
        of = of * (1.0 - lambda_init)
        o_ref[:, head * V_DIM:(head + 1) * V_DIM] = of.astype(o_ref.dtype)


def _attention(qt, k, vt, bias, lam_params, subln_g, lambda_init):
    nq = SEQ // TQ
    nk = SEQ // TK
    qk_blocks = QK_WIDTH // 2 // LANE
    return pl.pallas_call(
        functools.partial(_attn_kernel, lambda_init=lambda_init),
        grid=(BATCH, N_PAIRS, nq),
        in_specs=[
            pl.BlockSpec((LANE, TQ), lambda b, p, i: (p, b * nq + i)),
            pl.BlockSpec((LANE, TQ), lambda b, p, i: (qk_blocks + p, b * nq + i)),
            pl.BlockSpec((SEQ, LANE), lambda b, p, i: (b, p)),
            pl.BlockSpec((SEQ, LANE), lambda b, p, i: (b, qk_blocks + p)),
            pl.BlockSpec((nk, 2 * V_DIM, TK), lambda b, p, i: (b, p, 0)),
            pl.BlockSpec((2, 2, TK, TQ), lambda b, p, i: (p, 0, 0, 0)),
            pl.BlockSpec((4, HEAD_DIM), lambda b, p, i: (0, 0)),
            pl.BlockSpec((1, V_DIM), lambda b, p, i: (0, 0)),
        ],
        out_specs=pl.BlockSpec((TQ, 2 * V_DIM), lambda b, p, i: (b * nq + i, p)),
        out_shape=jax.ShapeDtypeStruct((T, V_WIDTH), BF16),
        scratch_shapes=[
            pltpu.VMEM((4, TK, TQ), F32),
            pltpu.VMEM((4, 1, TQ), F32),
            pltpu.VMEM((4, 1, TQ), F32),
            pltpu.VMEM((4, 1, TQ), F32),
            pltpu.VMEM((4, V_DIM, TQ), F32),
        ],
        compiler_params=_params(("parallel", "parallel", "arbitrary"), vmem_mb=56),
        name="diff_attention",
    )(qt, qt, k, k, vt, bias, lam_params, subln_g)


def kernel(x, a_w_pw1, a_b_pw1, a_w_dw, a_b_dw, a_ln_g, a_ln_b, a_w_pw2, a_b_pw2, w_kv, b_w_q,
           b_lambda, b_subln_g, b_w_o, rel_bias, ln_mix_g, ln_mix_b, ln_ffn_g, ln_ffn_b,
           router_w, router_bias, moe_w_gate, moe_w_up, moe_w_down):
    x = x.reshape(T, D_MODEL)
    row = lambda v: v.reshape(1, -1)
    wr_hi = router_w.T.astype(BF16)
    wr_mid = (router_w.T - wr_hi.astype(F32)).astype(BF16)
    wrt = jnp.concatenate([wr_hi, wr_mid], axis=0)
    rb = router_bias.reshape(N_EXPERTS, 1)
    k_all = vt_all = bias = None
    wg_all = moe_w_gate.astype(BF16)
    wu_all = moe_w_up.astype(BF16)
    wd_all = moe_w_down.astype(BF16)
    for l in range(DEPTH):
        if l < N_A:
            g = _glu(x, a_w_pw1[l].astype(BF16), row(a_b_pw1[l]))
            wdw = jnp.pad(a_w_dw[l], ((0, CONV_HALO - CONV_WIDTH), (0, 0)))
            x = _conv_back(g, x, wdw, row(a_b_dw[l]), row(a_ln_g[l]), row(a_ln_b[l]),
                           a_w_pw2[l].astype(BF16), row(a_b_pw2[l]),
                           row(ln_mix_g[l]), row(ln_mix_b[l]))
        else:
            if l == N_A:
                k_all, vt_all = _kv_proj(x, w_kv[:, :QK_WIDTH].astype(BF16),
                                         w_kv[:, QK_WIDTH:].T.astype(BF16))
                bias = _bias_tiles(rel_bias)
            j = l - N_A
            lambda_init = 0.8 - 0.6 * math.exp(-0.3 * l)
            qt = _q_proj(x, b_w_q[j].T.astype(BF16), HEAD_DIM ** -0.5 * LOG2E)
            o = _attention(qt, k_all, vt_all, bias, b_lambda[j], row(b_subln_g[j]), lambda_init)
            x = _proj_ln(o, x, b_w_o[j].astype(BF16), row(ln_mix_g[l]), row(ln_mix_b[l]))
        x = _moe(x, wrt, rb, wg_all, wu_all, wd_all, l, row(ln_ffn_g[l]), row(ln_ffn_b[l]))
    return x.reshape(BATCH, SEQ, D_MODEL)
```

```python
import functools
import math

import jax
import jax.numpy as jnp
from jax import lax
from jax.experimental import pallas as pl
from jax.experimental.pallas import tpu as pltpu

D_MODEL = 1024
BATCH = 2
SEQ = 8192
DEPTH = 4
N_A = DEPTH // 2
CONV_WIDTH = 31
N_HEADS = 8
HEAD_DIM = 64
V_DIM = 2 * HEAD_DIM
QK_WIDTH = 2 * N_HEADS * HEAD_DIM
V_WIDTH = N_HEADS * V_DIM
N_BUCKETS = 32
MAX_DISTANCE = 128
N_EXPERTS = 16
N_GROUPS = 4
EXPERTS_PER_GROUP = N_EXPERTS // N_GROUPS
D_EXPERT = 512
ALPHA = (2.0 * DEPTH) ** 0.25
LN_EPS = 1e-5

T = BATCH * SEQ
F32 = jnp.float32
BF16 = jnp.bfloat16
LOG2E = 1.4426950408889634
NEG_BIG = -1e30
LANE = 128
CONV_HALO = 32
N_PAIRS = N_HEADS // 2

TM_GLU = 512
TM_CONV = 256
TM_ROUTER = 1024
TM_MOE = 1024
TM_PROJ = 512
TQ = 512
TK = 512


def _params(sem, vmem_mb=None, flags=None):
    kw = dict(dimension_semantics=sem)
    if vmem_mb is not None:
        kw["vmem_limit_bytes"] = vmem_mb * 1024 * 1024
    if flags is not None:
        kw["flags"] = flags
    return pltpu.CompilerParams(**kw)


def _layer_norm(v, g, b):
    mu = jnp.mean(v, axis=-1, keepdims=True)
    d = v - mu
    var = jnp.mean(d * d, axis=-1, keepdims=True)
    return d * lax.rsqrt(var + LN_EPS) * g + b


def _sigmoid(v):
    return 1.0 / (1.0 + jnp.exp(-v))


def _glu_kernel(x_ref, w_ref, b_ref, o_ref):
    xb = x_ref[...].astype(BF16)
    a = jnp.dot(xb, w_ref[:, :D_MODEL], preferred_element_type=F32) + b_ref[:, :D_MODEL]
    gate = jnp.dot(xb, w_ref[:, D_MODEL:], preferred_element_type=F32) + b_ref[:, D_MODEL:]
    o_ref[...] = a * _sigmoid(gate)


def _glu(x, w, b):
    return pl.pallas_call(
        _glu_kernel,
        grid=(T // TM_GLU,),
        in_specs=[
            pl.BlockSpec((TM_GLU, D_MODEL), lambda i: (i, 0)),
            pl.BlockSpec((D_MODEL, 2 * D_MODEL), lambda i: (0, 0)),
            pl.BlockSpec((1, 2 * D_MODEL), lambda i: (0, 0)),
        ],
        out_specs=pl.BlockSpec((TM_GLU, D_MODEL), lambda i: (i, 0)),
        out_shape=jax.ShapeDtypeStruct((T, D_MODEL), F32),
        compiler_params=_params(("parallel",)),
        name="glu_front",
    )(x, w, b)


SUBLANES = 8
CONV_BASE = CONV_HALO - (CONV_WIDTH - 1)


def _conv_kernel(g_ref, halo_ref, x_ref, wdw_ref, bdw_ref, lng_ref, lnb_ref, w2_ref, b2_ref,
                 mg_ref, mb_ref, o_ref, buf_ref, cv_ref):
    i = pl.program_id(0)
    first = (i % (SEQ // TM_CONV)) == 0
    buf_ref[0:CONV_HALO, :] = jnp.where(first, 0.0, halo_ref[...])
    buf_ref[CONV_HALO:CONV_HALO + TM_CONV, :] = g_ref[...]
    buf_ref[CONV_HALO + TM_CONV:, :] = jnp.zeros((SUBLANES, D_MODEL), F32)
    sub = lax.broadcasted_iota(jnp.int32, (SUBLANES, LANE), 0)
    n_q = (CONV_BASE + CONV_WIDTH - 1) // SUBLANES + 1
    for c in range(D_MODEL // LANE):
        cs = slice(c * LANE, (c + 1) * LANE)
        w_b = [jnp.broadcast_to(wdw_ref[j:j + 1, cs], (SUBLANES, LANE)) for j in range(CONV_WIDTH)]
        bias = jnp.broadcast_to(bdw_ref[:, cs], (SUBLANES, LANE))

        def partials(v):
            tiles = [buf_ref[pl.ds(v + SUBLANES * q, SUBLANES), cs] for q in range(n_q)]
            out = []
            for s in range(SUBLANES):
                a = None
                for q in range(n_q):
                    j = SUBLANES * q + s - CONV_BASE
                    if 0 <= j < CONV_WIDTH:
                        term = w_b[j] * tiles[q]
                        a = term if a is None else a + term
                out.append(a)
            return tuple(out)

        def row_body(t, prev):
            r = pl.multiple_of(t * SUBLANES, SUBLANES)
            cur = partials(pl.multiple_of(r + SUBLANES, SUBLANES))
            acc = bias + prev[0]
            for s in range(1, SUBLANES):
                mixed = jnp.where(sub >= s, prev[s], cur[s])
                acc = acc + pltpu.roll(mixed, SUBLANES - s, axis=0)
            cv_ref[pl.ds(r, SUBLANES), cs] = acc
            return cur

        lax.fori_loop(0, TM_CONV // SUBLANES, row_body, partials(0), unroll=2)
    h = _layer_norm(cv_ref[...], lng_ref[...], lnb_ref[...])
    h = h * _sigmoid(h)
    mix = jnp.dot(h.astype(BF16), w2_ref[...], preferred_element_type=F32) + b2_ref[...]
    o_ref[...] = _layer_norm(ALPHA * x_ref[...] + mix, mg_ref[...], mb_ref[...])


def _conv_back(g, x, wdw, bdw, lng, lnb, w2, b2, mg, mb):
    row = lambda i: (i, 0)
    fixed = lambda i: (0, 0)
    vec = pl.BlockSpec((1, D_MODEL), fixed)
    halo_blocks = TM_CONV // CONV_HALO
    return pl.pallas_call(
        _conv_kernel,
        grid=(T // TM_CONV,),
        in_specs=[
            pl.BlockSpec((TM_CONV, D_MODEL), row),
            pl.BlockSpec((CONV_HALO, D_MODEL), lambda i: (jnp.maximum(i * halo_blocks - 1, 0), 0)),
            pl.BlockSpec((TM_CONV, D_MODEL), row),
            pl.BlockSpec((CONV_HALO, D_MODEL), fixed),
            vec, vec, vec,
            pl.BlockSpec((D_MODEL, D_MODEL), fixed),
            vec, vec, vec,
        ],
        out_specs=pl.BlockSpec((TM_CONV, D_MODEL), row),
        out_shape=jax.ShapeDtypeStruct((T, D_MODEL), F32),
        scratch_shapes=[
            pltpu.VMEM((TM_CONV + CONV_HALO + SUBLANES, D_MODEL), F32),
            pltpu.VMEM((TM_CONV, D_MODEL), F32),
        ],
        compiler_params=_params(("parallel",)),
        name="conv_back",
    )(g, g, x, wdw, bdw, lng, lnb, w2, b2, mg, mb)


def _ranks_before(vals):
    n = len(vals)
    ranks = []
    for j in range(n):
        r = jnp.zeros_like(vals[j])
        for i in range(n):
            if i == j:
                continue
            before = (vals[i] > vals[j]) | ((vals[i] == vals[j]) & (i < j))
            r = r + before.astype(F32)
        ranks.append(r)
    return ranks


PAIRS = [(a, b) for a in range(EXPERTS_PER_GROUP) for b in range(a + 1, EXPERTS_PER_GROUP)]
N_CLASSES = N_GROUPS * len(PAIRS)
CLASS_ROWS = 32
CLASS_EXPERTS = [(g * EXPERTS_PER_GROUP + a, g * EXPERTS_PER_GROUP + b)
                 for g in range(N_GROUPS) for (a, b) in PAIRS]
TM_G = 256
NT_MAX = -(-(T + N_CLASSES * (TM_G - 1)) // TM_G)
P_MAX = NT_MAX * TM_G
XG_W = D_MODEL + LANE
GATE_ROWS = 8
TD = 256
NT_MIN = T // TM_G
N_ZERO_TILES = N_CLASSES + NT_MAX - NT_MIN


def _route_kernel(x_ref, wrt_ref, rb_ref, gw_ref, cls_ref, pos_ref, tot_ref, tri_ref, carry_ref):
    i = pl.program_id(0)

    @pl.when(i == 0)
    def _():
        r = lax.broadcasted_iota(jnp.int32, (TM_ROUTER, TM_ROUTER), 0)
        c = lax.broadcasted_iota(jnp.int32, (TM_ROUTER, TM_ROUTER), 1)
        tri_ref[...] = jnp.where(r < c, 1.0, 0.0).astype(BF16)
        carry_ref[...] = jnp.zeros_like(carry_ref)

    x = x_ref[...]
    x_hi = x.astype(BF16)
    x_mid = (x - x_hi.astype(F32)).astype(BF16)
    both = lax.dot_general(wrt_ref[...], x_hi, _NT, preferred_element_type=F32)
    logits = (both[:N_EXPERTS] + both[N_EXPERTS:]
              + lax.dot_general(wrt_ref[:N_EXPERTS, :], x_mid, _NT, preferred_element_type=F32))
    aff = _sigmoid(logits)
    sel = aff + rb_ref[...]
    aff_rows = [aff[e:e + 1, :] for e in range(N_EXPERTS)]
    sel_rows = [sel[e:e + 1, :] for e in range(N_EXPERTS)]
    in_top2 = []
    scores = []
    for g in range(N_GROUPS):
        members = sel_rows[g * EXPERTS_PER_GROUP:(g + 1) * EXPERTS_PER_GROUP]
        ranks = _ranks_before(members)
        top = [r < 2.0 for r in ranks]
        in_top2.extend(top)
        s = jnp.zeros_like(members[0])
        for v, t in zip(members, top):
            s = s + jnp.where(t, v, 0.0)
        scores.append(s)
    g_ranks = _ranks_before(scores)
    w_rows = []
    for e in range(N_EXPERTS):
        chosen = (g_ranks[e // EXPERTS_PER_GROUP] < 1.0) & in_top2[e]
        w_rows.append(jnp.where(chosen, aff_rows[e], 0.0))
    denom = w_rows[0]
    for e in range(1, N_EXPERTS):
        denom = denom + w_rows[e]
    inv = 1.0 / denom

    masks = []
    wa = jnp.zeros_like(denom)
    wb = jnp.zeros_like(denom)
    for c, (ea, eb) in enumerate(CLASS_EXPERTS):
        m = (g_ranks[c // len(PAIRS)] < 1.0) & in_top2[ea] & in_top2[eb]
        masks.append(m.astype(F32))
        wa = wa + jnp.where(m, aff_rows[ea], 0.0)
        wb = wb + jnp.where(m, aff_rows[eb], 0.0)
    zero_row = jnp.zeros_like(denom)
    onehot = jnp.concatenate(masks + [zero_row] * (CLASS_ROWS - N_CLASSES), axis=0)

    before = jnp.dot(onehot.astype(BF16), tri_ref[...], preferred_element_type=F32)
    carry = carry_ref[...]
    class_id = lax.broadcasted_iota(jnp.int32, (CLASS_ROWS, 1), 0).astype(F32)
    pos_ref[...] = jnp.sum(onehot * (before + carry), axis=0, keepdims=True).astype(jnp.int32)
    cls_ref[...] = jnp.sum(onehot * class_id, axis=0, keepdims=True).astype(jnp.int32)
    carry = carry + jnp.sum(onehot, axis=1, keepdims=True)
    carry_ref[...] = carry
    tot_ref[...] = carry

    gw_ref[...] = jnp.concatenate([wa * inv, wb * inv] + [zero_row] * (GATE_ROWS - 2), axis=0)


def _route(x, wrt, rb):
    return pl.pallas_call(
        _route_kernel,
        grid=(T // TM_ROUTER,),
        in_specs=[
            pl.BlockSpec((TM_ROUTER, D_MODEL), lambda i: (i, 0)),
            pl.BlockSpec((2 * N_EXPERTS, D_MODEL), lambda i: (0, 0)),
            pl.BlockSpec((N_EXPERTS, 1), lambda i: (0, 0)),
        ],
        out_specs=[
            pl.BlockSpec((GATE_ROWS, TM_ROUTER), lambda i: (0, i)),
            pl.BlockSpec((1, TM_ROUTER), lambda i: (0, i)),
            pl.BlockSpec((1, TM_ROUTER), lambda i: (0, i)),
            pl.BlockSpec((CLASS_ROWS, 1), lambda i: (0, 0)),
        ],
        out_shape=[
            jax.ShapeDtypeStruct((GATE_ROWS, T), F32),
            jax.ShapeDtypeStruct((1, T), jnp.int32),
            jax.ShapeDtypeStruct((1, T), jnp.int32),
            jax.ShapeDtypeStruct((CLASS_ROWS, 1), F32),
        ],
        scratch_shapes=[
            pltpu.VMEM((TM_ROUTER, TM_ROUTER), BF16),
            pltpu.VMEM((CLASS_ROWS, 1), F32),
        ],
        compiler_params=_params(("arbitrary",)),
        name="route",
    )(x, wrt, rb)


def _start_all(copies):
    for r, cp in enumerate(copies):
        cp.start(priority=r % 2)


def _row_copies(pay, slot, dst_hbm, dest_ref, base, sem):
    return [pltpu.make_async_copy(pay.at[slot, r], dst_hbm.at[dest_ref[base + r]], sem.at[slot])
            for r in range(TD)]


def _dispatch_kernel(dest_ref, zstart_ref, x_ref, gw_ref, xs_hbm, pay, zbuf, zsem, sem):
    i = pl.program_id(0)
    last = pl.num_programs(0) - 1

    @pl.when(i == 0)
    def _():
        zbuf[...] = jnp.zeros_like(zbuf)

        def zero_tile(c):
            start = pl.multiple_of(zstart_ref[c], TM_G)
            return pltpu.make_async_copy(zbuf, xs_hbm.at[pl.ds(start, TM_G)], zsem)

        for c in range(N_ZERO_TILES):
            @pl.when(zstart_ref[c] >= 0)
            def _():
                zero_tile(c).start()
        for c in range(N_ZERO_TILES):
            @pl.when(zstart_ref[c] >= 0)
            def _():
                zero_tile(c).wait()

    for parity in range(2):
        @pl.when(i % 2 == parity)
        def _():
            pay[parity, :, :D_MODEL] = x_ref[...]
            gates = jnp.concatenate(
                [gw_ref[...], jnp.zeros((LANE - GATE_ROWS, TD), F32)], axis=0)
            pay[parity, :, D_MODEL:] = gates.T
            _start_all(_row_copies(pay, parity, xs_hbm, dest_ref, i * TD, sem))

            @pl.when(i > 0)
            def _():
                for cp in _row_copies(pay, 1 - parity, xs_hbm, dest_ref, (i - 1) * TD, sem):
                    cp.wait()

            @pl.when(i == last)
            def _():
                for cp in _row_copies(pay, parity, xs_hbm, dest_ref, i * TD, sem):
                    cp.wait()


def _dispatch(dest, zstart, x, gw):
    return pl.pallas_call(
        _dispatch_kernel,
        grid_spec=pltpu.PrefetchScalarGridSpec(
            num_scalar_prefetch=2,
            grid=(T // TD,),
            in_specs=[
                pl.BlockSpec((TD, D_MODEL), lambda i, d, z: (i, 0)),
                pl.BlockSpec((GATE_ROWS, TD), lambda i, d, z: (0, i)),
            ],
            out_specs=pl.BlockSpec(memory_space=pl.ANY),
            scratch_shapes=[
                pltpu.VMEM((2, TD, XG_W), F32),
                pltpu.VMEM((TM_G, XG_W), F32),
                pltpu.SemaphoreType.DMA,
                pltpu.SemaphoreType.DMA((2,)),
            ],
        ),
        out_shape=jax.ShapeDtypeStruct((P_MAX, XG_W), F32),
        compiler_params=_params(("arbitrary",)),
        name="dispatch",
    )(dest, zstart, x, gw)


def _experts_kernel(te_ref, nt_ref, xs_ref, wga_ref, wua_ref, wda_ref, wgb_ref, wub_ref, wdb_ref,
                    ys_ref):
    i = pl.program_id(0)

    @pl.when(i < nt_ref[0])
    def _():
        xb = xs_ref[:, :D_MODEL].astype(BF16)
        gates = xs_ref[:, D_MODEL:]
        y = None
        for which, (wg_ref, wu_ref, wd_ref) in enumerate(((wga_ref, wua_ref, wda_ref),
                                                          (wgb_ref, wub_ref, wdb_ref))):
            hg = jnp.dot(xb, wg_ref[0, 0], preferred_element_type=F32)
            hu = jnp.dot(xb, wu_ref[0, 0], preferred_element_type=F32)
            h = hg * _sigmoid(hg) * hu * gates[:, which:which + 1]
            part = jnp.dot(h.astype(BF16), wd_ref[0, 0], preferred_element_type=F32)
            y = part if y is None else y + part
        ys_ref[...] = y

    @pl.when(i >= nt_ref[0])
    def _():
        ys_ref[...] = jnp.zeros_like(ys_ref)


def _experts(tile_e, n_tiles, xs, wg, wu, wd, layer):
    def row_map(i, te, nt):
        return (jnp.minimum(i, nt[0] - 1), 0)

    def w_map(which):
        return lambda i, te, nt: (layer, te[which * NT_MAX + i], 0, 0)

    up_spec = lambda which: pl.BlockSpec((1, 1, D_MODEL, D_EXPERT), w_map(which))
    down_spec = lambda which: pl.BlockSpec((1, 1, D_EXPERT, D_MODEL), w_map(which))
    return pl.pallas_call(
        _experts_kernel,
        grid_spec=pltpu.PrefetchScalarGridSpec(
            num_scalar_prefetch=2,
            grid=(NT_MAX,),
            in_specs=[
                pl.BlockSpec((TM_G, XG_W), row_map),
                up_spec(0), up_spec(0), down_spec(0),
                up_spec(1), up_spec(1), down_spec(1),
            ],
            out_specs=pl.BlockSpec((TM_G, D_MODEL), lambda i, te, nt: (i, 0)),
        ),
        out_shape=jax.ShapeDtypeStruct((P_MAX, D_MODEL), F32),
        compiler_params=_params(("arbitrary",)),
        name="experts",
    )(tile_e, n_tiles, xs, wg, wu, wd, wg, wu, wd)


def _gather_copies(ys_hbm, buf, dest_ref, base, slot, sem):
    return [pltpu.make_async_copy(ys_hbm.at[dest_ref[base + r]], buf.at[slot, r], sem.at[slot])
            for r in range(TD)]


def _combine_kernel(dest_ref, x_ref, ys_hbm, lg_ref, lb_ref, o_ref, buf, sem):
    i = pl.program_id(0)
    last = pl.num_programs(0) - 1

    @pl.when(i == 0)
    def _():
        _start_all(_gather_copies(ys_hbm, buf, dest_ref, 0, 0, sem))

    for parity in range(2):
        @pl.when(i % 2 == parity)
        def _():
            @pl.when(i < last)
            def _():
                _start_all(_gather_copies(ys_hbm, buf, dest_ref, (i + 1) * TD, 1 - parity, sem))

            for cp in _gather_copies(ys_hbm, buf, dest_ref, i * TD, parity, sem):
                cp.wait()
            o_ref[...] = _layer_norm(ALPHA * x_ref[...] + buf[parity], lg_ref[...], lb_ref[...])


def _combine(dest, x, ys, lg, lb):
    vec = pl.BlockSpec((1, D_MODEL), lambda i, d: (0, 0))
    return pl.pallas_call(
        _combine_kernel,
        grid_spec=pltpu.PrefetchScalarGridSpec(
            num_scalar_prefetch=1,
            grid=(T // TD,),
            in_specs=[
                pl.BlockSpec((TD, D_MODEL), lambda i, d: (i, 0)),
                pl.BlockSpec(memory_space=pl.ANY),
                vec, vec,
            ],
            out_specs=pl.BlockSpec((TD, D_MODEL), lambda i, d: (i, 0)),
            scratch_shapes=[
                pltpu.VMEM((2, TD, D_MODEL), F32),
                pltpu.SemaphoreType.DMA((2,)),
            ],
        ),
        out_shape=jax.ShapeDtypeStruct((T, D_MODEL), F32),
        compiler_params=_params(("arbitrary",)),
        name="combine",
    )(dest, x, ys, lg, lb)


def _moe(x, wrt, rb, wg, wu, wd, layer, lg, lb):
    gw, cls, pos, tot = _route(x, wrt, rb)
    counts = tot[:N_CLASSES, 0].astype(jnp.int32)
    padded = (counts + TM_G - 1) // TM_G * TM_G
    ends = jnp.cumsum(padded)
    starts = ends - padded
    dest = starts[cls[0]] + pos[0]
    n_tiles = ends[-1] // TM_G
    tail = jnp.arange(NT_MIN, NT_MAX)
    zstart = jnp.concatenate([jnp.where(padded > 0, ends - TM_G, -1),
                              jnp.where(tail >= n_tiles, tail * TM_G, -1)])
    tile = jnp.minimum(jnp.arange(NT_MAX), n_tiles - 1)
    tile_cls = jnp.sum(tile[:, None] * TM_G >= ends[None, :], axis=1)
    tile_e = jnp.asarray(CLASS_EXPERTS, jnp.int32)[tile_cls].T.reshape(-1)
    xs = _dispatch(dest, zstart, x, gw)
    ys = _experts(tile_e, n_tiles.reshape(1), xs, wg, wu, wd, layer)
    return _combine(dest, x, ys, lg, lb)


_NT = (((1,), (1,)), ((), ()))


def _q_proj_kernel(x_ref, wt_ref, o_ref, *, scale):
    y = lax.dot_general(wt_ref[...], x_ref[...].astype(BF16), _NT, preferred_element_type=F32)
    o_ref[...] = (y * scale).astype(BF16)


def _q_proj(x, wt, scale):
    return pl.pallas_call(
        functools.partial(_q_proj_kernel, scale=scale),
        grid=(T // TM_PROJ,),
        in_specs=[
            pl.BlockSpec((TM_PROJ, D_MODEL), lambda i: (i, 0)),
            pl.BlockSpec((QK_WIDTH, D_MODEL), lambda i: (0, 0)),
        ],
        out_specs=pl.BlockSpec((QK_WIDTH, TM_PROJ), lambda i: (0, i)),
        out_shape=jax.ShapeDtypeStruct((QK_WIDTH, T), BF16),
        compiler_params=_params(("parallel",)),
        name="q_proj",
    )(x, wt)


def _kv_proj_kernel(x_ref, wk_ref, wvt_ref, k_ref, vt_ref):
    xb = x_ref[...].astype(BF16)
    k_ref[...] = jnp.dot(xb, wk_ref[...], preferred_element_type=F32).astype(BF16)
    vt_ref[0] = lax.dot_general(wvt_ref[...], xb, _NT, preferred_element_type=F32).astype(BF16)


def _kv_proj(x, wk, wvt):
    return pl.pallas_call(
        _kv_proj_kernel,
        grid=(T // TK,),
        in_specs=[
            pl.BlockSpec((TK, D_MODEL), lambda i: (i, 0)),
            pl.BlockSpec((D_MODEL, QK_WIDTH), lambda i: (0, 0)),
            pl.BlockSpec((V_WIDTH, D_MODEL), lambda i: (0, 0)),
        ],
        out_specs=[
            pl.BlockSpec((TK, QK_WIDTH), lambda i: (i, 0)),
            pl.BlockSpec((1, V_WIDTH, TK), lambda i: (i, 0, 0)),
        ],
        out_shape=[
            jax.ShapeDtypeStruct((T, QK_WIDTH), BF16),
            jax.ShapeDtypeStruct((T // TK, V_WIDTH, TK), BF16),
        ],
        compiler_params=_params(("parallel",)),
        name="kv_proj",
    )(x, wk, wvt)


def _proj_ln_kernel(a_ref, x_ref, w_ref, g_ref, b_ref, o_ref):
    mix = jnp.dot(a_ref[...], w_ref[...], preferred_element_type=F32)
    o_ref[...] = _layer_norm(ALPHA * x_ref[...] + mix, g_ref[...], b_ref[...])


def _proj_ln(a, x, w, g, b):
    vec = pl.BlockSpec((1, D_MODEL), lambda i: (0, 0))
    return pl.pallas_call(
        _proj_ln_kernel,
        grid=(T // TM_PROJ,),
        in_specs=[
            pl.BlockSpec((TM_PROJ, V_WIDTH), lambda i: (i, 0)),
            pl.BlockSpec((TM_PROJ, D_MODEL), lambda i: (i, 0)),
            pl.BlockSpec((V_WIDTH, D_MODEL), lambda i: (0, 0)),
            vec, vec,
        ],
        out_specs=pl.BlockSpec((TM_PROJ, D_MODEL), lambda i: (i, 0)),
        out_shape=jax.ShapeDtypeStruct((T, D_MODEL), F32),
        compiler_params=_params(("parallel",)),
        name="proj_ln",
    )(a, x, w, g, b)


def _bias_kernel(rb_ref, o_ref):
    h = pl.program_id(0)
    max_exact = N_BUCKETS // 2
    far = rb_ref[N_BUCKETS - 1, h]
    c = lax.broadcasted_iota(jnp.int32, (TK, TQ), 0)
    r = lax.broadcasted_iota(jnp.int32, (TK, TQ), 1)
    for kind in range(2):
        rel = kind * TK + r - c
        n = jnp.maximum(rel, 0)
        nf = jnp.maximum(n, 1).astype(F32)
        large = max_exact + (jnp.log(nf / max_exact) / math.log(MAX_DISTANCE / max_exact)
                             * (N_BUCKETS - max_exact)).astype(jnp.int32)
        large = jnp.minimum(large, N_BUCKETS - 1)
        bucket = jnp.where(n < max_exact, n, large)
        bias = jnp.zeros((TK, TQ), F32)
        for b in range(N_BUCKETS):
            bias = jnp.where(bucket == b, rb_ref[b, h] - far, bias)
        o_ref[0, kind] = jnp.where(rel >= 0, bias * LOG2E, NEG_BIG)


def _bias_tiles(rel_bias):
    return pl.pallas_call(
        _bias_kernel,
        grid=(N_HEADS,),
        in_specs=[pl.BlockSpec(memory_space=pltpu.SMEM)],
        out_specs=pl.BlockSpec((1, 2, TK, TQ), lambda h: (h, 0, 0, 0)),
        out_shape=jax.ShapeDtypeStruct((N_HEADS, 2, TK, TQ), F32),
        compiler_params=_params(("parallel",)),
        name="bias_tiles",
    )(rel_bias)


def _attn_kernel(q1t_ref, q2t_ref, q1n_ref, q2n_ref, k1_ref, k2_ref, vt_ref, bias_ref, lam_ref,
                 sg_ref, o_ref, s_ref, mx_ref, m_ref, l_ref, acc_ref, *, lambda_init):
    qi = pl.program_id(2)
    dim = lax.broadcasted_iota(jnp.int32, (LANE, 1), 0)
    lo = dim < HEAD_DIM

    def head_masked(q1, q2):
        zero = jnp.zeros_like(q1)
        return [jnp.where(lo, q1, zero), jnp.where(lo, q2, zero),
                jnp.where(lo, zero, q1), jnp.where(lo, zero, q2)]

    qs = head_masked(q1t_ref[...], q2t_ref[...])
    qs_next = head_masked(q1n_ref[...], q2n_ref[...])

    m_ref[...] = jnp.full(m_ref.shape, NEG_BIG, F32)
    l_ref[...] = jnp.zeros(l_ref.shape, F32)
    acc_ref[...] = jnp.zeros(acc_ref.shape, F32)

    def scores(j, slot, q_slots=qs):
        k0 = pl.multiple_of(j * TK, TK)
        k_ref = k1_ref if slot % 2 == 0 else k2_ref
        st = jnp.dot(k_ref[pl.ds(k0, TK), :], q_slots[slot], preferred_element_type=F32)
        s_ref[slot] = st
        mx_ref[slot] = jnp.max(st, axis=0, keepdims=True)

    def update(j, slot, kind):
        head = slot // 2
        if kind == 1:
            rows = slice(TK - LANE, TK)
            corner = s_ref[slot, rows, 0:LANE] + bias_ref[head, 1, rows, 0:LANE]
            s_ref[slot, rows, 0:LANE] = corner
            mx_ref[slot, :, 0:LANE] = jnp.maximum(mx_ref[slot, :, 0:LANE],
                                                  jnp.max(corner, axis=0, keepdims=True))
        st = s_ref[slot]
        if kind == 0:
            st = st + bias_ref[head, 0]
            mx = jnp.max(st, axis=0, keepdims=True)
        else:
            mx = mx_ref[slot]
        m_prev = m_ref[slot]
        m_new = jnp.maximum(m_prev, mx)
        alpha = jnp.exp2(m_prev - m_new)
        p = jnp.exp2(st - m_new)
        l_ref[slot] = alpha * l_ref[slot] + jnp.sum(p, axis=0, keepdims=True)
        vt = vt_ref[j, head * V_DIM:(head + 1) * V_DIM, :]
        acc_ref[slot] = alpha * acc_ref[slot] + jnp.dot(vt, p.astype(BF16),
                                                        preferred_element_type=F32)
        m_ref[slot] = m_new

    def tile(j, kind, nxt):
        for slot in range(4):
            ahead = slot + 2
            if ahead < 4:
                scores(j, ahead)
            elif nxt is None:
                scores(0, ahead - 4, qs_next)
            else:
                scores(nxt, ahead - 4)
            update(j, slot, kind)

    @pl.when(qi == 0)
    def _():
        scores(0, 0)
        scores(0, 1)

    n_far = jnp.maximum(qi - 1, 0)

    def far_pair(t, carry):
        tile(2 * t, None, 2 * t + 1)
        tile(2 * t + 1, None, 2 * t + 2)
        return carry

    lax.fori_loop(0, n_far // 2, far_pair, 0)

    @pl.when(n_far % 2 == 1)
    def _():
        tile(n_far - 1, None, n_far)

    @pl.when(qi >= 1)
    def _():
        tile(qi - 1, 1, qi)

    tile(qi, 0, None)

    lp = lam_ref[...]
    lam = (jnp.exp(jnp.sum(lp[0:1] * lp[1:2], axis=1, keepdims=True))
           - jnp.exp(jnp.sum(lp[2:3] * lp[3:4], axis=1, keepdims=True)) + lambda_init)
    for head in range(2):
        a1 = acc_ref[2 * head] / l_ref[2 * head]
        a2 = acc_ref[2 * head + 1] / l_ref[2 * head + 1]
        of = (a1 - lam * a2).T
        of = of * lax.rsqrt(jnp.mean(of * of, axis=-1, keepdims=True) + LN_EPS) * sg_ref[...]
        of = of * (1.0 - lambda_init)
        o_ref[:, head * V_DIM:(head + 1) * V_DIM] = of.astype(o_ref.dtype)


def _attention(qt, k, vt, bias, lam_params, subln_g, lambda_init):
    nq = SEQ // TQ
    nk = SEQ // TK
    qk_blocks = QK_WIDTH // 2 // LANE
    return pl.pallas_call(
        functools.partial(_attn_kernel, lambda_init=lambda_init),
        grid=(BATCH, N_PAIRS, nq),
        in_specs=[
            pl.BlockSpec((LANE, TQ), lambda b, p, i: (p, b * nq + i)),
            pl.BlockSpec((LANE, TQ), lambda b, p, i: (qk_blocks + p, b * nq + i)),
            pl.BlockSpec((LANE, TQ), lambda b, p, i: (p, b * nq + jnp.minimum(i + 1, nq - 1))),
            pl.BlockSpec((LANE, TQ),
                         lambda b, p, i: (qk_blocks + p, b * nq + jnp.minimum(i + 1, nq - 1))),
            pl.BlockSpec((SEQ, LANE), lambda b, p, i: (b, p)),
            pl.BlockSpec((SEQ, LANE), lambda b, p, i: (b, qk_blocks + p)),
            pl.BlockSpec((nk, 2 * V_DIM, TK), lambda b, p, i: (b, p, 0)),
            pl.BlockSpec((2, 2, TK, TQ), lambda b, p, i: (p, 0, 0, 0)),
            pl.BlockSpec((4, HEAD_DIM), lambda b, p, i: (0, 0)),
            pl.BlockSpec((1, V_DIM), lambda b, p, i: (0, 0)),
        ],
        out_specs=pl.BlockSpec((TQ, 2 * V_DIM), lambda b, p, i: (b * nq + i, p)),
        out_shape=jax.ShapeDtypeStruct((T, V_WIDTH), BF16),
        scratch_shapes=[
            pltpu.VMEM((4, TK, TQ), F32),
            pltpu.VMEM((4, 1, TQ), F32),
            pltpu.VMEM((4, 1, TQ), F32),
            pltpu.VMEM((4, 1, TQ), F32),
            pltpu.VMEM((4, V_DIM, TQ), F32),
        ],
        compiler_params=_params(("parallel", "parallel", "arbitrary"), vmem_mb=56),
        name="diff_attention",
    )(qt, qt, qt, qt, k, k, vt, bias, lam_params, subln_g)


def kernel(x, a_w_pw1, a_b_pw1, a_w_dw, a_b_dw, a_ln_g, a_ln_b, a_w_pw2, a_b_pw2, w_kv, b_w_q,
           b_lambda, b_subln_g, b_w_o, rel_bias, ln_mix_g, ln_mix_b, ln_ffn_g, ln_ffn_b,
           router_w, router_bias, moe_w_gate, moe_w_up, moe_w_down):
    x = x.reshape(T, D_MODEL)
    row = lambda v: v.reshape(1, -1)
    wr_hi = router_w.T.astype(BF16)
    wr_mid = (router_w.T - wr_hi.astype(F32)).astype(BF16)
    wrt = jnp.concatenate([wr_hi, wr_mid], axis=0)
    rb = router_bias.reshape(N_EXPERTS, 1)
    k_all = vt_all = bias = None
    wg_all = moe_w_gate.astype(BF16)
    wu_all = moe_w_up.astype(BF16)
    wd_all = moe_w_down.astype(BF16)
    for l in range(DEPTH):
        if l < N_A:
            g = _glu(x, a_w_pw1[l].astype(BF16), row(a_b_pw1[l]))
            wdw = jnp.pad(a_w_dw[l], ((0, CONV_HALO - CONV_WIDTH), (0, 0)))
            x = _conv_back(g, x, wdw, row(a_b_dw[l]), row(a_ln_g[l]), row(a_ln_b[l]),
                           a_w_pw2[l].astype(BF16), row(a_b_pw2[l]),
                           row(ln_mix_g[l]), row(ln_mix_b[l]))
        else:
            if l == N_A:
                k_all, vt_all = _kv_proj(x, w_kv[:, :QK_WIDTH].astype(BF16),
                                         w_kv[:, QK_WIDTH:].T.astype(BF16))
                bias = _bias_tiles(rel_bias)
            j = l - N_A
            lambda_init = 0.8 - 0.6 * math.exp(-0.3 * l)
            qt = _q_proj(x, b_w_q[j].T.astype(BF16), HEAD_DIM ** -0.5 * LOG2E)
            o = _attention(qt, k_all, vt_all, bias, b_lambda[j], row(b_subln_g[j]), lambda_init)
            x = _proj_ln(o, x, b_w_o[j].astype(BF16), row(ln_mix_g[l]), row(ln_mix_b[l]))
        x = _moe(x, wrt, rb, wg_all, wu_all, wd_all, l, row(ln_ffn_g[l]), row(ln_ffn_b[l]))
    return x.reshape(BATCH, SEQ, D_MODEL)
```

```python
import functools
import math

import jax
import jax.numpy as jnp
from jax import lax
from jax.experimental import pallas as pl
from jax.experimental.pallas import tpu as pltpu

D_MODEL = 1024
BATCH = 2
SEQ = 8192
DEPTH = 4
N_A = DEPTH // 2
CONV_WIDTH = 31
N_HEADS = 8
HEAD_DIM = 64
V_DIM = 2 * HEAD_DIM
QK_WIDTH = 2 * N_HEADS * HEAD_DIM
V_WIDTH = N_HEADS * V_DIM
N_BUCKETS = 32
MAX_DISTANCE = 128
N_EXPERTS = 16
N_GROUPS = 4
EXPERTS_PER_GROUP = N_EXPERTS // N_GROUPS
D_EXPERT = 512
ALPHA = (2.0 * DEPTH) ** 0.25
LN_EPS = 1e-5

T = BATCH * SEQ
F32 = jnp.float32
BF16 = jnp.bfloat16
LOG2E = 1.4426950408889634
NEG_BIG = -1e30
LANE = 128
CONV_HALO = 32
N_PAIRS = N_HEADS // 2

TM_GLU = 512
TM_CONV = 256
TM_ROUTER = 1024
TM_MOE = 1024
TM_PROJ = 512
TQ = 512
TK = 512
FAR_UNROLL = 4


def _params(sem, vmem_mb=None, flags=None):
    kw = dict(dimension_semantics=sem)
    if vmem_mb is not None:
        kw["vmem_limit_bytes"] = vmem_mb * 1024 * 1024
    if flags is not None:
        kw["flags"] = flags
    return pltpu.CompilerParams(**kw)


def _layer_norm(v, g, b):
    mu = jnp.mean(v, axis=-1, keepdims=True)
    d = v - mu
    var = jnp.mean(d * d, axis=-1, keepdims=True)
    return d * lax.rsqrt(var + LN_EPS) * g + b


def _sigmoid(v):
    return 1.0 / (1.0 + jnp.exp(-v))


def _glu_kernel(x_ref, w_ref, b_ref, o_ref):
    xb = x_ref[...].astype(BF16)
    a = jnp.dot(xb, w_ref[:, :D_MODEL], preferred_element_type=F32) + b_ref[:, :D_MODEL]
    gate = jnp.dot(xb, w_ref[:, D_MODEL:], preferred_element_type=F32) + b_ref[:, D_MODEL:]
    o_ref[...] = a * _sigmoid(gate)


def _glu(x, w, b):
    return pl.pallas_call(
        _glu_kernel,
        grid=(T // TM_GLU,),
        in_specs=[
            pl.BlockSpec((TM_GLU, D_MODEL), lambda i: (i, 0)),
            pl.BlockSpec((D_MODEL, 2 * D_MODEL), lambda i: (0, 0)),
            pl.BlockSpec((1, 2 * D_MODEL), lambda i: (0, 0)),
        ],
        out_specs=pl.BlockSpec((TM_GLU, D_MODEL), lambda i: (i, 0)),
        out_shape=jax.ShapeDtypeStruct((T, D_MODEL), F32),
        compiler_params=_params(("parallel",)),
        name="glu_front",
    )(x, w, b)


SUBLANES = 8
CONV_BASE = CONV_HALO - (CONV_WIDTH - 1)


def _conv_kernel(g_ref, halo_ref, x_ref, wdw_ref, bdw_ref, lng_ref, lnb_ref, w2_ref, b2_ref,
                 mg_ref, mb_ref, o_ref, buf_ref, cv_ref):
    i = pl.program_id(0)
    first = (i % (SEQ // TM_CONV)) == 0
    buf_ref[0:CONV_HALO, :] = jnp.where(first, 0.0, halo_ref[...])
    buf_ref[CONV_HALO:CONV_HALO + TM_CONV, :] = g_ref[...]
    buf_ref[CONV_HALO + TM_CONV:, :] = jnp.zeros((SUBLANES, D_MODEL), F32)
    sub = lax.broadcasted_iota(jnp.int32, (SUBLANES, LANE), 0)
    n_q = (CONV_BASE + CONV_WIDTH - 1) // SUBLANES + 1
    for c in range(D_MODEL // LANE):
        cs = slice(c * LANE, (c + 1) * LANE)
        w_b = [jnp.broadcast_to(wdw_ref[j:j + 1, cs], (SUBLANES, LANE)) for j in range(CONV_WIDTH)]
        bias = jnp.broadcast_to(bdw_ref[:, cs], (SUBLANES, LANE))

        def partials(v):
            tiles = [buf_ref[pl.ds(v + SUBLANES * q, SUBLANES), cs] for q in range(n_q)]
            out = []
            for s in range(SUBLANES):
                a = None
                for q in range(n_q):
                    j = SUBLANES * q + s - CONV_BASE
                    if 0 <= j < CONV_WIDTH:
                        term = w_b[j] * tiles[q]
                        a = term if a is None else a + term
                out.append(a)
            return tuple(out)

        def row_body(t, prev):
            r = pl.multiple_of(t * SUBLANES, SUBLANES)
            cur = partials(pl.multiple_of(r + SUBLANES, SUBLANES))
            acc = bias + prev[0]
            for s in range(1, SUBLANES):
                mixed = jnp.where(sub >= s, prev[s], cur[s])
                acc = acc + pltpu.roll(mixed, SUBLANES - s, axis=0)
            cv_ref[pl.ds(r, SUBLANES), cs] = acc
            return cur

        lax.fori_loop(0, TM_CONV // SUBLANES, row_body, partials(0), unroll=2)
    h = _layer_norm(cv_ref[...], lng_ref[...], lnb_ref[...])
    h = h * _sigmoid(h)
    mix = jnp.dot(h.astype(BF16), w2_ref[...], preferred_element_type=F32) + b2_ref[...]
    o_ref[...] = _layer_norm(ALPHA * x_ref[...] + mix, mg_ref[...], mb_ref[...])


def _conv_back(g, x, wdw, bdw, lng, lnb, w2, b2, mg, mb):
    row = lambda i: (i, 0)
    fixed = lambda i: (0, 0)
    vec = pl.BlockSpec((1, D_MODEL), fixed)
    halo_blocks = TM_CONV // CONV_HALO
    return pl.pallas_call(
        _conv_kernel,
        grid=(T // TM_CONV,),
        in_specs=[
            pl.BlockSpec((TM_CONV, D_MODEL), row),
            pl.BlockSpec((CONV_HALO, D_MODEL), lambda i: (jnp.maximum(i * halo_blocks - 1, 0), 0)),
            pl.BlockSpec((TM_CONV, D_MODEL), row),
            pl.BlockSpec((CONV_HALO, D_MODEL), fixed),
            vec, vec, vec,
            pl.BlockSpec((D_MODEL, D_MODEL), fixed),
            vec, vec, vec,
        ],
        out_specs=pl.BlockSpec((TM_CONV, D_MODEL), row),
        out_shape=jax.ShapeDtypeStruct((T, D_MODEL), F32),
        scratch_shapes=[
            pltpu.VMEM((TM_CONV + CONV_HALO + SUBLANES, D_MODEL), F32),
            pltpu.VMEM((TM_CONV, D_MODEL), F32),
        ],
        compiler_params=_params(("parallel",)),
        name="conv_back",
    )(g, g, x, wdw, bdw, lng, lnb, w2, b2, mg, mb)


def _ranks_before(vals):
    n = len(vals)
    ranks = []
    for j in range(n):
        r = jnp.zeros_like(vals[j])
        for i in range(n):
            if i == j:
                continue
            before = (vals[i] > vals[j]) | ((vals[i] == vals[j]) & (i < j))
            r = r + before.astype(F32)
        ranks.append(r)
    return ranks


PAIRS = [(a, b) for a in range(EXPERTS_PER_GROUP) for b in range(a + 1, EXPERTS_PER_GROUP)]
N_CLASSES = N_GROUPS * len(PAIRS)
CLASS_ROWS = 32
CLASS_EXPERTS = [(g * EXPERTS_PER_GROUP + a, g * EXPERTS_PER_GROUP + b)
                 for g in range(N_GROUPS) for (a, b) in PAIRS]
TM_G = 256
NT_MAX = -(-(T + N_CLASSES * (TM_G - 1)) // TM_G)
P_MAX = NT_MAX * TM_G
XG_W = D_MODEL + LANE
GATE_ROWS = 8
TD = 256
NT_MIN = T // TM_G
N_ZERO_TILES = N_CLASSES + NT_MAX - NT_MIN


def _route_kernel(x_ref, wrt_ref, rb_ref, gw_ref, cls_ref, pos_ref, tot_ref, tri_ref, carry_ref):
    i = pl.program_id(0)

    @pl.when(i == 0)
    def _():
        r = lax.broadcasted_iota(jnp.int32, (TM_ROUTER, TM_ROUTER), 0)
        c = lax.broadcasted_iota(jnp.int32, (TM_ROUTER, TM_ROUTER), 1)
        tri_ref[...] = jnp.where(r < c, 1.0, 0.0).astype(BF16)
        carry_ref[...] = jnp.zeros_like(carry_ref)

    x = x_ref[...]
    x_hi = x.astype(BF16)
    x_mid = (x - x_hi.astype(F32)).astype(BF16)
    both = lax.dot_general(wrt_ref[...], x_hi, _NT, preferred_element_type=F32)
    logits = (both[:N_EXPERTS] + both[N_EXPERTS:]
              + lax.dot_general(wrt_ref[:N_EXPERTS, :], x_mid, _NT, preferred_element_type=F32))
    aff = _sigmoid(logits)
    sel = aff + rb_ref[...]
    aff_rows = [aff[e:e + 1, :] for e in range(N_EXPERTS)]
    sel_rows = [sel[e:e + 1, :] for e in range(N_EXPERTS)]
    in_top2 = []
    scores = []
    for g in range(N_GROUPS):
        members = sel_rows[g * EXPERTS_PER_GROUP:(g + 1) * EXPERTS_PER_GROUP]
        ranks = _ranks_before(members)
        top = [r < 2.0 for r in ranks]
        in_top2.extend(top)
        s = jnp.zeros_like(members[0])
        for v, t in zip(members, top):
            s = s + jnp.where(t, v, 0.0)
        scores.append(s)
    g_ranks = _ranks_before(scores)
    w_rows = []
    for e in range(N_EXPERTS):
        chosen = (g_ranks[e // EXPERTS_PER_GROUP] < 1.0) & in_top2[e]
        w_rows.append(jnp.where(chosen, aff_rows[e], 0.0))
    denom = w_rows[0]
    for e in range(1, N_EXPERTS):
        denom = denom + w_rows[e]
    inv = 1.0 / denom

    masks = []
    wa = jnp.zeros_like(denom)
    wb = jnp.zeros_like(denom)
    for c, (ea, eb) in enumerate(CLASS_EXPERTS):
        m = (g_ranks[c // len(PAIRS)] < 1.0) & in_top2[ea] & in_top2[eb]
        masks.append(m.astype(F32))
        wa = wa + jnp.where(m, aff_rows[ea], 0.0)
        wb = wb + jnp.where(m, aff_rows[eb], 0.0)
    zero_row = jnp.zeros_like(denom)
    onehot = jnp.concatenate(masks + [zero_row] * (CLASS_ROWS - N_CLASSES), axis=0)

    before = jnp.dot(onehot.astype(BF16), tri_ref[...], preferred_element_type=F32)
    carry = carry_ref[...]
    class_id = lax.broadcasted_iota(jnp.int32, (CLASS_ROWS, 1), 0).astype(F32)
    pos_ref[...] = jnp.sum(onehot * (before + carry), axis=0, keepdims=True).astype(jnp.int32)
    cls_ref[...] = jnp.sum(onehot * class_id, axis=0, keepdims=True).astype(jnp.int32)
    carry = carry + jnp.sum(onehot, axis=1, keepdims=True)
    carry_ref[...] = carry
    tot_ref[...] = carry

    gw_ref[...] = jnp.concatenate([wa * inv, wb * inv] + [zero_row] * (GATE_ROWS - 2), axis=0)


def _route(x, wrt, rb):
    return pl.pallas_call(
        _route_kernel,
        grid=(T // TM_ROUTER,),
        in_specs=[
            pl.BlockSpec((TM_ROUTER, D_MODEL), lambda i: (i, 0)),
            pl.BlockSpec((2 * N_EXPERTS, D_MODEL), lambda i: (0, 0)),
            pl.BlockSpec((N_EXPERTS, 1), lambda i: (0, 0)),
        ],
        out_specs=[
            pl.BlockSpec((GATE_ROWS, TM_ROUTER), lambda i: (0, i)),
            pl.BlockSpec((1, TM_ROUTER), lambda i: (0, i)),
            pl.BlockSpec((1, TM_ROUTER), lambda i: (0, i)),
            pl.BlockSpec((CLASS_ROWS, 1), lambda i: (0, 0)),
        ],
        out_shape=[
            jax.ShapeDtypeStruct((GATE_ROWS, T), F32),
            jax.ShapeDtypeStruct((1, T), jnp.int32),
            jax.ShapeDtypeStruct((1, T), jnp.int32),
            jax.ShapeDtypeStruct((CLASS_ROWS, 1), F32),
        ],
        scratch_shapes=[
            pltpu.VMEM((TM_ROUTER, TM_ROUTER), BF16),
            pltpu.VMEM((CLASS_ROWS, 1), F32),
        ],
        compiler_params=_params(("arbitrary",)),
        name="route",
    )(x, wrt, rb)


def _start_all(copies):
    for r, cp in enumerate(copies):
        cp.start(priority=r % 2)


def _row_copies(pay, slot, dst_hbm, dest_ref, base, sem):
    return [pltpu.make_async_copy(pay.at[slot, r], dst_hbm.at[dest_ref[base + r]], sem.at[slot])
            for r in range(TD)]


def _dispatch_kernel(dest_ref, zstart_ref, x_ref, gw_ref, xs_hbm, pay, zbuf, zsem, sem):
    i = pl.program_id(0)
    last = pl.num_programs(0) - 1

    @pl.when(i == 0)
    def _():
        zbuf[...] = jnp.zeros_like(zbuf)

        def zero_tile(c):
            start = pl.multiple_of(zstart_ref[c], TM_G)
            return pltpu.make_async_copy(zbuf, xs_hbm.at[pl.ds(start, TM_G)], zsem)

        for c in range(N_ZERO_TILES):
            @pl.when(zstart_ref[c] >= 0)
            def _():
                zero_tile(c).start()
        for c in range(N_ZERO_TILES):
            @pl.when(zstart_ref[c] >= 0)
            def _():
                zero_tile(c).wait()

    for parity in range(2):
        @pl.when(i % 2 == parity)
        def _():
            pay[parity, :, :D_MODEL] = x_ref[...]
            gates = jnp.concatenate(
                [gw_ref[...], jnp.zeros((LANE - GATE_ROWS, TD), F32)], axis=0)
            pay[parity, :, D_MODEL:] = gates.T
            _start_all(_row_copies(pay, parity, xs_hbm, dest_ref, i * TD, sem))

            @pl.when(i > 0)
            def _():
                for cp in _row_copies(pay, 1 - parity, xs_hbm, dest_ref, (i - 1) * TD, sem):
                    cp.wait()

            @pl.when(i == last)
            def _():
                for cp in _row_copies(pay, parity, xs_hbm, dest_ref, i * TD, sem):
                    cp.wait()


def _dispatch(dest, zstart, x, gw):
    return pl.pallas_call(
        _dispatch_kernel,
        grid_spec=pltpu.PrefetchScalarGridSpec(
            num_scalar_prefetch=2,
            grid=(T // TD,),
            in_specs=[
                pl.BlockSpec((TD, D_MODEL), lambda i, d, z: (i, 0)),
                pl.BlockSpec((GATE_ROWS, TD), lambda i, d, z: (0, i)),
            ],
            out_specs=pl.BlockSpec(memory_space=pl.ANY),
            scratch_shapes=[
                pltpu.VMEM((2, TD, XG_W), F32),
                pltpu.VMEM((TM_G, XG_W), F32),
                pltpu.SemaphoreType.DMA,
                pltpu.SemaphoreType.DMA((2,)),
            ],
        ),
        out_shape=jax.ShapeDtypeStruct((P_MAX, XG_W), F32),
        compiler_params=_params(("arbitrary",)),
        name="dispatch",
    )(dest, zstart, x, gw)


def _experts_kernel(te_ref, nt_ref, xs_ref, wga_ref, wua_ref, wda_ref, wgb_ref, wub_ref, wdb_ref,
                    ys_ref):
    i = pl.program_id(0)

    @pl.when(i < nt_ref[0])
    def _():
        xb = xs_ref[:, :D_MODEL].astype(BF16)
        gates = xs_ref[:, D_MODEL:]
        y = None
        for which, (wg_ref, wu_ref, wd_ref) in enumerate(((wga_ref, wua_ref, wda_ref),
                                                          (wgb_ref, wub_ref, wdb_ref))):
            hg = jnp.dot(xb, wg_ref[0, 0], preferred_element_type=F32)
            hu = jnp.dot(xb, wu_ref[0, 0], preferred_element_type=F32)
            h = hg * _sigmoid(hg) * hu * gates[:, which:which + 1]
            part = jnp.dot(h.astype(BF16), wd_ref[0, 0], preferred_element_type=F32)
            y = part if y is None else y + part
        ys_ref[...] = y

    @pl.when(i >= nt_ref[0])
    def _():
        ys_ref[...] = jnp.zeros_like(ys_ref)


def _experts(tile_e, n_tiles, xs, wg, wu, wd, layer):
    def row_map(i, te, nt):
        return (jnp.minimum(i, nt[0] - 1), 0)

    def w_map(which):
        return lambda i, te, nt: (layer, te[which * NT_MAX + i], 0, 0)

    up_spec = lambda which: pl.BlockSpec((1, 1, D_MODEL, D_EXPERT), w_map(which))
    down_spec = lambda which: pl.BlockSpec((1, 1, D_EXPERT, D_MODEL), w_map(which))
    return pl.pallas_call(
        _experts_kernel,
        grid_spec=pltpu.PrefetchScalarGridSpec(
            num_scalar_prefetch=2,
            grid=(NT_MAX,),
            in_specs=[
                pl.BlockSpec((TM_G, XG_W), row_map),
                up_spec(0), up_spec(0), down_spec(0),
                up_spec(1), up_spec(1), down_spec(1),
            ],
            out_specs=pl.BlockSpec((TM_G, D_MODEL), lambda i, te, nt: (i, 0)),
        ),
        out_shape=jax.ShapeDtypeStruct((P_MAX, D_MODEL), F32),
        compiler_params=_params(("arbitrary",)),
        name="experts",
    )(tile_e, n_tiles, xs, wg, wu, wd, wg, wu, wd)


def _gather_copies(ys_hbm, buf, dest_ref, base, slot, sem):
    return [pltpu.make_async_copy(ys_hbm.at[dest_ref[base + r]], buf.at[slot, r], sem.at[slot])
            for r in range(TD)]


def _combine_kernel(dest_ref, x_ref, ys_hbm, lg_ref, lb_ref, o_ref, buf, sem):
    i = pl.program_id(0)
    last = pl.num_programs(0) - 1

    @pl.when(i == 0)
    def _():
        _start_all(_gather_copies(ys_hbm, buf, dest_ref, 0, 0, sem))

    for parity in range(2):
        @pl.when(i % 2 == parity)
        def _():
            @pl.when(i < last)
            def _():
                _start_all(_gather_copies(ys_hbm, buf, dest_ref, (i + 1) * TD, 1 - parity, sem))

            for cp in _gather_copies(ys_hbm, buf, dest_ref, i * TD, parity, sem):
                cp.wait()
            o_ref[...] = _layer_norm(ALPHA * x_ref[...] + buf[parity], lg_ref[...], lb_ref[...])


def _combine(dest, x, ys, lg, lb):
    vec = pl.BlockSpec((1, D_MODEL), lambda i, d: (0, 0))
    return pl.pallas_call(
        _combine_kernel,
        grid_spec=pltpu.PrefetchScalarGridSpec(
            num_scalar_prefetch=1,
            grid=(T // TD,),
            in_specs=[
                pl.BlockSpec((TD, D_MODEL), lambda i, d: (i, 0)),
                pl.BlockSpec(memory_space=pl.ANY),
                vec, vec,
            ],
            out_specs=pl.BlockSpec((TD, D_MODEL), lambda i, d: (i, 0)),
            scratch_shapes=[
                pltpu.VMEM((2, TD, D_MODEL), F32),
                pltpu.SemaphoreType.DMA((2,)),
            ],
        ),
        out_shape=jax.ShapeDtypeStruct((T, D_MODEL), F32),
        compiler_params=_params(("arbitrary",)),
        name="combine",
    )(dest, x, ys, lg, lb)


TM_G_LOG2 = TM_G.bit_length() - 1
assert 1 << TM_G_LOG2 == TM_G


def _plan_kernel(cnt_ref, cls_ref, pos_ref, dest_ref, zstart_ref, te_ref, nt_ref):
    shr = lax.shift_right_logical
    run = jnp.int32(0)
    starts, ends = [], []
    for c in range(N_CLASSES):
        padded = shr(cnt_ref[c] + (TM_G - 1), TM_G_LOG2) * TM_G
        starts.append(run)
        run = run + padded
        ends.append(run)
        zstart_ref[c] = jnp.where(padded > 0, run - TM_G, -1)
    n_tiles = shr(run, TM_G_LOG2)
    nt_ref[0] = n_tiles
    for t in range(NT_MIN, NT_MAX):
        zstart_ref[N_CLASSES + t - NT_MIN] = jnp.where(t >= n_tiles, t * TM_G, -1)

    def tile_body(t, carry):
        row0 = jnp.minimum(t, n_tiles - 1) * TM_G
        ea = jnp.int32(0)
        eb = jnp.int32(0)
        for c in range(N_CLASSES):
            inside = (row0 >= starts[c]) & (row0 < ends[c])
            ea = jnp.where(inside, CLASS_EXPERTS[c][0], ea)
            eb = jnp.where(inside, CLASS_EXPERTS[c][1], eb)
        te_ref[t] = ea
        te_ref[NT_MAX + t] = eb
        return carry

    lax.fori_loop(0, NT_MAX, tile_body, 0)

    cls = cls_ref[...]
    dest = pos_ref[...]
    for c in range(N_CLASSES):
        dest = dest + jnp.where(cls == c, starts[c], 0)
    dest_ref[...] = dest


def _plan(counts, cls, pos):
    smem = pl.BlockSpec(memory_space=pltpu.SMEM)
    vmem = pl.BlockSpec(memory_space=pltpu.VMEM)
    return pl.pallas_call(
        _plan_kernel,
        in_specs=[smem, vmem, vmem],
        out_specs=[vmem, smem, smem, smem],
        out_shape=[
            jax.ShapeDtypeStruct((1, T), jnp.int32),
            jax.ShapeDtypeStruct((N_ZERO_TILES,), jnp.int32),
            jax.ShapeDtypeStruct((2 * NT_MAX,), jnp.int32),
            jax.ShapeDtypeStruct((1,), jnp.int32),
        ],
        name="plan",
    )(counts, cls, pos)


def _moe(x, wrt, rb, wg, wu, wd, layer, lg, lb):
    gw, cls, pos, tot = _route(x, wrt, rb)
    dest, zstart, tile_e, n_tiles = _plan(tot[:, 0].astype(jnp.int32), cls, pos)
    dest = dest.reshape(T)
    xs = _dispatch(dest, zstart, x, gw)
    ys = _experts(tile_e, n_tiles, xs, wg, wu, wd, layer)
    return _combine(dest, x, ys, lg, lb)


_NT = (((1,), (1,)), ((), ()))


def _q_proj_kernel(x_ref, wt_ref, o_ref, *, scale):
    y = lax.dot_general(wt_ref[...], x_ref[...].astype(BF16), _NT, preferred_element_type=F32)
    o_ref[...] = (y * scale).astype(BF16)


def _q_proj(x, wt, scale):
    return pl.pallas_call(
        functools.partial(_q_proj_kernel, scale=scale),
        grid=(T // TM_PROJ,),
        in_specs=[
            pl.BlockSpec((TM_PROJ, D_MODEL), lambda i: (i, 0)),
            pl.BlockSpec((QK_WIDTH, D_MODEL), lambda i: (0, 0)),
        ],
        out_specs=pl.BlockSpec((QK_WIDTH, TM_PROJ), lambda i: (0, i)),
        out_shape=jax.ShapeDtypeStruct((QK_WIDTH, T), BF16),
        compiler_params=_params(("parallel",)),
        name="q_proj",
    )(x, wt)


def _kv_proj_kernel(x_ref, wk_ref, wvt_ref, k_ref, vt_ref):
    xb = x_ref[...].astype(BF16)
    k_ref[...] = jnp.dot(xb, wk_ref[...], preferred_element_type=F32).astype(BF16)
    vt_ref[0] = lax.dot_general(wvt_ref[...], xb, _NT, preferred_element_type=F32).astype(BF16)


def _kv_proj(x, wk, wvt):
    return pl.pallas_call(
        _kv_proj_kernel,
        grid=(T // TK,),
        in_specs=[
            pl.BlockSpec((TK, D_MODEL), lambda i: (i, 0)),
            pl.BlockSpec((D_MODEL, QK_WIDTH), lambda i: (0, 0)),
            pl.BlockSpec((V_WIDTH, D_MODEL), lambda i: (0, 0)),
        ],
        out_specs=[
            pl.BlockSpec((TK, QK_WIDTH), lambda i: (i, 0)),
            pl.BlockSpec((1, V_WIDTH, TK), lambda i: (i, 0, 0)),
        ],
        out_shape=[
            jax.ShapeDtypeStruct((T, QK_WIDTH), BF16),
            jax.ShapeDtypeStruct((T // TK, V_WIDTH, TK), BF16),
        ],
        compiler_params=_params(("parallel",)),
        name="kv_proj",
    )(x, wk, wvt)


def _proj_ln_kernel(a_ref, x_ref, w_ref, g_ref, b_ref, o_ref):
    mix = jnp.dot(a_ref[...], w_ref[...], preferred_element_type=F32)
    o_ref[...] = _layer_norm(ALPHA * x_ref[...] + mix, g_ref[...], b_ref[...])


def _proj_ln(a, x, w, g, b):
    vec = pl.BlockSpec((1, D_MODEL), lambda i: (0, 0))
    return pl.pallas_call(
        _proj_ln_kernel,
        grid=(T // TM_PROJ,),
        in_specs=[
            pl.BlockSpec((TM_PROJ, V_WIDTH), lambda i: (i, 0)),
            pl.BlockSpec((TM_PROJ, D_MODEL), lambda i: (i, 0)),
            pl.BlockSpec((V_WIDTH, D_MODEL), lambda i: (0, 0)),
            vec, vec,
        ],
        out_specs=pl.BlockSpec((TM_PROJ, D_MODEL), lambda i: (i, 0)),
        out_shape=jax.ShapeDtypeStruct((T, D_MODEL), F32),
        compiler_params=_params(("parallel",)),
        name="proj_ln",
    )(a, x, w, g, b)


def _bias_kernel(rb_ref, diag_ref, corner_ref):
    h = pl.program_id(0)
    max_exact = N_BUCKETS // 2
    far = rb_ref[N_BUCKETS - 1, h]

    def tile(shape, offset):
        c = lax.broadcasted_iota(jnp.int32, shape, 0)
        r = lax.broadcasted_iota(jnp.int32, shape, 1)
        rel = offset + r - c
        n = jnp.maximum(rel, 0)
        nf = jnp.maximum(n, 1).astype(F32)
        large = max_exact + (jnp.log(nf / max_exact) / math.log(MAX_DISTANCE / max_exact)
                             * (N_BUCKETS - max_exact)).astype(jnp.int32)
        large = jnp.minimum(large, N_BUCKETS - 1)
        bucket = jnp.where(n < max_exact, n, large)
        bias = jnp.zeros(shape, F32)
        for b in range(N_BUCKETS):
            bias = jnp.where(bucket == b, rb_ref[b, h] - far, bias)
        return jnp.where(rel >= 0, bias * LOG2E, NEG_BIG)

    diag_ref[0] = tile((TK, TQ), 0)
    corner_ref[0] = tile((LANE, LANE), LANE)


def _bias_tiles(rel_bias):
    return pl.pallas_call(
        _bias_kernel,
        grid=(N_HEADS,),
        in_specs=[pl.BlockSpec(memory_space=pltpu.SMEM)],
        out_specs=[
            pl.BlockSpec((1, TK, TQ), lambda h: (h, 0, 0)),
            pl.BlockSpec((1, LANE, LANE), lambda h: (h, 0, 0)),
        ],
        out_shape=[
            jax.ShapeDtypeStruct((N_HEADS, TK, TQ), F32),
            jax.ShapeDtypeStruct((N_HEADS, LANE, LANE), F32),
        ],
        compiler_params=_params(("parallel",)),
        name="bias_tiles",
    )(rel_bias)


def _attn_kernel(q1t_ref, q2t_ref, q1n_ref, q2n_ref, k1_ref, k2_ref, vt_ref, diag_ref, corner_ref,
                 lam_ref, sg_ref, o_ref, s_ref, mx_ref, m_ref, l_ref, acc_ref, *, lambda_init):
    qi = pl.program_id(2)
    dim = lax.broadcasted_iota(jnp.int32, (LANE, 1), 0)
    lo = dim < HEAD_DIM

    def head_masked(q1, q2):
        zero = jnp.zeros_like(q1)
        return [jnp.where(lo, q1, zero), jnp.where(lo, q2, zero),
                jnp.where(lo, zero, q1), jnp.where(lo, zero, q2)]

    qs = head_masked(q1t_ref[...], q2t_ref[...])
    qs_next = head_masked(q1n_ref[...], q2n_ref[...])

    m_ref[...] = jnp.full(m_ref.shape, NEG_BIG, F32)
    l_ref[...] = jnp.zeros(l_ref.shape, F32)
    acc_ref[...] = jnp.zeros(acc_ref.shape, F32)

    def scores(j, slot, q_slots=qs):
        k0 = pl.multiple_of(j * TK, TK)
        k_ref = k1_ref if slot % 2 == 0 else k2_ref
        st = jnp.dot(k_ref[pl.ds(k0, TK), :], q_slots[slot], preferred_element_type=F32)
        s_ref[slot] = st
        mx_ref[slot] = jnp.max(st, axis=0, keepdims=True)

    def update(j, slot, kind):
        head = slot // 2
        if kind == 1:
            rows = slice(TK - LANE, TK)
            corner = s_ref[slot, rows, 0:LANE] + corner_ref[head]
            s_ref[slot, rows, 0:LANE] = corner
            mx_ref[slot, :, 0:LANE] = jnp.maximum(mx_ref[slot, :, 0:LANE],
                                                  jnp.max(corner, axis=0, keepdims=True))
        st = s_ref[slot]
        if kind == 0:
            st = st + diag_ref[head]
            mx = jnp.max(st, axis=0, keepdims=True)
        else:
            mx = mx_ref[slot]
        m_prev = m_ref[slot]
        m_new = jnp.maximum(m_prev, mx)
        alpha = jnp.exp2(m_prev - m_new)
        p = jnp.exp2(st - m_new)
        l_ref[slot] = alpha * l_ref[slot] + jnp.sum(p, axis=0, keepdims=True)
        vt = vt_ref[j, head * V_DIM:(head + 1) * V_DIM, :]
        acc_ref[slot] = alpha * acc_ref[slot] + jnp.dot(vt, p.astype(BF16),
                                                        preferred_element_type=F32)
        m_ref[slot] = m_new

    def tile(j, kind, nxt):
        for slot in range(4):
            ahead = slot + 2
            if ahead < 4:
                scores(j, ahead)
            elif nxt is None:
                scores(0, ahead - 4, qs_next)
            else:
                scores(nxt, ahead - 4)
            update(j, slot, kind)

    @pl.when(qi == 0)
    def _():
        scores(0, 0)
        scores(0, 1)

    n_far = jnp.maximum(qi - 1, 0)
    n_groups = n_far // FAR_UNROLL

    def far_group(t, carry):
        for u in range(FAR_UNROLL):
            tile(FAR_UNROLL * t + u, None, FAR_UNROLL * t + u + 1)
        return carry

    lax.fori_loop(0, n_groups, far_group, 0)

    def far_single(j, carry):
        tile(j, None, j + 1)
        return carry

    lax.fori_loop(n_groups * FAR_UNROLL, n_far, far_single, 0)

    @pl.when(qi >= 1)
    def _():
        tile(qi - 1, 1, qi)

    tile(qi, 0, None)

    lp = lam_ref[...]
    lam = (jnp.exp(jnp.sum(lp[0:1] * lp[1:2], axis=1, keepdims=True))
           - jnp.exp(jnp.sum(lp[2:3] * lp[3:4], axis=1, keepdims=True)) + lambda_init)
    for head in range(2):
        a1 = acc_ref[2 * head] / l_ref[2 * head]
        a2 = acc_ref[2 * head + 1] / l_ref[2 * head + 1]
        of = (a1 - lam * a2).T
        of = of * lax.rsqrt(jnp.mean(of * of, axis=-1, keepdims=True) + LN_EPS) * sg_ref[...]
        of = of * (1.0 - lambda_init)
        o_ref[:, head * V_DIM:(head + 1) * V_DIM] = of.astype(o_ref.dtype)


def _attention(qt, k, vt, bias_diag, bias_corner, lam_params, subln_g, lambda_init):
    nq = SEQ // TQ
    nk = SEQ // TK
    qk_blocks = QK_WIDTH // 2 // LANE
    return pl.pallas_call(
        functools.partial(_attn_kernel, lambda_init=lambda_init),
        grid=(BATCH, N_PAIRS, nq),
        in_specs=[
            pl.BlockSpec((LANE, TQ), lambda b, p, i: (p, b * nq + i)),
            pl.BlockSpec((LANE, TQ), lambda b, p, i: (qk_blocks + p, b * nq + i)),
            pl.BlockSpec((LANE, TQ), lambda b, p, i: (p, b * nq + jnp.minimum(i + 1, nq - 1))),
            pl.BlockSpec((LANE, TQ),
                         lambda b, p, i: (qk_blocks + p, b * nq + jnp.minimum(i + 1, nq - 1))),
            pl.BlockSpec((SEQ, LANE), lambda b, p, i: (b, p)),
            pl.BlockSpec((SEQ, LANE), lambda b, p, i: (b, qk_blocks + p)),
            pl.BlockSpec((nk, 2 * V_DIM, TK), lambda b, p, i: (b, p, 0)),
            pl.BlockSpec((2, TK, TQ), lambda b, p, i: (p, 0, 0)),
            pl.BlockSpec((2, LANE, LANE), lambda b, p, i: (p, 0, 0)),
            pl.BlockSpec((4, HEAD_DIM), lambda b, p, i: (0, 0)),
            pl.BlockSpec((1, V_DIM), lambda b, p, i: (0, 0)),
        ],
        out_specs=pl.BlockSpec((TQ, 2 * V_DIM), lambda b, p, i: (b * nq + i, p)),
        out_shape=jax.ShapeDtypeStruct((T, V_WIDTH), BF16),
        scratch_shapes=[
            pltpu.VMEM((4, TK, TQ), F32),
            pltpu.VMEM((4, 1, TQ), F32),
            pltpu.VMEM((4, 1, TQ), F32),
            pltpu.VMEM((4, 1, TQ), F32),
            pltpu.VMEM((4, V_DIM, TQ), F32),
        ],
        compiler_params=_params(("parallel", "parallel", "arbitrary"), vmem_mb=56),
        name="diff_attention",
    )(qt, qt, qt, qt, k, k, vt, bias_diag, bias_corner, lam_params, subln_g)


def kernel(x, a_w_pw1, a_b_pw1, a_w_dw, a_b_dw, a_ln_g, a_ln_b, a_w_pw2, a_b_pw2, w_kv, b_w_q,
           b_lambda, b_subln_g, b_w_o, rel_bias, ln_mix_g, ln_mix_b, ln_ffn_g, ln_ffn_b,
           router_w, router_bias, moe_w_gate, moe_w_up, moe_w_down):
    x = x.reshape(T, D_MODEL)
    row = lambda v: v.reshape(1, -1)
    wr_hi = router_w.T.astype(BF16)
    wr_mid = (router_w.T - wr_hi.astype(F32)).astype(BF16)
    wrt = jnp.concatenate([wr_hi, wr_mid], axis=0)
    rb = router_bias.reshape(N_EXPERTS, 1)
    k_all = vt_all = bias_diag = bias_corner = None
    wg_all = moe_w_gate.astype(BF16)
    wu_all = moe_w_up.astype(BF16)
    wd_all = moe_w_down.astype(BF16)
    for l in range(DEPTH):
        if l < N_A:
            g = _glu(x, a_w_pw1[l].astype(BF16), row(a_b_pw1[l]))
            wdw = jnp.pad(a_w_dw[l], ((0, CONV_HALO - CONV_WIDTH), (0, 0)))
            x = _conv_back(g, x, wdw, row(a_b_dw[l]), row(a_ln_g[l]), row(a_ln_b[l]),
                           a_w_pw2[l].astype(BF16), row(a_b_pw2[l]),
                           row(ln_mix_g[l]), row(ln_mix_b[l]))
        else:
            if l == N_A:
                k_all, vt_all = _kv_proj(x, w_kv[:, :QK_WIDTH].astype(BF16),
                                         w_kv[:, QK_WIDTH:].T.astype(BF16))
                bias_diag, bias_corner = _bias_tiles(rel_bias)
            j = l - N_A
            lambda_init = 0.8 - 0.6 * math.exp(-0.3 * l)
            qt = _q_proj(x, b_w_q[j].T.astype(BF16), HEAD_DIM ** -0.5 * LOG2E)
            o = _attention(qt, k_all, vt_all, bias_diag, bias_corner, b_lambda[j],
                           row(b_subln_g[j]), lambda_init)
            x = _proj_ln(o, x, b_w_o[j].astype(BF16), row(ln_mix_g[l]), row(ln_mix_b[l]))
        x = _moe(x, wrt, rb, wg_all, wu_all, wd_all, l, row(ln_ffn_g[l]), row(ln_ffn_b[l]))
    return x.reshape(BATCH, SEQ, D_MODEL)
```

```python
import functools
import math

import jax
import jax.numpy as jnp
from jax import lax
from jax.experimental import pallas as pl
from jax.experimental.pallas import tpu as pltpu

D_MODEL = 1024
BATCH = 2
SEQ = 8192
DEPTH = 4
N_A = DEPTH // 2
CONV_WIDTH = 31
N_HEADS = 8
HEAD_DIM = 64
V_DIM = 2 * HEAD_DIM
QK_WIDTH = 2 * N_HEADS * HEAD_DIM
V_WIDTH = N_HEADS * V_DIM
N_BUCKETS = 32
MAX_DISTANCE = 128
N_EXPERTS = 16
N_GROUPS = 4
EXPERTS_PER_GROUP = N_EXPERTS // N_GROUPS
D_EXPERT = 512
ALPHA = (2.0 * DEPTH) ** 0.25
LN_EPS = 1e-5

T = BATCH * SEQ
F32 = jnp.float32
BF16 = jnp.bfloat16
LOG2E = 1.4426950408889634
NEG_BIG = -1e30
LANE = 128
CONV_HALO = 32
N_PAIRS = N_HEADS // 2

TM_GLU = 512
TM_CONV = 256
TM_ROUTER = 1024
TM_MOE = 1024
TM_PROJ = 512
TQ = 512
TK = 512
FAR_UNROLL = 4


def _params(sem, vmem_mb=None, flags=None):
    kw = dict(dimension_semantics=sem)
    if vmem_mb is not None:
        kw["vmem_limit_bytes"] = vmem_mb * 1024 * 1024
    if flags is not None:
        kw["flags"] = flags
    return pltpu.CompilerParams(**kw)


def _layer_norm(v, g, b):
    mu = jnp.mean(v, axis=-1, keepdims=True)
    d = v - mu
    var = jnp.mean(d * d, axis=-1, keepdims=True)
    return d * lax.rsqrt(var + LN_EPS) * g + b


def _sigmoid(v):
    return 1.0 / (1.0 + jnp.exp(-v))


_NT = (((1,), (1,)), ((), ()))


def _glu_kernel(x_ref, w_ref, b_ref, o_ref):
    xb = x_ref[...].astype(BF16)
    a = jnp.dot(xb, w_ref[:, :D_MODEL], preferred_element_type=F32) + b_ref[:, :D_MODEL]
    gate = jnp.dot(xb, w_ref[:, D_MODEL:], preferred_element_type=F32) + b_ref[:, D_MODEL:]
    o_ref[...] = a * _sigmoid(gate)


def _glu(x, w, b):
    return pl.pallas_call(
        _glu_kernel,
        grid=(T // TM_GLU,),
        in_specs=[
            pl.BlockSpec((TM_GLU, D_MODEL), lambda i: (i, 0)),
            pl.BlockSpec((D_MODEL, 2 * D_MODEL), lambda i: (0, 0)),
            pl.BlockSpec((1, 2 * D_MODEL), lambda i: (0, 0)),
        ],
        out_specs=pl.BlockSpec((TM_GLU, D_MODEL), lambda i: (i, 0)),
        out_shape=jax.ShapeDtypeStruct((T, D_MODEL), F32),
        compiler_params=_params(("parallel",)),
        name="glu_front",
    )(x, w, b)


SUBLANES = 8
CONV_BASE = CONV_HALO - (CONV_WIDTH - 1)


def _conv_kernel(g_ref, halo_ref, x_ref, wdw_ref, bdw_ref, lng_ref, lnb_ref, w2_ref, b2_ref,
                 mg_ref, mb_ref, o_ref, buf_ref, cv_ref):
    i = pl.program_id(0)
    first = (i % (SEQ // TM_CONV)) == 0
    buf_ref[0:CONV_HALO, :] = jnp.where(first, 0.0, halo_ref[...])
    buf_ref[CONV_HALO:CONV_HALO + TM_CONV, :] = g_ref[...]
    buf_ref[CONV_HALO + TM_CONV:, :] = jnp.zeros((SUBLANES, D_MODEL), F32)
    sub = lax.broadcasted_iota(jnp.int32, (SUBLANES, LANE), 0)
    n_q = (CONV_BASE + CONV_WIDTH - 1) // SUBLANES + 1
    for c in range(D_MODEL // LANE):
        cs = slice(c * LANE, (c + 1) * LANE)
        w_b = [jnp.broadcast_to(wdw_ref[j:j + 1, cs], (SUBLANES, LANE)) for j in range(CONV_WIDTH)]
        bias = jnp.broadcast_to(bdw_ref[:, cs], (SUBLANES, LANE))

        def partials(v):
            tiles = [buf_ref[pl.ds(v + SUBLANES * q, SUBLANES), cs] for q in range(n_q)]
            out = []
            for s in range(SUBLANES):
                a = None
                for q in range(n_q):
                    j = SUBLANES * q + s - CONV_BASE
                    if 0 <= j < CONV_WIDTH:
                        term = w_b[j] * tiles[q]
                        a = term if a is None else a + term
                out.append(a)
            return tuple(out)

        def row_body(t, prev):
            r = pl.multiple_of(t * SUBLANES, SUBLANES)
            cur = partials(pl.multiple_of(r + SUBLANES, SUBLANES))
            acc = bias + prev[0]
            for s in range(1, SUBLANES):
                mixed = jnp.where(sub >= s, prev[s], cur[s])
                acc = acc + pltpu.roll(mixed, SUBLANES - s, axis=0)
            cv_ref[pl.ds(r, SUBLANES), cs] = acc
            return cur

        lax.fori_loop(0, TM_CONV // SUBLANES, row_body, partials(0), unroll=2)
    h = _layer_norm(cv_ref[...], lng_ref[...], lnb_ref[...])
    h = h * _sigmoid(h)
    mix = jnp.dot(h.astype(BF16), w2_ref[...], preferred_element_type=F32) + b2_ref[...]
    o_ref[...] = _layer_norm(ALPHA * x_ref[...] + mix, mg_ref[...], mb_ref[...])


def _conv_back(g, x, wdw, bdw, lng, lnb, w2, b2, mg, mb):
    row = lambda i: (i, 0)
    fixed = lambda i: (0, 0)
    vec = pl.BlockSpec((1, D_MODEL), fixed)
    halo_blocks = TM_CONV // CONV_HALO
    return pl.pallas_call(
        _conv_kernel,
        grid=(T // TM_CONV,),
        in_specs=[
            pl.BlockSpec((TM_CONV, D_MODEL), row),
            pl.BlockSpec((CONV_HALO, D_MODEL), lambda i: (jnp.maximum(i * halo_blocks - 1, 0), 0)),
            pl.BlockSpec((TM_CONV, D_MODEL), row),
            pl.BlockSpec((CONV_HALO, D_MODEL), fixed),
            vec, vec, vec,
            pl.BlockSpec((D_MODEL, D_MODEL), fixed),
            vec, vec, vec,
        ],
        out_specs=pl.BlockSpec((TM_CONV, D_MODEL), row),
        out_shape=jax.ShapeDtypeStruct((T, D_MODEL), F32),
        scratch_shapes=[
            pltpu.VMEM((TM_CONV + CONV_HALO + SUBLANES, D_MODEL), F32),
            pltpu.VMEM((TM_CONV, D_MODEL), F32),
        ],
        compiler_params=_params(("parallel",)),
        name="conv_back",
    )(g, g, x, wdw, bdw, lng, lnb, w2, b2, mg, mb)


def _ranks_before(vals):
    n = len(vals)
    ranks = []
    for j in range(n):
        r = jnp.zeros_like(vals[j])
        for i in range(n):
            if i == j:
                continue
            before = (vals[i] > vals[j]) | ((vals[i] == vals[j]) & (i < j))
            r = r + before.astype(F32)
        ranks.append(r)
    return ranks


PAIRS = [(a, b) for a in range(EXPERTS_PER_GROUP) for b in range(a + 1, EXPERTS_PER_GROUP)]
N_CLASSES = N_GROUPS * len(PAIRS)
CLASS_ROWS = 32
CLASS_EXPERTS = [(g * EXPERTS_PER_GROUP + a, g * EXPERTS_PER_GROUP + b)
                 for g in range(N_GROUPS) for (a, b) in PAIRS]
TM_G = 256
NT_MAX = -(-(T + N_CLASSES * (TM_G - 1)) // TM_G)
P_MAX = NT_MAX * TM_G
XG_W = D_MODEL + LANE
GATE_ROWS = 8
TD = 256
NT_MIN = T // TM_G
N_ZERO_TILES = N_CLASSES + NT_MAX - NT_MIN


def _route_kernel(x_ref, wrt_ref, rb_ref, gw_ref, cls_ref, pos_ref, tot_ref, tri_ref, carry_ref):
    i = pl.program_id(0)

    @pl.when(i == 0)
    def _():
        r = lax.broadcasted_iota(jnp.int32, (TM_ROUTER, TM_ROUTER), 0)
        c = lax.broadcasted_iota(jnp.int32, (TM_ROUTER, TM_ROUTER), 1)
        tri_ref[...] = jnp.where(r < c, 1.0, 0.0).astype(BF16)
        carry_ref[...] = jnp.zeros_like(carry_ref)

    x = x_ref[...]
    x_hi = x.astype(BF16)
    x_mid = (x - x_hi.astype(F32)).astype(BF16)
    both = lax.dot_general(wrt_ref[...], x_hi, _NT, preferred_element_type=F32)
    logits = (both[:N_EXPERTS] + both[N_EXPERTS:]
              + lax.dot_general(wrt_ref[:N_EXPERTS, :], x_mid, _NT, preferred_element_type=F32))
    aff = _sigmoid(logits)
    sel = aff + rb_ref[...]
    aff_rows = [aff[e:e + 1, :] for e in range(N_EXPERTS)]
    sel_rows = [sel[e:e + 1, :] for e in range(N_EXPERTS)]
    in_top2 = []
    scores = []
    for g in range(N_GROUPS):
        members = sel_rows[g * EXPERTS_PER_GROUP:(g + 1) * EXPERTS_PER_GROUP]
        ranks = _ranks_before(members)
        top = [r < 2.0 for r in ranks]
        in_top2.extend(top)
        s = jnp.zeros_like(members[0])
        for v, t in zip(members, top):
            s = s + jnp.where(t, v, 0.0)
        scores.append(s)
    g_ranks = _ranks_before(scores)
    w_rows = []
    for e in range(N_EXPERTS):
        chosen = (g_ranks[e // EXPERTS_PER_GROUP] < 1.0) & in_top2[e]
        w_rows.append(jnp.where(chosen, aff_rows[e], 0.0))
    denom = w_rows[0]
    for e in range(1, N_EXPERTS):
        denom = denom + w_rows[e]
    inv = 1.0 / denom

    masks = []
    wa = jnp.zeros_like(denom)
    wb = jnp.zeros_like(denom)
    for c, (ea, eb) in enumerate(CLASS_EXPERTS):
        m = (g_ranks[c // len(PAIRS)] < 1.0) & in_top2[ea] & in_top2[eb]
        masks.append(m.astype(F32))
        wa = wa + jnp.where(m, aff_rows[ea], 0.0)
        wb = wb + jnp.where(m, aff_rows[eb], 0.0)
    zero_row = jnp.zeros_like(denom)
    onehot = jnp.concatenate(masks + [zero_row] * (CLASS_ROWS - N_CLASSES), axis=0)

    before = jnp.dot(onehot.astype(BF16), tri_ref[...], preferred_element_type=F32)
    carry = carry_ref[...]
    class_id = lax.broadcasted_iota(jnp.int32, (CLASS_ROWS, 1), 0).astype(F32)
    pos_ref[...] = jnp.sum(onehot * (before + carry), axis=0, keepdims=True).astype(jnp.int32)
    cls_ref[...] = jnp.sum(onehot * class_id, axis=0, keepdims=True).astype(jnp.int32)
    carry = carry + jnp.sum(onehot, axis=1, keepdims=True)
    carry_ref[...] = carry
    tot_ref[...] = carry

    gw_ref[...] = jnp.concatenate([wa * inv, wb * inv] + [zero_row] * (GATE_ROWS - 2), axis=0)


def _route(x, wrt, rb):
    return pl.pallas_call(
        _route_kernel,
        grid=(T // TM_ROUTER,),
        in_specs=[
            pl.BlockSpec((TM_ROUTER, D_MODEL), lambda i: (i, 0)),
            pl.BlockSpec((2 * N_EXPERTS, D_MODEL), lambda i: (0, 0)),
            pl.BlockSpec((N_EXPERTS, 1), lambda i: (0, 0)),
        ],
        out_specs=[
            pl.BlockSpec((GATE_ROWS, TM_ROUTER), lambda i: (0, i)),
            pl.BlockSpec((1, TM_ROUTER), lambda i: (0, i)),
            pl.BlockSpec((1, TM_ROUTER), lambda i: (0, i)),
            pl.BlockSpec((CLASS_ROWS, 1), lambda i: (0, 0)),
        ],
        out_shape=[
            jax.ShapeDtypeStruct((GATE_ROWS, T), F32),
            jax.ShapeDtypeStruct((1, T), jnp.int32),
            jax.ShapeDtypeStruct((1, T), jnp.int32),
            jax.ShapeDtypeStruct((CLASS_ROWS, 1), F32),
        ],
        scratch_shapes=[
            pltpu.VMEM((TM_ROUTER, TM_ROUTER), BF16),
            pltpu.VMEM((CLASS_ROWS, 1), F32),
        ],
        compiler_params=_params(("arbitrary",)),
        name="route",
    )(x, wrt, rb)


def _start_all(copies):
    for r, cp in enumerate(copies):
        cp.start(priority=r % 2)


def _row_copies(pay, slot, dst_hbm, dest_ref, base, sem):
    return [pltpu.make_async_copy(pay.at[slot, r], dst_hbm.at[dest_ref[base + r]], sem.at[slot])
            for r in range(TD)]


def _dispatch_kernel(dest_ref, zstart_ref, x_ref, gw_ref, xs_hbm, pay, zbuf, zsem, sem):
    i = pl.program_id(0)
    last = pl.num_programs(0) - 1

    @pl.when(i == 0)
    def _():
        zbuf[...] = jnp.zeros_like(zbuf)

        def zero_tile(c):
            start = pl.multiple_of(zstart_ref[c], TM_G)
            return pltpu.make_async_copy(zbuf, xs_hbm.at[pl.ds(start, TM_G)], zsem)

        for c in range(N_ZERO_TILES):
            @pl.when(zstart_ref[c] >= 0)
            def _():
                zero_tile(c).start()
        for c in range(N_ZERO_TILES):
            @pl.when(zstart_ref[c] >= 0)
            def _():
                zero_tile(c).wait()

    for parity in range(2):
        @pl.when(i % 2 == parity)
        def _():
            pay[parity, :, :D_MODEL] = x_ref[...]
            gates = jnp.concatenate(
                [gw_ref[...], jnp.zeros((LANE - GATE_ROWS, TD), F32)], axis=0)
            pay[parity, :, D_MODEL:] = gates.T
            _start_all(_row_copies(pay, parity, xs_hbm, dest_ref, i * TD, sem))

            @pl.when(i > 0)
            def _():
                for cp in _row_copies(pay, 1 - parity, xs_hbm, dest_ref, (i - 1) * TD, sem):
                    cp.wait()

            @pl.when(i == last)
            def _():
                for cp in _row_copies(pay, parity, xs_hbm, dest_ref, i * TD, sem):
                    cp.wait()


def _dispatch(dest, zstart, x, gw):
    return pl.pallas_call(
        _dispatch_kernel,
        grid_spec=pltpu.PrefetchScalarGridSpec(
            num_scalar_prefetch=2,
            grid=(T // TD,),
            in_specs=[
                pl.BlockSpec((TD, D_MODEL), lambda i, d, z: (i, 0)),
                pl.BlockSpec((GATE_ROWS, TD), lambda i, d, z: (0, i)),
            ],
            out_specs=pl.BlockSpec(memory_space=pl.ANY),
            scratch_shapes=[
                pltpu.VMEM((2, TD, XG_W), F32),
                pltpu.VMEM((TM_G, XG_W), F32),
                pltpu.SemaphoreType.DMA,
                pltpu.SemaphoreType.DMA((2,)),
            ],
        ),
        out_shape=jax.ShapeDtypeStruct((P_MAX, XG_W), F32),
        compiler_params=_params(("arbitrary",)),
        name="dispatch",
    )(dest, zstart, x, gw)


def _experts_kernel(te_ref, nt_ref, xs_ref, wga_ref, wua_ref, wda_ref, wgb_ref, wub_ref, wdb_ref,
                    ys_ref):
    i = pl.program_id(0)

    @pl.when(i < nt_ref[0])
    def _():
        xb = xs_ref[:, :D_MODEL].astype(BF16)
        gates = xs_ref[:, D_MODEL:]
        y = None
        for which, (wg_ref, wu_ref, wd_ref) in enumerate(((wga_ref, wua_ref, wda_ref),
                                                          (wgb_ref, wub_ref, wdb_ref))):
            hg = jnp.dot(xb, wg_ref[0, 0], preferred_element_type=F32)
            hu = jnp.dot(xb, wu_ref[0, 0], preferred_element_type=F32)
            h = hg * _sigmoid(hg) * hu * gates[:, which:which + 1]
            part = jnp.dot(h.astype(BF16), wd_ref[0, 0], preferred_element_type=F32)
            y = part if y is None else y + part
        ys_ref[...] = y

    @pl.when(i >= nt_ref[0])
    def _():
        ys_ref[...] = jnp.zeros_like(ys_ref)


def _experts(tile_e, n_tiles, xs, wg, wu, wd, layer):
    def row_map(i, te, nt):
        return (jnp.minimum(i, nt[0] - 1), 0)

    def w_map(which):
        return lambda i, te, nt: (layer, te[which * NT_MAX + i], 0, 0)

    up_spec = lambda which: pl.BlockSpec((1, 1, D_MODEL, D_EXPERT), w_map(which))
    down_spec = lambda which: pl.BlockSpec((1, 1, D_EXPERT, D_MODEL), w_map(which))
    return pl.pallas_call(
        _experts_kernel,
        grid_spec=pltpu.PrefetchScalarGridSpec(
            num_scalar_prefetch=2,
            grid=(NT_MAX,),
            in_specs=[
                pl.BlockSpec((TM_G, XG_W), row_map),
                up_spec(0), up_spec(0), down_spec(0),
                up_spec(1), up_spec(1), down_spec(1),
            ],
            out_specs=pl.BlockSpec((TM_G, D_MODEL), lambda i, te, nt: (i, 0)),
        ),
        out_shape=jax.ShapeDtypeStruct((P_MAX, D_MODEL), F32),
        compiler_params=_params(("arbitrary",)),
        name="experts",
    )(tile_e, n_tiles, xs, wg, wu, wd, wg, wu, wd)


def _gather_copies(ys_hbm, buf, dest_ref, base, slot, sem):
    return [pltpu.make_async_copy(ys_hbm.at[dest_ref[base + r]], buf.at[slot, r], sem.at[slot])
            for r in range(TD)]


def _combine_kernel(dest_ref, x_ref, ys_hbm, lg_ref, lb_ref, *rest, q_scale):
    if q_scale is None:
        o_ref, buf, sem = rest
    else:
        wqt_ref, o_ref, qt_ref, buf, sem = rest
    i = pl.program_id(0)
    last = pl.num_programs(0) - 1

    @pl.when(i == 0)
    def _():
        _start_all(_gather_copies(ys_hbm, buf, dest_ref, 0, 0, sem))

    for parity in range(2):
        @pl.when(i % 2 == parity)
        def _():
            @pl.when(i < last)
            def _():
                _start_all(_gather_copies(ys_hbm, buf, dest_ref, (i + 1) * TD, 1 - parity, sem))

            for cp in _gather_copies(ys_hbm, buf, dest_ref, i * TD, parity, sem):
                cp.wait()
            xn = _layer_norm(ALPHA * x_ref[...] + buf[parity], lg_ref[...], lb_ref[...])
            o_ref[...] = xn
            if q_scale is not None:
                q = lax.dot_general(wqt_ref[...], xn.astype(BF16), _NT, preferred_element_type=F32)
                qt_ref[...] = (q * q_scale).astype(BF16)


def _combine(dest, x, ys, lg, lb, wq_t=None, q_scale=None):
    vec = pl.BlockSpec((1, D_MODEL), lambda i, d: (0, 0))
    rows = pl.BlockSpec((TD, D_MODEL), lambda i, d: (i, 0))
    in_specs = [rows, pl.BlockSpec(memory_space=pl.ANY), vec, vec]
    out_specs = [rows]
    out_shape = [jax.ShapeDtypeStruct((T, D_MODEL), F32)]
    operands = [dest, x, ys, lg, lb]
    if wq_t is not None:
        in_specs.append(pl.BlockSpec((QK_WIDTH, D_MODEL), lambda i, d: (0, 0)))
        out_specs.append(pl.BlockSpec((QK_WIDTH, TD), lambda i, d: (0, i)))
        out_shape.append(jax.ShapeDtypeStruct((QK_WIDTH, T), BF16))
        operands.append(wq_t)
    outs = pl.pallas_call(
        functools.partial(_combine_kernel, q_scale=q_scale if wq_t is not None else None),
        grid_spec=pltpu.PrefetchScalarGridSpec(
            num_scalar_prefetch=1,
            grid=(T // TD,),
            in_specs=in_specs,
            out_specs=out_specs,
            scratch_shapes=[
                pltpu.VMEM((2, TD, D_MODEL), F32),
                pltpu.SemaphoreType.DMA((2,)),
            ],
        ),
        out_shape=out_shape,
        compiler_params=_params(("arbitrary",)),
        name="combine",
    )(*operands)
    return outs[0] if wq_t is None else tuple(outs)


TM_G_LOG2 = TM_G.bit_length() - 1
assert 1 << TM_G_LOG2 == TM_G


def _plan_kernel(cnt_ref, cls_ref, pos_ref, dest_ref, zstart_ref, te_ref, nt_ref):
    shr = lax.shift_right_logical
    run = jnp.int32(0)
    starts, ends = [], []
    for c in range(N_CLASSES):
        padded = shr(cnt_ref[c] + (TM_G - 1), TM_G_LOG2) * TM_G
        starts.append(run)
        run = run + padded
        ends.append(run)
        zstart_ref[c] = jnp.where(padded > 0, run - TM_G, -1)
    n_tiles = shr(run, TM_G_LOG2)
    nt_ref[0] = n_tiles
    for t in range(NT_MIN, NT_MAX):
        zstart_ref[N_CLASSES + t - NT_MIN] = jnp.where(t >= n_tiles, t * TM_G, -1)

    def tile_body(t, carry):
        row0 = jnp.minimum(t, n_tiles - 1) * TM_G
        ea = jnp.int32(0)
        eb = jnp.int32(0)
        for c in range(N_CLASSES):
            inside = (row0 >= starts[c]) & (row0 < ends[c])
            ea = jnp.where(inside, CLASS_EXPERTS[c][0], ea)
            eb = jnp.where(inside, CLASS_EXPERTS[c][1], eb)
        te_ref[t] = ea
        te_ref[NT_MAX + t] = eb
        return carry

    lax.fori_loop(0, NT_MAX, tile_body, 0)

    cls = cls_ref[...]
    dest = pos_ref[...]
    for c in range(N_CLASSES):
        dest = dest + jnp.where(cls == c, starts[c], 0)
    dest_ref[...] = dest


def _plan(counts, cls, pos):
    smem = pl.BlockSpec(memory_space=pltpu.SMEM)
    vmem = pl.BlockSpec(memory_space=pltpu.VMEM)
    return pl.pallas_call(
        _plan_kernel,
        in_specs=[smem, vmem, vmem],
        out_specs=[vmem, smem, smem, smem],
        out_shape=[
            jax.ShapeDtypeStruct((1, T), jnp.int32),
            jax.ShapeDtypeStruct((N_ZERO_TILES,), jnp.int32),
            jax.ShapeDtypeStruct((2 * NT_MAX,), jnp.int32),
            jax.ShapeDtypeStruct((1,), jnp.int32),
        ],
        name="plan",
    )(counts, cls, pos)


def _moe(x, wrt, rb, wg, wu, wd, layer, lg, lb, wq_t=None, q_scale=None):
    gw, cls, pos, tot = _route(x, wrt, rb)
    dest, zstart, tile_e, n_tiles = _plan(tot[:, 0].astype(jnp.int32), cls, pos)
    dest = dest.reshape(T)
    xs = _dispatch(dest, zstart, x, gw)
    ys = _experts(tile_e, n_tiles, xs, wg, wu, wd, layer)
    return _combine(dest, x, ys, lg, lb, wq_t, q_scale)


def _kv_proj_kernel(x_ref, wk_ref, wvt_ref, k_ref, vt_ref):
    xb = x_ref[...].astype(BF16)
    k_ref[...] = jnp.dot(xb, wk_ref[...], preferred_element_type=F32).astype(BF16)
    vt_ref[0] = lax.dot_general(wvt_ref[...], xb, _NT, preferred_element_type=F32).astype(BF16)


def _kv_proj(x, wk, wvt):
    return pl.pallas_call(
        _kv_proj_kernel,
        grid=(T // TK,),
        in_specs=[
            pl.BlockSpec((TK, D_MODEL), lambda i: (i, 0)),
            pl.BlockSpec((D_MODEL, QK_WIDTH), lambda i: (0, 0)),
            pl.BlockSpec((V_WIDTH, D_MODEL), lambda i: (0, 0)),
        ],
        out_specs=[
            pl.BlockSpec((TK, QK_WIDTH), lambda i: (i, 0)),
            pl.BlockSpec((1, V_WIDTH, TK), lambda i: (i, 0, 0)),
        ],
        out_shape=[
            jax.ShapeDtypeStruct((T, QK_WIDTH), BF16),
            jax.ShapeDtypeStruct((T // TK, V_WIDTH, TK), BF16),
        ],
        compiler_params=_params(("parallel",)),
        name="kv_proj",
    )(x, wk, wvt)


def _proj_ln_kernel(a_ref, x_ref, w_ref, g_ref, b_ref, o_ref):
    mix = jnp.dot(a_ref[...], w_ref[...], preferred_element_type=F32)
    o_ref[...] = _layer_norm(ALPHA * x_ref[...] + mix, g_ref[...], b_ref[...])


def _proj_ln(a, x, w, g, b):
    vec = pl.BlockSpec((1, D_MODEL), lambda i: (0, 0))
    return pl.pallas_call(
        _proj_ln_kernel,
        grid=(T // TM_PROJ,),
        in_specs=[
            pl.BlockSpec((TM_PROJ, V_WIDTH), lambda i: (i, 0)),
            pl.BlockSpec((TM_PROJ, D_MODEL), lambda i: (i, 0)),
            pl.BlockSpec((V_WIDTH, D_MODEL), lambda i: (0, 0)),
            vec, vec,
        ],
        out_specs=pl.BlockSpec((TM_PROJ, D_MODEL), lambda i: (i, 0)),
        out_shape=jax.ShapeDtypeStruct((T, D_MODEL), F32),
        compiler_params=_params(("parallel",)),
        name="proj_ln",
    )(a, x, w, g, b)


def _bias_kernel(rb_ref, diag_ref, corner_ref):
    h = pl.program_id(0)
    max_exact = N_BUCKETS // 2
    far = rb_ref[N_BUCKETS - 1, h]

    def tile(shape, offset):
        c = lax.broadcasted_iota(jnp.int32, shape, 0)
        r = lax.broadcasted_iota(jnp.int32, shape, 1)
        rel = offset + r - c
        n = jnp.maximum(rel, 0)
        nf = jnp.maximum(n, 1).astype(F32)
        large = max_exact + (jnp.log(nf / max_exact) / math.log(MAX_DISTANCE / max_exact)
                             * (N_BUCKETS - max_exact)).astype(jnp.int32)
        large = jnp.minimum(large, N_BUCKETS - 1)
        bucket = jnp.where(n < max_exact, n, large)
        bias = jnp.zeros(shape, F32)
        for b in range(N_BUCKETS):
            bias = jnp.where(bucket == b, rb_ref[b, h] - far, bias)
        return jnp.where(rel >= 0, bias * LOG2E, NEG_BIG)

    diag_ref[0] = tile((TK, TQ), 0)
    corner_ref[0] = tile((LANE, LANE), LANE)


def _bias_tiles(rel_bias):
    return pl.pallas_call(
        _bias_kernel,
        grid=(N_HEADS,),
        in_specs=[pl.BlockSpec(memory_space=pltpu.SMEM)],
        out_specs=[
            pl.BlockSpec((1, TK, TQ), lambda h: (h, 0, 0)),
            pl.BlockSpec((1, LANE, LANE), lambda h: (h, 0, 0)),
        ],
        out_shape=[
            jax.ShapeDtypeStruct((N_HEADS, TK, TQ), F32),
            jax.ShapeDtypeStruct((N_HEADS, LANE, LANE), F32),
        ],
        compiler_params=_params(("parallel",)),
        name="bias_tiles",
    )(rel_bias)


def _attn_kernel(q1t_ref, q2t_ref, q1n_ref, q2n_ref, k1_ref, k2_ref, vt_ref, diag_ref, corner_ref,
                 lam_ref, sg_ref, o_ref, s_ref, mx_ref, m_ref, l_ref, acc_ref, *, lambda_init):
    qi = pl.program_id(2)
    dim = lax.broadcasted_iota(jnp.int32, (LANE, 1), 0)
    lo = dim < HEAD_DIM

    def head_masked(q1, q2):
        zero = jnp.zeros_like(q1)
        return [jnp.where(lo, q1, zero), jnp.where(lo, q2, zero),
                jnp.where(lo, zero, q1), jnp.where(lo, zero, q2)]

    qs = head_masked(q1t_ref[...], q2t_ref[...])
    qs_next = head_masked(q1n_ref[...], q2n_ref[...])

    def reset_stats():
        m_ref[...] = jnp.full(m_ref.shape, NEG_BIG, F32)
        l_ref[...] = jnp.zeros(l_ref.shape, F32)
        acc_ref[...] = jnp.zeros(acc_ref.shape, F32)

    def scores(j, slot, q_slots=qs):
        k0 = pl.multiple_of(j * TK, TK)
        k_ref = k1_ref if slot % 2 == 0 else k2_ref
        st = jnp.dot(k_ref[pl.ds(k0, TK), :], q_slots[slot], preferred_element_type=F32)
        s_ref[slot] = st
        mx_ref[slot] = jnp.max(st, axis=0, keepdims=True)

    def update(j, slot, kind):
        head = slot // 2
        if kind == 1:
            rows = slice(TK - LANE, TK)
            corner = s_ref[slot, rows, 0:LANE] + corner_ref[head]
            s_ref[slot, rows, 0:LANE] = corner
            mx_ref[slot, :, 0:LANE] = jnp.maximum(mx_ref[slot, :, 0:LANE],
                                                  jnp.max(corner, axis=0, keepdims=True))
        st = s_ref[slot]
        if kind == 0:
            st = st + diag_ref[head]
            mx = jnp.max(st, axis=0, keepdims=True)
        else:
            mx = mx_ref[slot]
        m_prev = m_ref[slot]
        m_new = jnp.maximum(m_prev, mx)
        alpha = jnp.exp2(m_prev - m_new)
        p = jnp.exp2(st - m_new)
        l_ref[slot] = alpha * l_ref[slot] + jnp.sum(p, axis=0, keepdims=True)
        vt = vt_ref[j, head * V_DIM:(head + 1) * V_DIM, :]
        acc_ref[slot] = alpha * acc_ref[slot] + jnp.dot(vt, p.astype(BF16),
                                                        preferred_element_type=F32)
        m_ref[slot] = m_new

    def tile(j, kind, nxt):
        for slot in range(4):
            ahead = slot + 2
            if ahead < 4:
                scores(j, ahead)
            elif nxt is None:
                scores(0, ahead - 4, qs_next)
            else:
                scores(nxt, ahead - 4)
            update(j, slot, kind)

    @pl.when(qi == 0)
    def _():
        reset_stats()
        scores(0, 0)
        scores(0, 1)

    n_far = jnp.maximum(qi - 1, 0)
    n_groups = n_far // FAR_UNROLL

    def far_group(t, carry):
        for u in range(FAR_UNROLL):
            tile(FAR_UNROLL * t + u, None, FAR_UNROLL * t + u + 1)
        return carry

    lax.fori_loop(0, n_groups, far_group, 0)

    def far_single(j, carry):
        tile(j, None, j + 1)
        return carry

    lax.fori_loop(n_groups * FAR_UNROLL, n_far, far_single, 0)

    @pl.when(qi >= 1)
    def _():
        tile(qi - 1, 1, qi)

    tile(qi, 0, None)

    lp = lam_ref[...]
    lam = (jnp.exp(jnp.sum(lp[0:1] * lp[1:2], axis=1, keepdims=True))
           - jnp.exp(jnp.sum(lp[2:3] * lp[3:4], axis=1, keepdims=True)) + lambda_init)
    for head in range(2):
        a1 = acc_ref[2 * head] / l_ref[2 * head]
        a2 = acc_ref[2 * head + 1] / l_ref[2 * head + 1]
        of = (a1 - lam * a2).T
        of = of * lax.rsqrt(jnp.mean(of * of, axis=-1, keepdims=True) + LN_EPS) * sg_ref[...]
        of = of * (1.0 - lambda_init)
        o_ref[:, head * V_DIM:(head + 1) * V_DIM] = of.astype(o_ref.dtype)
    reset_stats()


def _attention(qt, k, vt, bias_diag, bias_corner, lam_params, subln_g, lambda_init):
    nq = SEQ // TQ
    nk = SEQ // TK
    qk_blocks = QK_WIDTH // 2 // LANE
    return pl.pallas_call(
        functools.partial(_attn_kernel, lambda_init=lambda_init),
        grid=(BATCH, N_PAIRS, nq),
        in_specs=[
            pl.BlockSpec((LANE, TQ), lambda b, p, i: (p, b * nq + i)),
            pl.BlockSpec((LANE, TQ), lambda b, p, i: (qk_blocks + p, b * nq + i)),
            pl.BlockSpec((LANE, TQ), lambda b, p, i: (p, b * nq + jnp.minimum(i + 1, nq - 1))),
            pl.BlockSpec((LANE, TQ),
                         lambda b, p, i: (qk_blocks + p, b * nq + jnp.minimum(i + 1, nq - 1))),
            pl.BlockSpec((SEQ, LANE), lambda b, p, i: (b, p)),
            pl.BlockSpec((SEQ, LANE), lambda b, p, i: (b, qk_blocks + p)),
            pl.BlockSpec((nk, 2 * V_DIM, TK), lambda b, p, i: (b, p, 0)),
            pl.BlockSpec((2, TK, TQ), lambda b, p, i: (p, 0, 0)),
            pl.BlockSpec((2, LANE, LANE), lambda b, p, i: (p, 0, 0)),
            pl.BlockSpec((4, HEAD_DIM), lambda b, p, i: (0, 0)),
            pl.BlockSpec((1, V_DIM), lambda b, p, i: (0, 0)),
        ],
        out_specs=pl.BlockSpec((TQ, 2 * V_DIM), lambda b, p, i: (b * nq + i, p)),
        out_shape=jax.ShapeDtypeStruct((T, V_WIDTH), BF16),
        scratch_shapes=[
            pltpu.VMEM((4, TK, TQ), F32),
            pltpu.VMEM((4, 1, TQ), F32),
            pltpu.VMEM((4, 1, TQ), F32),
            pltpu.VMEM((4, 1, TQ), F32),
            pltpu.VMEM((4, V_DIM, TQ), F32),
        ],
        compiler_params=_params(("parallel", "parallel", "arbitrary"), vmem_mb=56),
        name="diff_attention",
    )(qt, qt, qt, qt, k, k, vt, bias_diag, bias_corner, lam_params, subln_g)


def kernel(x, a_w_pw1, a_b_pw1, a_w_dw, a_b_dw, a_ln_g, a_ln_b, a_w_pw2, a_b_pw2, w_kv, b_w_q,
           b_lambda, b_subln_g, b_w_o, rel_bias, ln_mix_g, ln_mix_b, ln_ffn_g, ln_ffn_b,
           router_w, router_bias, moe_w_gate, moe_w_up, moe_w_down):
    x = x.reshape(T, D_MODEL)
    row = lambda v: v.reshape(1, -1)
    wr_hi = router_w.T.astype(BF16)
    wr_mid = (router_w.T - wr_hi.astype(F32)).astype(BF16)
    wrt = jnp.concatenate([wr_hi, wr_mid], axis=0)
    rb = router_bias.reshape(N_EXPERTS, 1)
    k_all = vt_all = bias_diag = bias_corner = qt = None
    wg_all = moe_w_gate.astype(BF16)
    wu_all = moe_w_up.astype(BF16)
    wd_all = moe_w_down.astype(BF16)
    for l in range(DEPTH):
        if l < N_A:
            g = _glu(x, a_w_pw1[l].astype(BF16), row(a_b_pw1[l]))
            wdw = jnp.pad(a_w_dw[l], ((0, CONV_HALO - CONV_WIDTH), (0, 0)))
            x = _conv_back(g, x, wdw, row(a_b_dw[l]), row(a_ln_g[l]), row(a_ln_b[l]),
                           a_w_pw2[l].astype(BF16), row(a_b_pw2[l]),
                           row(ln_mix_g[l]), row(ln_mix_b[l]))
        else:
            if l == N_A:
                k_all, vt_all = _kv_proj(x, w_kv[:, :QK_WIDTH].astype(BF16),
                                         w_kv[:, QK_WIDTH:].T.astype(BF16))
                bias_diag, bias_corner = _bias_tiles(rel_bias)
            j = l - N_A
            lambda_init = 0.8 - 0.6 * math.exp(-0.3 * l)
            o = _attention(qt, k_all, vt_all, bias_diag, bias_corner, b_lambda[j],
                           row(b_subln_g[j]), lambda_init)
            x = _proj_ln(o, x, b_w_o[j].astype(BF16), row(ln_mix_g[l]), row(ln_mix_b[l]))
        moe_args = (x, wrt, rb, wg_all, wu_all, wd_all, l, row(ln_ffn_g[l]), row(ln_ffn_b[l]))
        if N_A <= l + 1 < DEPTH:
            x, qt = _moe(*moe_args, wq_t=b_w_q[l + 1 - N_A].T.astype(BF16),
                         q_scale=HEAD_DIM ** -0.5 * LOG2E)
        else:
            x = _moe(*moe_args)
    return x.reshape(BATCH, SEQ, D_MODEL)
```

```python
import functools
import math

import jax
import jax.numpy as jnp
from jax import lax
from jax.experimental import pallas as pl
from jax.experimental.pallas import tpu as pltpu

D_MODEL = 1024
BATCH = 2
SEQ = 8192
DEPTH = 4
N_A = DEPTH // 2
CONV_WIDTH = 31
N_HEADS = 8
HEAD_DIM = 64
V_DIM = 2 * HEAD_DIM
QK_WIDTH = 2 * N_HEADS * HEAD_DIM
V_WIDTH = N_HEADS * V_DIM
N_BUCKETS = 32
MAX_DISTANCE = 128
N_EXPERTS = 16
N_GROUPS = 4
EXPERTS_PER_GROUP = N_EXPERTS // N_GROUPS
D_EXPERT = 512
ALPHA = (2.0 * DEPTH) ** 0.25
LN_EPS = 1e-5

T = BATCH * SEQ
F32 = jnp.float32
BF16 = jnp.bfloat16
LOG2E = 1.4426950408889634
NEG_BIG = -1e30
LANE = 128
SUBLANES = 8
CONV_HALO = 32
N_PAIRS = N_HEADS // 2

TM_GLU = 512
TM_CONV = 256
TM_ROUTER = 1024
TM_PROJ = 512
TQ = 512
TK = 512
FAR_UNROLL = 4


def _params(sem, vmem_mb=None):
    kw = dict(dimension_semantics=sem)
    if vmem_mb is not None:
        kw["vmem_limit_bytes"] = vmem_mb * 1024 * 1024
    return pltpu.CompilerParams(**kw)


def _layer_norm(v, g, b):
    mu = jnp.mean(v, axis=-1, keepdims=True)
    d = v - mu
    var = jnp.mean(d * d, axis=-1, keepdims=True)
    return d * lax.rsqrt(var + LN_EPS) * g + b


def _sigmoid(v):
    return 1.0 / (1.0 + jnp.exp(-v))


_NT = (((1,), (1,)), ((), ()))


def _glu_kernel(x_ref, w_ref, b_ref, o_ref):
    xb = x_ref[...].astype(BF16)
    a = jnp.dot(xb, w_ref[:, :D_MODEL], preferred_element_type=F32) + b_ref[:, :D_MODEL]
    gate = jnp.dot(xb, w_ref[:, D_MODEL:], preferred_element_type=F32) + b_ref[:, D_MODEL:]
    o_ref[...] = a * _sigmoid(gate)


def _glu(x, w, b):
    return pl.pallas_call(
        _glu_kernel,
        grid=(T // TM_GLU,),
        in_specs=[
            pl.BlockSpec((TM_GLU, D_MODEL), lambda i: (i, 0)),
            pl.BlockSpec((D_MODEL, 2 * D_MODEL), lambda i: (0, 0)),
            pl.BlockSpec((1, 2 * D_MODEL), lambda i: (0, 0)),
        ],
        out_specs=pl.BlockSpec((TM_GLU, D_MODEL), lambda i: (i, 0)),
        out_shape=jax.ShapeDtypeStruct((T, D_MODEL), F32),
        compiler_params=_params(("parallel",)),
        name="glu_front",
    )(x, w, b)


CONV_BASE = CONV_HALO - (CONV_WIDTH - 1)


def _conv_kernel(g_ref, halo_ref, x_ref, wdw_ref, bdw_ref, lng_ref, lnb_ref, w2_ref, b2_ref,
                 mg_ref, mb_ref, o_ref, buf_ref, cv_ref):
    i = pl.program_id(0)
    first = (i % (SEQ // TM_CONV)) == 0
    buf_ref[0:CONV_HALO, :] = jnp.where(first, 0.0, halo_ref[...])
    buf_ref[CONV_HALO:CONV_HALO + TM_CONV, :] = g_ref[...]
    buf_ref[CONV_HALO + TM_CONV:, :] = jnp.zeros((SUBLANES, D_MODEL), F32)
    sub = lax.broadcasted_iota(jnp.int32, (SUBLANES, LANE), 0)
    n_q = (CONV_BASE + CONV_WIDTH - 1) // SUBLANES + 1
    for c in range(D_MODEL // LANE):
        cs = slice(c * LANE, (c + 1) * LANE)
        w_b = [jnp.broadcast_to(wdw_ref[j:j + 1, cs], (SUBLANES, LANE)) for j in range(CONV_WIDTH)]
        bias = jnp.broadcast_to(bdw_ref[:, cs], (SUBLANES, LANE))

        def partials(v):
            tiles = [buf_ref[pl.ds(v + SUBLANES * q, SUBLANES), cs] for q in range(n_q)]
            out = []
            for s in range(SUBLANES):
                a = None
                for q in range(n_q):
                    j = SUBLANES * q + s - CONV_BASE
                    if 0 <= j < CONV_WIDTH:
                        term = w_b[j] * tiles[q]
                        a = term if a is None else a + term
                out.append(a)
            return tuple(out)

        def row_body(t, prev):
            r = pl.multiple_of(t * SUBLANES, SUBLANES)
            cur = partials(pl.multiple_of(r + SUBLANES, SUBLANES))
            acc = bias + prev[0]
            for s in range(1, SUBLANES):
                mixed = jnp.where(sub >= s, prev[s], cur[s])
                acc = acc + pltpu.roll(mixed, SUBLANES - s, axis=0)
            cv_ref[pl.ds(r, SUBLANES), cs] = acc
            return cur

        lax.fori_loop(0, TM_CONV // SUBLANES, row_body, partials(0), unroll=2)
    h = _layer_norm(cv_ref[...], lng_ref[...], lnb_ref[...])
    h = h * _sigmoid(h)
    mix = jnp.dot(h.astype(BF16), w2_ref[...], preferred_element_type=F32) + b2_ref[...]
    o_ref[...] = _layer_norm(ALPHA * x_ref[...] + mix, mg_ref[...], mb_ref[...])


def _conv_back(g, x, wdw, bdw, lng, lnb, w2, b2, mg, mb):
    row = lambda i: (i, 0)
    fixed = lambda i: (0, 0)
    vec = pl.BlockSpec((1, D_MODEL), fixed)
    halo_blocks = TM_CONV // CONV_HALO
    return pl.pallas_call(
        _conv_kernel,
        grid=(T // TM_CONV,),
        in_specs=[
            pl.BlockSpec((TM_CONV, D_MODEL), row),
            pl.BlockSpec((CONV_HALO, D_MODEL), lambda i: (jnp.maximum(i * halo_blocks - 1, 0), 0)),
            pl.BlockSpec((TM_CONV, D_MODEL), row),
            pl.BlockSpec((CONV_HALO, D_MODEL), fixed),
            vec, vec, vec,
            pl.BlockSpec((D_MODEL, D_MODEL), fixed),
            vec, vec, vec,
        ],
        out_specs=pl.BlockSpec((TM_CONV, D_MODEL), row),
        out_shape=jax.ShapeDtypeStruct((T, D_MODEL), F32),
        scratch_shapes=[
            pltpu.VMEM((TM_CONV + CONV_HALO + SUBLANES, D_MODEL), F32),
            pltpu.VMEM((TM_CONV, D_MODEL), F32),
        ],
        compiler_params=_params(("parallel",)),
        name="conv_back",
    )(g, g, x, wdw, bdw, lng, lnb, w2, b2, mg, mb)


def _ranks_before(vals):
    n = len(vals)
    ranks = []
    for j in range(n):
        r = jnp.zeros_like(vals[j])
        for i in range(n):
            if i == j:
                continue
            before = (vals[i] > vals[j]) | ((vals[i] == vals[j]) & (i < j))
            r = r + before.astype(F32)
        ranks.append(r)
    return ranks


PAIRS = [(a, b) for a in range(EXPERTS_PER_GROUP) for b in range(a + 1, EXPERTS_PER_GROUP)]
N_CLASSES = N_GROUPS * len(PAIRS)
CLASS_ROWS = 32
CLASS_EXPERTS = [(g * EXPERTS_PER_GROUP + a, g * EXPERTS_PER_GROUP + b)
                 for g in range(N_GROUPS) for (a, b) in PAIRS]
TM_G = 256
NT_MAX = -(-(T + N_CLASSES * (TM_G - 1)) // TM_G)
P_MAX = NT_MAX * TM_G
XG_W = D_MODEL + LANE
GATE_ROWS = 8
TD = 256
NT_MIN = T // TM_G
N_ZERO_TILES = N_CLASSES + NT_MAX - NT_MIN


def _route_kernel(x_ref, wrt_ref, rb_ref, gw_ref, cls_ref, pos_ref, tot_ref, tri_ref, carry_ref):
    i = pl.program_id(0)

    @pl.when(i == 0)
    def _():
        r = lax.broadcasted_iota(jnp.int32, (TM_ROUTER, TM_ROUTER), 0)
        c = lax.broadcasted_iota(jnp.int32, (TM_ROUTER, TM_ROUTER), 1)
        tri_ref[...] = jnp.where(r < c, 1.0, 0.0).astype(BF16)
        carry_ref[...] = jnp.zeros_like(carry_ref)

    x = x_ref[...]
    x_hi = x.astype(BF16)
    x_mid = (x - x_hi.astype(F32)).astype(BF16)
    both = lax.dot_general(wrt_ref[...], x_hi, _NT, preferred_element_type=F32)
    logits = (both[:N_EXPERTS] + both[N_EXPERTS:]
              + lax.dot_general(wrt_ref[:N_EXPERTS, :], x_mid, _NT, preferred_element_type=F32))
    aff = _sigmoid(logits)
    sel = aff + rb_ref[...]
    aff_rows = [aff[e:e + 1, :] for e in range(N_EXPERTS)]
    sel_rows = [sel[e:e + 1, :] for e in range(N_EXPERTS)]
    in_top2 = []
    scores = []
    for g in range(N_GROUPS):
        members = sel_rows[g * EXPERTS_PER_GROUP:(g + 1) * EXPERTS_PER_GROUP]
        ranks = _ranks_before(members)
        top = [r < 2.0 for r in ranks]
        in_top2.extend(top)
        s = jnp.zeros_like(members[0])
        for v, t in zip(members, top):
            s = s + jnp.where(t, v, 0.0)
        scores.append(s)
    g_ranks = _ranks_before(scores)
    w_rows = []
    for e in range(N_EXPERTS):
        chosen = (g_ranks[e // EXPERTS_PER_GROUP] < 1.0) & in_top2[e]
        w_rows.append(jnp.where(chosen, aff_rows[e], 0.0))
    denom = w_rows[0]
    for e in range(1, N_EXPERTS):
        denom = denom + w_rows[e]
    inv = 1.0 / denom

    masks = []
    wa = jnp.zeros_like(denom)
    wb = jnp.zeros_like(denom)
    for c, (ea, eb) in enumerate(CLASS_EXPERTS):
        m = (g_ranks[c // len(PAIRS)] < 1.0) & in_top2[ea] & in_top2[eb]
        masks.append(m.astype(F32))
        wa = wa + jnp.where(m, aff_rows[ea], 0.0)
        wb = wb + jnp.where(m, aff_rows[eb], 0.0)
    zero_row = jnp.zeros_like(denom)
    onehot = jnp.concatenate(masks + [zero_row] * (CLASS_ROWS - N_CLASSES), axis=0)

    before = jnp.dot(onehot.astype(BF16), tri_ref[...], preferred_element_type=F32)
    carry = carry_ref[...]
    class_id = lax.broadcasted_iota(jnp.int32, (CLASS_ROWS, 1), 0).astype(F32)
    pos_ref[...] = jnp.sum(onehot * (before + carry), axis=0, keepdims=True).astype(jnp.int32)
    cls_ref[...] = jnp.sum(onehot * class_id, axis=0, keepdims=True).astype(jnp.int32)
    carry = carry + jnp.sum(onehot, axis=1, keepdims=True)
    carry_ref[...] = carry
    tot_ref[...] = carry

    gw_ref[...] = jnp.concatenate([wa * inv, wb * inv] + [zero_row] * (GATE_ROWS - 2), axis=0)


def _route(x, wrt, rb):
    return pl.pallas_call(
        _route_kernel,
        grid=(T // TM_ROUTER,),
        in_specs=[
            pl.BlockSpec((TM_ROUTER, D_MODEL), lambda i: (i, 0)),
            pl.BlockSpec((2 * N_EXPERTS, D_MODEL), lambda i: (0, 0)),
            pl.BlockSpec((N_EXPERTS, 1), lambda i: (0, 0)),
        ],
        out_specs=[
            pl.BlockSpec((GATE_ROWS, TM_ROUTER), lambda i: (0, i)),
            pl.BlockSpec((1, TM_ROUTER), lambda i: (0, i)),
            pl.BlockSpec((1, TM_ROUTER), lambda i: (0, i)),
            pl.BlockSpec((CLASS_ROWS, 1), lambda i: (0, 0)),
        ],
        out_shape=[
            jax.ShapeDtypeStruct((GATE_ROWS, T), F32),
            jax.ShapeDtypeStruct((1, T), jnp.int32),
            jax.ShapeDtypeStruct((1, T), jnp.int32),
            jax.ShapeDtypeStruct((CLASS_ROWS, 1), F32),
        ],
        scratch_shapes=[
            pltpu.VMEM((TM_ROUTER, TM_ROUTER), BF16),
            pltpu.VMEM((CLASS_ROWS, 1), F32),
        ],
        compiler_params=_params(("arbitrary",)),
        name="route",
    )(x, wrt, rb)


def _start_all(copies):
    for cp in copies:
        cp.start()


def _row_copies(pay, slot, dst_hbm, dest_ref, base, sem):
    return [pltpu.make_async_copy(pay.at[slot, r], dst_hbm.at[dest_ref[base + r]], sem.at[slot])
            for r in range(TD)]


def _dispatch_kernel(dest_ref, zstart_ref, x_ref, gw_ref, wg_ref, wu_ref, wd_ref,
                     xs_hbm, wgb_ref, wub_ref, wdb_ref, pay, zbuf, zsem, sem):
    i = pl.program_id(0)
    last = pl.num_programs(0) - 1

    wgb_ref[0] = wg_ref[0, 0].astype(BF16)
    wub_ref[0] = wu_ref[0, 0].astype(BF16)
    wdb_ref[0] = wd_ref[0, 0].astype(BF16)

    @pl.when(i == 0)
    def _():
        zbuf[...] = jnp.zeros_like(zbuf)

        def zero_tile(c):
            start = pl.multiple_of(zstart_ref[c], TM_G)
            return pltpu.make_async_copy(zbuf, xs_hbm.at[pl.ds(start, TM_G)], zsem)

        for c in range(N_ZERO_TILES):
            @pl.when(zstart_ref[c] >= 0)
            def _():
                zero_tile(c).start()
        for c in range(N_ZERO_TILES):
            @pl.when(zstart_ref[c] >= 0)
            def _():
                zero_tile(c).wait()

    for parity in range(2):
        @pl.when(i % 2 == parity)
        def _():
            pay[parity, :, :D_MODEL] = x_ref[...]
            gates = jnp.concatenate(
                [gw_ref[...], jnp.zeros((LANE - GATE_ROWS, TD), F32)], axis=0)
            pay[parity, :, D_MODEL:] = gates.T
            _start_all(_row_copies(pay, parity, xs_hbm, dest_ref, i * TD, sem))

            @pl.when(i > 0)
            def _():
                for cp in _row_copies(pay, 1 - parity, xs_hbm, dest_ref, (i - 1) * TD, sem):
                    cp.wait()

            @pl.when(i == last)
            def _():
                for cp in _row_copies(pay, parity, xs_hbm, dest_ref, i * TD, sem):
                    cp.wait()


W_PARTS = (T // TD) // N_EXPERTS
assert W_PARTS * N_EXPERTS == T // TD


def _dispatch(dest, zstart, x, gw, wg, wu, wd, layer):
    up_rows, down_rows = D_MODEL // W_PARTS, D_EXPERT // W_PARTS
    w_in = lambda i, d, z: (layer, i // W_PARTS, i % W_PARTS, 0)
    w_out = lambda i, d, z: (i // W_PARTS, i % W_PARTS, 0)
    return pl.pallas_call(
        _dispatch_kernel,
        grid_spec=pltpu.PrefetchScalarGridSpec(
            num_scalar_prefetch=2,
            grid=(T // TD,),
            in_specs=[
                pl.BlockSpec((TD, D_MODEL), lambda i, d, z: (i, 0)),
                pl.BlockSpec((GATE_ROWS, TD), lambda i, d, z: (0, i)),
                pl.BlockSpec((1, 1, up_rows, D_EXPERT), w_in),
                pl.BlockSpec((1, 1, up_rows, D_EXPERT), w_in),
                pl.BlockSpec((1, 1, down_rows, D_MODEL), w_in),
            ],
            out_specs=[
                pl.BlockSpec(memory_space=pl.ANY),
                pl.BlockSpec((1, up_rows, D_EXPERT), w_out),
                pl.BlockSpec((1, up_rows, D_EXPERT), w_out),
                pl.BlockSpec((1, down_rows, D_MODEL), w_out),
            ],
            scratch_shapes=[
                pltpu.VMEM((2, TD, XG_W), F32),
                pltpu.VMEM((TM_G, XG_W), F32),
                pltpu.SemaphoreType.DMA,
                pltpu.SemaphoreType.DMA((2,)),
            ],
        ),
        out_shape=[
            jax.ShapeDtypeStruct((P_MAX, XG_W), F32),
            jax.ShapeDtypeStruct((N_EXPERTS, D_MODEL, D_EXPERT), BF16),
            jax.ShapeDtypeStruct((N_EXPERTS, D_MODEL, D_EXPERT), BF16),
            jax.ShapeDtypeStruct((N_EXPERTS, D_EXPERT, D_MODEL), BF16),
        ],
        compiler_params=_params(("arbitrary",)),
        name="dispatch",
    )(dest, zstart, x, gw, wg, wu, wd)


def _experts_kernel(te_ref, nt_ref, xs_ref, wga_ref, wua_ref, wda_ref, wgb_ref, wub_ref, wdb_ref,
                    ys_ref):
    i = pl.program_id(0)

    @pl.when(i < nt_ref[0])
    def _():
        xb = xs_ref[:, :D_MODEL].astype(BF16)
        gates = xs_ref[:, D_MODEL:]
        y = None
        for which, (wg_ref, wu_ref, wd_ref) in enumerate(((wga_ref, wua_ref, wda_ref),
                                                          (wgb_ref, wub_ref, wdb_ref))):
            hg = jnp.dot(xb, wg_ref[0], preferred_element_type=F32)
            hu = jnp.dot(xb, wu_ref[0], preferred_element_type=F32)
            h = hg * _sigmoid(hg) * hu * gates[:, which:which + 1]
            part = jnp.dot(h.astype(BF16), wd_ref[0], preferred_element_type=F32)
            y = part if y is None else y + part
        ys_ref[...] = y

    @pl.when(i >= nt_ref[0])
    def _():
        ys_ref[...] = jnp.zeros_like(ys_ref)


def _experts(tile_e, n_tiles, xs, wg, wu, wd):
    def row_map(i, te, nt):
        return (jnp.minimum(i, nt[0] - 1), 0)

    def w_map(which):
        return lambda i, te, nt: (te[which * NT_MAX + i], 0, 0)

    up_spec = lambda which: pl.BlockSpec((1, D_MODEL, D_EXPERT), w_map(which))
    down_spec = lambda which: pl.BlockSpec((1, D_EXPERT, D_MODEL), w_map(which))
    return pl.pallas_call(
        _experts_kernel,
        grid_spec=pltpu.PrefetchScalarGridSpec(
            num_scalar_prefetch=2,
            grid=(NT_MAX,),
            in_specs=[
                pl.BlockSpec((TM_G, XG_W), row_map),
                up_spec(0), up_spec(0), down_spec(0),
                up_spec(1), up_spec(1), down_spec(1),
            ],
            out_specs=pl.BlockSpec((TM_G, D_MODEL), lambda i, te, nt: (i, 0)),
        ),
        out_shape=jax.ShapeDtypeStruct((P_MAX, D_MODEL), F32),
        compiler_params=_params(("arbitrary",)),
        name="experts",
    )(tile_e, n_tiles, xs, wg, wu, wd, wg, wu, wd)


def _gather_copies(ys_hbm, buf, dest_ref, base, slot, sem):
    return [pltpu.make_async_copy(ys_hbm.at[dest_ref[base + r]], buf.at[slot, r], sem.at[slot])
            for r in range(TD)]


def _combine_kernel(dest_ref, x_ref, ys_hbm, lg_ref, lb_ref, *rest, q_scale):
    if q_scale is None:
        o_ref, buf, sem = rest
    else:
        wqt_ref, o_ref, qt_ref, buf, sem = rest
    i = pl.program_id(0)
    last = pl.num_programs(0) - 1

    @pl.when(i == 0)
    def _():
        _start_all(_gather_copies(ys_hbm, buf, dest_ref, 0, 0, sem))

    for parity in range(2):
        @pl.when(i % 2 == parity)
        def _():
            @pl.when(i < last)
            def _():
                _start_all(_gather_copies(ys_hbm, buf, dest_ref, (i + 1) * TD, 1 - parity, sem))

            for cp in _gather_copies(ys_hbm, buf, dest_ref, i * TD, parity, sem):
                cp.wait()
            xn = _layer_norm(ALPHA * x_ref[...] + buf[parity], lg_ref[...], lb_ref[...])
            o_ref[...] = xn
            if q_scale is not None:
                q = lax.dot_general(wqt_ref[...], xn.astype(BF16), _NT, preferred_element_type=F32)
                qt_ref[...] = (q * q_scale).astype(BF16)


def _combine(dest, x, ys, lg, lb, wq_t=None, q_scale=None):
    vec = pl.BlockSpec((1, D_MODEL), lambda i, d: (0, 0))
    rows = pl.BlockSpec((TD, D_MODEL), lambda i, d: (i, 0))
    in_specs = [rows, pl.BlockSpec(memory_space=pl.ANY), vec, vec]
    out_specs = [rows]
    out_shape = [jax.ShapeDtypeStruct((T, D_MODEL), F32)]
    operands = [dest, x, ys, lg, lb]
    if wq_t is not None:
        in_specs.append(pl.BlockSpec((QK_WIDTH, D_MODEL), lambda i, d: (0, 0)))
        out_specs.append(pl.BlockSpec((QK_WIDTH, TD), lambda i, d: (0, i)))
        out_shape.append(jax.ShapeDtypeStruct((QK_WIDTH, T), BF16))
        operands.append(wq_t)
    outs = pl.pallas_call(
        functools.partial(_combine_kernel, q_scale=q_scale if wq_t is not None else None),
        grid_spec=pltpu.PrefetchScalarGridSpec(
            num_scalar_prefetch=1,
            grid=(T // TD,),
            in_specs=in_specs,
            out_specs=out_specs,
            scratch_shapes=[
                pltpu.VMEM((2, TD, D_MODEL), F32),
                pltpu.SemaphoreType.DMA((2,)),
            ],
        ),
        out_shape=out_shape,
        compiler_params=_params(("arbitrary",)),
        name="combine",
    )(*operands)
    return outs[0] if wq_t is None else tuple(outs)


TM_G_LOG2 = TM_G.bit_length() - 1
assert 1 << TM_G_LOG2 == TM_G


def _plan_kernel(cnt_ref, cls_ref, pos_ref, dest_ref, zstart_ref, te_ref, nt_ref):
    shr = lax.shift_right_logical
    run = jnp.int32(0)
    starts, ends = [], []
    for c in range(N_CLASSES):
        padded = shr(cnt_ref[c] + (TM_G - 1), TM_G_LOG2) * TM_G
        starts.append(run)
        run = run + padded
        ends.append(run)
        zstart_ref[c] = jnp.where(padded > 0, run - TM_G, -1)
    n_tiles = shr(run, TM_G_LOG2)
    nt_ref[0] = n_tiles
    for t in range(NT_MIN, NT_MAX):
        zstart_ref[N_CLASSES + t - NT_MIN] = jnp.where(t >= n_tiles, t * TM_G, -1)

    def tile_body(t, carry):
        row0 = jnp.minimum(t, n_tiles - 1) * TM_G
        ea = jnp.int32(0)
        eb = jnp.int32(0)
        for c in range(N_CLASSES):
            inside = (row0 >= starts[c]) & (row0 < ends[c])
            ea = jnp.where(inside, CLASS_EXPERTS[c][0], ea)
            eb = jnp.where(inside, CLASS_EXPERTS[c][1], eb)
        te_ref[t] = ea
        te_ref[NT_MAX + t] = eb
        return carry

    lax.fori_loop(0, NT_MAX, tile_body, 0)

    cls = cls_ref[...]
    dest = pos_ref[...]
    for c in range(N_CLASSES):
        dest = dest + jnp.where(cls == c, starts[c], 0)
    dest_ref[...] = dest


def _plan(counts, cls, pos):
    smem = pl.BlockSpec(memory_space=pltpu.SMEM)
    vmem = pl.BlockSpec(memory_space=pltpu.VMEM)
    return pl.pallas_call(
        _plan_kernel,
        in_specs=[smem, vmem, vmem],
        out_specs=[vmem, smem, smem, smem],
        out_shape=[
            jax.ShapeDtypeStruct((1, T), jnp.int32),
            jax.ShapeDtypeStruct((N_ZERO_TILES,), jnp.int32),
            jax.ShapeDtypeStruct((2 * NT_MAX,), jnp.int32),
            jax.ShapeDtypeStruct((1,), jnp.int32),
        ],
        name="plan",
    )(counts, cls, pos)


def _moe(x, wrt, rb, wg, wu, wd, layer, lg, lb, wq_t=None, q_scale=None):
    gw, cls, pos, tot = _route(x, wrt, rb)
    dest, zstart, tile_e, n_tiles = _plan(tot[:, 0].astype(jnp.int32), cls, pos)
    dest = dest.reshape(T)
    xs, wg_b, wu_b, wd_b = _dispatch(dest, zstart, x, gw, wg, wu, wd, layer)
    ys = _experts(tile_e, n_tiles, xs, wg_b, wu_b, wd_b)
    return _combine(dest, x, ys, lg, lb, wq_t, q_scale)


def _kv_proj_kernel(x_ref, wk_ref, wvt_ref, k_ref, vt_ref):
    xb = x_ref[...].astype(BF16)
    k_ref[...] = jnp.dot(xb, wk_ref[...], preferred_element_type=F32).astype(BF16)
    vt_ref[0] = lax.dot_general(wvt_ref[...], xb, _NT, preferred_element_type=F32).astype(BF16)


def _kv_proj(x, wk, wvt):
    return pl.pallas_call(
        _kv_proj_kernel,
        grid=(T // TK,),
        in_specs=[
            pl.BlockSpec((TK, D_MODEL), lambda i: (i, 0)),
            pl.BlockSpec((D_MODEL, QK_WIDTH), lambda i: (0, 0)),
            pl.BlockSpec((V_WIDTH, D_MODEL), lambda i: (0, 0)),
        ],
        out_specs=[
            pl.BlockSpec((TK, QK_WIDTH), lambda i: (i, 0)),
            pl.BlockSpec((1, V_WIDTH, TK), lambda i: (i, 0, 0)),
        ],
        out_shape=[
            jax.ShapeDtypeStruct((T, QK_WIDTH), BF16),
            jax.ShapeDtypeStruct((T // TK, V_WIDTH, TK), BF16),
        ],
        compiler_params=_params(("parallel",)),
        name="kv_proj",
    )(x, wk, wvt)


def _proj_ln_kernel(a_ref, x_ref, w_ref, g_ref, b_ref, o_ref):
    mix = jnp.dot(a_ref[...], w_ref[...], preferred_element_type=F32)
    o_ref[...] = _layer_norm(ALPHA * x_ref[...] + mix, g_ref[...], b_ref[...])


def _proj_ln(a, x, w, g, b):
    vec = pl.BlockSpec((1, D_MODEL), lambda i: (0, 0))
    return pl.pallas_call(
        _proj_ln_kernel,
        grid=(T // TM_PROJ,),
        in_specs=[
            pl.BlockSpec((TM_PROJ, V_WIDTH), lambda i: (i, 0)),
            pl.BlockSpec((TM_PROJ, D_MODEL), lambda i: (i, 0)),
            pl.BlockSpec((V_WIDTH, D_MODEL), lambda i: (0, 0)),
            vec, vec,
        ],
        out_specs=pl.BlockSpec((TM_PROJ, D_MODEL), lambda i: (i, 0)),
        out_shape=jax.ShapeDtypeStruct((T, D_MODEL), F32),
        compiler_params=_params(("parallel",)),
        name="proj_ln",
    )(a, x, w, g, b)


assert MAX_DISTANCE <= LANE
def _bias_kernel(rb_ref, diag_ref, corner_ref):
    h = pl.program_id(0)
    max_exact = N_BUCKETS // 2
    far = rb_ref[N_BUCKETS - 1, h]

    def tile(shape, offset):
        c = lax.broadcasted_iota(jnp.int32, shape, 0)
        r = lax.broadcasted_iota(jnp.int32, shape, 1)
        rel = offset + r - c
        n = jnp.maximum(rel, 0)
        nf = jnp.maximum(n, 1).astype(F32)
        large = max_exact + (jnp.log(nf / max_exact) / math.log(MAX_DISTANCE / max_exact)
                             * (N_BUCKETS - max_exact)).astype(jnp.int32)
        large = jnp.minimum(large, N_BUCKETS - 1)
        bucket = jnp.where(n < max_exact, n, large)
        bias = jnp.zeros(shape, F32)
        for b in range(N_BUCKETS):
            bias = jnp.where(bucket == b, rb_ref[b, h] - far, bias)
        return jnp.where(rel >= 0, bias * LOG2E, NEG_BIG)

    diag_ref[0] = tile((TK, TQ), 0)
    corner_ref[0] = tile((LANE, LANE), LANE)


def _bias_tiles(rel_bias):
    return pl.pallas_call(
        _bias_kernel,
        grid=(N_HEADS,),
        in_specs=[pl.BlockSpec(memory_space=pltpu.SMEM)],
        out_specs=[
            pl.BlockSpec((1, TK, TQ), lambda h: (h, 0, 0)),
            pl.BlockSpec((1, LANE, LANE), lambda h: (h, 0, 0)),
        ],
        out_shape=[
            jax.ShapeDtypeStruct((N_HEADS, TK, TQ), F32),
            jax.ShapeDtypeStruct((N_HEADS, LANE, LANE), F32),
        ],
        compiler_params=_params(("parallel",)),
        name="bias_tiles",
    )(rel_bias)


def _attn_kernel(q1t_ref, q2t_ref, q1n_ref, q2n_ref, k1_ref, k2_ref, vt_ref, diag_ref, corner_ref,
                 lam_ref, sg_ref, o_ref, s_ref, mx_ref, m_ref, l_ref, acc_ref, *, lambda_init):
    qi = pl.program_id(2)
    dim = lax.broadcasted_iota(jnp.int32, (LANE, 1), 0)
    lo = dim < HEAD_DIM

    def head_masked(q1, q2):
        zero = jnp.zeros_like(q1)
        return [jnp.where(lo, q1, zero), jnp.where(lo, q2, zero),
                jnp.where(lo, zero, q1), jnp.where(lo, zero, q2)]

    qs = head_masked(q1t_ref[...], q2t_ref[...])
    qs_next = head_masked(q1n_ref[...], q2n_ref[...])

    def reset_stats():
        m_ref[...] = jnp.full(m_ref.shape, NEG_BIG, F32)
        l_ref[...] = jnp.zeros(l_ref.shape, F32)
        acc_ref[...] = jnp.zeros(acc_ref.shape, F32)

    def scores(j, slot, q_slots=qs):
        k0 = pl.multiple_of(j * TK, TK)
        k_ref = k1_ref if slot % 2 == 0 else k2_ref
        st = jnp.dot(k_ref[pl.ds(k0, TK), :], q_slots[slot], preferred_element_type=F32)
        s_ref[slot] = st
        mx_ref[slot] = jnp.max(st, axis=0, keepdims=True)

    def update(j, slot, kind):
        head = slot // 2
        if kind == 1:
            rows = slice(TK - LANE, TK)
            corner = s_ref[slot, rows, 0:LANE] + corner_ref[head]
            s_ref[slot, rows, 0:LANE] = corner
            mx_ref[slot, :, 0:LANE] = jnp.maximum(mx_ref[slot, :, 0:LANE],
                                                  jnp.max(corner, axis=0, keepdims=True))
        st = s_ref[slot]
        if kind == 0:
            st = st + diag_ref[head]
            mx = jnp.max(st, axis=0, keepdims=True)
        else:
            mx = mx_ref[slot]
        m_prev = m_ref[slot]
        m_new = jnp.maximum(m_prev, mx)
        alpha = jnp.exp2(m_prev - m_new)
        p = jnp.exp2(st - m_new)
        l_ref[slot] = alpha * l_ref[slot] + jnp.sum(p, axis=0, keepdims=True)
        vt = vt_ref[j, head * V_DIM:(head + 1) * V_DIM, :]
        acc_ref[slot] = alpha * acc_ref[slot] + jnp.dot(vt, p.astype(BF16),
                                                        preferred_element_type=F32)
        m_ref[slot] = m_new

    def tile(j, kind, nxt):
        for slot in range(4):
            ahead = slot + 2
            if ahead < 4:
                scores(j, ahead)
            elif nxt is None:
                scores(0, ahead - 4, qs_next)
            else:
                scores(nxt, ahead - 4)
            update(j, slot, kind)

    @pl.when(qi == 0)
    def _():
        reset_stats()
        scores(0, 0)
        scores(0, 1)

    n_far = jnp.maximum(qi - 1, 0)
    n_groups = n_far // FAR_UNROLL

    def far_group(t, carry):
        for u in range(FAR_UNROLL):
            tile(FAR_UNROLL * t + u, None, FAR_UNROLL * t + u + 1)
        return carry

    lax.fori_loop(0, n_groups, far_group, 0)

    def far_single(j, carry):
        tile(j, None, j + 1)
        return carry

    lax.fori_loop(n_groups * FAR_UNROLL, n_far, far_single, 0)

    @pl.when(qi >= 1)
    def _():
        tile(qi - 1, 1, qi)

    tile(qi, 0, None)

    lp = lam_ref[...]
    lam = (jnp.exp(jnp.sum(lp[0:1] * lp[1:2], axis=1, keepdims=True))
           - jnp.exp(jnp.sum(lp[2:3] * lp[3:4], axis=1, keepdims=True)) + lambda_init)
    for head in range(2):
        a1 = acc_ref[2 * head] / l_ref[2 * head]
        a2 = acc_ref[2 * head + 1] / l_ref[2 * head + 1]
        of = (a1 - lam * a2).T
        of = of * lax.rsqrt(jnp.mean(of * of, axis=-1, keepdims=True) + LN_EPS) * sg_ref[...]
        of = of * (1.0 - lambda_init)
        o_ref[:, head * V_DIM:(head + 1) * V_DIM] = of.astype(o_ref.dtype)
    reset_stats()


def _attention(qt, k, vt, bias_diag, bias_corner, lam_params, subln_g, lambda_init):
    nq = SEQ // TQ
    nk = SEQ // TK
    qk_blocks = QK_WIDTH // 2 // LANE
    return pl.pallas_call(
        functools.partial(_attn_kernel, lambda_init=lambda_init),
        grid=(BATCH, N_PAIRS, nq),
        in_specs=[
            pl.BlockSpec((LANE, TQ), lambda b, p, i: (p, b * nq + i)),
            pl.BlockSpec((LANE, TQ), lambda b, p, i: (qk_blocks + p, b * nq + i)),
            pl.BlockSpec((LANE, TQ), lambda b, p, i: (p, b * nq + jnp.minimum(i + 1, nq - 1))),
            pl.BlockSpec((LANE, TQ),
                         lambda b, p, i: (qk_blocks + p, b * nq + jnp.minimum(i + 1, nq - 1))),
            pl.BlockSpec((SEQ, LANE), lambda b, p, i: (b, p)),
            pl.BlockSpec((SEQ, LANE), lambda b, p, i: (b, qk_blocks + p)),
            pl.BlockSpec((nk, 2 * V_DIM, TK), lambda b, p, i: (b, p, 0)),
            pl.BlockSpec((2, TK, TQ), lambda b, p, i: (p, 0, 0)),
            pl.BlockSpec((2, LANE, LANE), lambda b, p, i: (p, 0, 0)),
            pl.BlockSpec((4, HEAD_DIM), lambda b, p, i: (0, 0)),
            pl.BlockSpec((1, V_DIM), lambda b, p, i: (0, 0)),
        ],
        out_specs=pl.BlockSpec((TQ, 2 * V_DIM), lambda b, p, i: (b * nq + i, p)),
        out_shape=jax.ShapeDtypeStruct((T, V_WIDTH), BF16),
        scratch_shapes=[
            pltpu.VMEM((4, TK, TQ), F32),
            pltpu.VMEM((4, 1, TQ), F32),
            pltpu.VMEM((4, 1, TQ), F32),
            pltpu.VMEM((4, 1, TQ), F32),
            pltpu.VMEM((4, V_DIM, TQ), F32),
        ],
        compiler_params=_params(("parallel", "parallel", "arbitrary"), vmem_mb=56),
        name="diff_attention",
    )(qt, qt, qt, qt, k, k, vt, bias_diag, bias_corner, lam_params, subln_g)


def kernel(x, a_w_pw1, a_b_pw1, a_w_dw, a_b_dw, a_ln_g, a_ln_b, a_w_pw2, a_b_pw2, w_kv, b_w_q,
           b_lambda, b_subln_g, b_w_o, rel_bias, ln_mix_g, ln_mix_b, ln_ffn_g, ln_ffn_b,
           router_w, router_bias, moe_w_gate, moe_w_up, moe_w_down):
    x = x.reshape(T, D_MODEL)
    row = lambda v: v.reshape(1, -1)
    wr_hi = router_w.T.astype(BF16)
    wr_mid = (router_w.T - wr_hi.astype(F32)).astype(BF16)
    wrt = jnp.concatenate([wr_hi, wr_mid], axis=0)
    rb = router_bias.reshape(N_EXPERTS, 1)
    k_all = vt_all = bias_diag = bias_corner = qt = None
    for l in range(DEPTH):
        if l < N_A:
            g = _glu(x, a_w_pw1[l].astype(BF16), row(a_b_pw1[l]))
            wdw = jnp.pad(a_w_dw[l], ((0, CONV_HALO - CONV_WIDTH), (0, 0)))
            x = _conv_back(g, x, wdw, row(a_b_dw[l]), row(a_ln_g[l]), row(a_ln_b[l]),
                           a_w_pw2[l].astype(BF16), row(a_b_pw2[l]),
                           row(ln_mix_g[l]), row(ln_mix_b[l]))
        else:
            if l == N_A:
                k_all, vt_all = _kv_proj(x, w_kv[:, :QK_WIDTH].astype(BF16),
                                         w_kv[:, QK_WIDTH:].T.astype(BF16))
                bias_diag, bias_corner = _bias_tiles(rel_bias)
            j = l - N_A
            lambda_init = 0.8 - 0.6 * math.exp(-0.3 * l)
            o = _attention(qt, k_all, vt_all, bias_diag, bias_corner, b_lambda[j],
                           row(b_subln_g[j]), lambda_init)
            x = _proj_ln(o, x, b_w_o[j].astype(BF16), row(ln_mix_g[l]), row(ln_mix_b[l]))
        moe_args = (x, wrt, rb, moe_w_gate, moe_w_up, moe_w_down, l,
                    row(ln_ffn_g[l]), row(ln_ffn_b[l]))
        if N_A <= l + 1 < DEPTH:
            x, qt = _moe(*moe_args, wq_t=b_w_q[l + 1 - N_A].T.astype(BF16),
                         q_scale=HEAD_DIM ** -0.5 * LOG2E)
        else:
            x = _moe(*moe_args)
    return x.reshape(BATCH, SEQ, D_MODEL)
```

```python
import functools
import math

import jax
import jax.numpy as jnp
from jax import lax
from jax.experimental import pallas as pl
from jax.experimental.pallas import tpu as pltpu

D_MODEL = 1024
BATCH = 2
SEQ = 8192
DEPTH = 4
N_A = DEPTH // 2
CONV_WIDTH = 31
N_HEADS = 8
HEAD_DIM = 64
V_DIM = 2 * HEAD_DIM
QK_WIDTH = 2 * N_HEADS * HEAD_DIM
V_WIDTH = N_HEADS * V_DIM
N_BUCKETS = 32
MAX_DISTANCE = 128
N_EXPERTS = 16
N_GROUPS = 4
EXPERTS_PER_GROUP = N_EXPERTS // N_GROUPS
D_EXPERT = 512
ALPHA = (2.0 * DEPTH) ** 0.25
LN_EPS = 1e-5

T = BATCH * SEQ
F32 = jnp.float32
BF16 = jnp.bfloat16
LOG2E = 1.4426950408889634
NEG_BIG = -1e30
LANE = 128
SUBLANES = 8
CONV_HALO = 32
N_PAIRS = N_HEADS // 2

TM_GLU = 512
TM_CONV = 256
TM_ROUTER = 1024
TM_PROJ = 512
TQ = 512
TK = 512
FAR_UNROLL = 4


def _params(sem, vmem_mb=None):
    kw = dict(dimension_semantics=sem)
    if vmem_mb is not None:
        kw["vmem_limit_bytes"] = vmem_mb * 1024 * 1024
    return pltpu.CompilerParams(**kw)


def _layer_norm(v, g, b):
    mu = jnp.mean(v, axis=-1, keepdims=True)
    d = v - mu
    var = jnp.mean(d * d, axis=-1, keepdims=True)
    return d * lax.rsqrt(var + LN_EPS) * g + b


def _sigmoid(v):
    return 1.0 / (1.0 + jnp.exp(-v))


_NT = (((1,), (1,)), ((), ()))


def _glu_kernel(x_ref, w_ref, b_ref, o_ref):
    xb = x_ref[...].astype(BF16)
    a = jnp.dot(xb, w_ref[:, :D_MODEL], preferred_element_type=F32) + b_ref[:, :D_MODEL]
    gate = jnp.dot(xb, w_ref[:, D_MODEL:], preferred_element_type=F32) + b_ref[:, D_MODEL:]
    o_ref[...] = a * _sigmoid(gate)


def _glu(x, w, b):
    return pl.pallas_call(
        _glu_kernel,
        grid=(T // TM_GLU,),
        in_specs=[
            pl.BlockSpec((TM_GLU, D_MODEL), lambda i: (i, 0)),
            pl.BlockSpec((D_MODEL, 2 * D_MODEL), lambda i: (0, 0)),
            pl.BlockSpec((1, 2 * D_MODEL), lambda i: (0, 0)),
        ],
        out_specs=pl.BlockSpec((TM_GLU, D_MODEL), lambda i: (i, 0)),
        out_shape=jax.ShapeDtypeStruct((T, D_MODEL), F32),
        compiler_params=_params(("parallel",)),
        name="glu_front",
    )(x, w, b)


CONV_BASE = CONV_HALO - (CONV_WIDTH - 1)
N_CAST_PER_CONV = DEPTH // N_A
W_PARTS = (T // TM_CONV) // N_EXPERTS
assert W_PARTS * N_EXPERTS == T // TM_CONV and N_CAST_PER_CONV * N_A == DEPTH


def _conv_kernel(g_ref, halo_ref, x_ref, wdw_ref, bdw_ref, lng_ref, lnb_ref, w2_ref, b2_ref,
                 mg_ref, mb_ref, *rest):
    n_w = 3 * N_CAST_PER_CONV
    w_refs, o_ref, wb_refs, (buf_ref, cv_ref) = (rest[:n_w], rest[n_w], rest[n_w + 1:2 * n_w + 1],
                                                 rest[2 * n_w + 1:])
    for w_ref, wb_ref in zip(w_refs, wb_refs):
        wb_ref[0] = w_ref[0, 0].astype(BF16)

    i = pl.program_id(0)
    first = (i % (SEQ // TM_CONV)) == 0
    buf_ref[0:CONV_HALO, :] = jnp.where(first, 0.0, halo_ref[...])
    buf_ref[CONV_HALO:CONV_HALO + TM_CONV, :] = g_ref[...]
    buf_ref[CONV_HALO + TM_CONV:, :] = jnp.zeros((SUBLANES, D_MODEL), F32)
    sub = lax.broadcasted_iota(jnp.int32, (SUBLANES, LANE), 0)
    n_q = (CONV_BASE + CONV_WIDTH - 1) // SUBLANES + 1
    for c in range(D_MODEL // LANE):
        cs = slice(c * LANE, (c + 1) * LANE)
        w_b = [jnp.broadcast_to(wdw_ref[j:j + 1, cs], (SUBLANES, LANE)) for j in range(CONV_WIDTH)]
        bias = jnp.broadcast_to(bdw_ref[:, cs], (SUBLANES, LANE))

        def partials(v):
            tiles = [buf_ref[pl.ds(v + SUBLANES * q, SUBLANES), cs] for q in range(n_q)]
            out = []
            for s in range(SUBLANES):
                a = None
                for q in range(n_q):
                    j = SUBLANES * q + s - CONV_BASE
                    if 0 <= j < CONV_WIDTH:
                        term = w_b[j] * tiles[q]
                        a = term if a is None else a + term
                out.append(a)
            return tuple(out)

        def row_body(t, prev):
            r = pl.multiple_of(t * SUBLANES, SUBLANES)
            cur = partials(pl.multiple_of(r + SUBLANES, SUBLANES))
            acc = bias + prev[0]
            for s in range(1, SUBLANES):
                mixed = jnp.where(sub >= s, prev[s], cur[s])
                acc = acc + pltpu.roll(mixed, SUBLANES - s, axis=0)
            cv_ref[pl.ds(r, SUBLANES), cs] = acc
            return cur

        lax.fori_loop(0, TM_CONV // SUBLANES, row_body, partials(0), unroll=2)
    h = _layer_norm(cv_ref[...], lng_ref[...], lnb_ref[...])
    h = h * _sigmoid(h)
    mix = jnp.dot(h.astype(BF16), w2_ref[...], preferred_element_type=F32) + b2_ref[...]
    o_ref[...] = _layer_norm(ALPHA * x_ref[...] + mix, mg_ref[...], mb_ref[...])


def _conv_back(g, x, wdw, bdw, lng, lnb, w2, b2, mg, mb, wg, wu, wd, cast_layers):
    assert len(cast_layers) == N_CAST_PER_CONV
    row = lambda i: (i, 0)
    fixed = lambda i: (0, 0)
    vec = pl.BlockSpec((1, D_MODEL), fixed)
    halo_blocks = TM_CONV // CONV_HALO
    up_rows, down_rows = D_MODEL // W_PARTS, D_EXPERT // W_PARTS
    w_out = lambda i: (i // W_PARTS, i % W_PARTS, 0)
    w_specs, wb_specs, wb_shapes, w_ops = [], [], [], []
    for layer in cast_layers:
        w_in = lambda i, layer=layer: (layer, i // W_PARTS, i % W_PARTS, 0)
        for w, rows, cols in ((wg, up_rows, D_EXPERT), (wu, up_rows, D_EXPERT),
                              (wd, down_rows, D_MODEL)):
            w_specs.append(pl.BlockSpec((1, 1, rows, cols), w_in))
            wb_specs.append(pl.BlockSpec((1, rows, cols), w_out))
            wb_shapes.append(jax.ShapeDtypeStruct((N_EXPERTS, rows * W_PARTS, cols), BF16))
            w_ops.append(w)
    outs = pl.pallas_call(
        _conv_kernel,
        grid=(T // TM_CONV,),
        in_specs=[
            pl.BlockSpec((TM_CONV, D_MODEL), row),
            pl.BlockSpec((CONV_HALO, D_MODEL), lambda i: (jnp.maximum(i * halo_blocks - 1, 0), 0)),
            pl.BlockSpec((TM_CONV, D_MODEL), row),
            pl.BlockSpec((CONV_HALO, D_MODEL), fixed),
            vec, vec, vec,
            pl.BlockSpec((D_MODEL, D_MODEL), fixed),
            vec, vec, vec,
        ] + w_specs,
        out_specs=[pl.BlockSpec((TM_CONV, D_MODEL), row)] + wb_specs,
        out_shape=[jax.ShapeDtypeStruct((T, D_MODEL), F32)] + wb_shapes,
        scratch_shapes=[
            pltpu.VMEM((TM_CONV + CONV_HALO + SUBLANES, D_MODEL), F32),
            pltpu.VMEM((TM_CONV, D_MODEL), F32),
        ],
        compiler_params=_params(("parallel",)),
        name="conv_back",
    )(g, g, x, wdw, bdw, lng, lnb, w2, b2, mg, mb, *w_ops)
    cast = {layer: tuple(outs[1 + 3 * n:4 + 3 * n]) for n, layer in enumerate(cast_layers)}
    return outs[0], cast


def _ranks_before(vals):
    n = len(vals)
    ranks = []
    for j in range(n):
        r = jnp.zeros_like(vals[j])
        for i in range(n):
            if i == j:
                continue
            before = (vals[i] > vals[j]) | ((vals[i] == vals[j]) & (i < j))
            r = r + before.astype(F32)
        ranks.append(r)
    return ranks


PAIRS = [(a, b) for a in range(EXPERTS_PER_GROUP) for b in range(a + 1, EXPERTS_PER_GROUP)]
N_CLASSES = N_GROUPS * len(PAIRS)
CLASS_ROWS = 32
CLASS_EXPERTS = [(g * EXPERTS_PER_GROUP + a, g * EXPERTS_PER_GROUP + b)
                 for g in range(N_GROUPS) for (a, b) in PAIRS]
TM_G = 256
NT_MAX = -(-(T + N_CLASSES * (TM_G - 1)) // TM_G)
P_MAX = NT_MAX * TM_G
XG_W = D_MODEL + LANE
GATE_ROWS = 8
TD = 256
NT_MIN = T // TM_G
N_ZERO_TILES = N_CLASSES + NT_MAX - NT_MIN


def _route_kernel(x_ref, wrt_ref, rb_ref, gw_ref, cls_ref, pos_ref, tot_ref, tri_ref, carry_ref):
    i = pl.program_id(0)

    @pl.when(i == 0)
    def _():
        r = lax.broadcasted_iota(jnp.int32, (TM_ROUTER, TM_ROUTER), 0)
        c = lax.broadcasted_iota(jnp.int32, (TM_ROUTER, TM_ROUTER), 1)
        tri_ref[...] = jnp.where(r < c, 1.0, 0.0).astype(BF16)
        carry_ref[...] = jnp.zeros_like(carry_ref)

    x = x_ref[...]
    x_hi = x.astype(BF16)
    x_mid = (x - x_hi.astype(F32)).astype(BF16)
    both = lax.dot_general(wrt_ref[...], x_hi, _NT, preferred_element_type=F32)
    logits = (both[:N_EXPERTS] + both[N_EXPERTS:]
              + lax.dot_general(wrt_ref[:N_EXPERTS, :], x_mid, _NT, preferred_element_type=F32))
    aff = _sigmoid(logits)
    sel = aff + rb_ref[...]
    aff_rows = [aff[e:e + 1, :] for e in range(N_EXPERTS)]
    sel_rows = [sel[e:e + 1, :] for e in range(N_EXPERTS)]
    in_top2 = []
    scores = []
    for g in range(N_GROUPS):
        members = sel_rows[g * EXPERTS_PER_GROUP:(g + 1) * EXPERTS_PER_GROUP]
        ranks = _ranks_before(members)
        top = [r < 2.0 for r in ranks]
        in_top2.extend(top)
        s = jnp.zeros_like(members[0])
        for v, t in zip(members, top):
            s = s + jnp.where(t, v, 0.0)
        scores.append(s)
    g_ranks = _ranks_before(scores)
    w_rows = []
    for e in range(N_EXPERTS):
        chosen = (g_ranks[e // EXPERTS_PER_GROUP] < 1.0) & in_top2[e]
        w_rows.append(jnp.where(chosen, aff_rows[e], 0.0))
    denom = w_rows[0]
    for e in range(1, N_EXPERTS):
        denom = denom + w_rows[e]
    inv = 1.0 / denom

    masks = []
    wa = jnp.zeros_like(denom)
    wb = jnp.zeros_like(denom)
    for c, (ea, eb) in enumerate(CLASS_EXPERTS):
        m = (g_ranks[c // len(PAIRS)] < 1.0) & in_top2[ea] & in_top2[eb]
        masks.append(m.astype(F32))
        wa = wa + jnp.where(m, aff_rows[ea], 0.0)
        wb = wb + jnp.where(m, aff_rows[eb], 0.0)
    zero_row = jnp.zeros_like(denom)
    onehot = jnp.concatenate(masks + [zero_row] * (CLASS_ROWS - N_CLASSES), axis=0)

    before = jnp.dot(onehot.astype(BF16), tri_ref[...], preferred_element_type=F32)
    carry = carry_ref[...]
    class_id = lax.broadcasted_iota(jnp.int32, (CLASS_ROWS, 1), 0).astype(F32)
    pos_ref[...] = jnp.sum(onehot * (before + carry), axis=0, keepdims=True).astype(jnp.int32)
    cls_ref[...] = jnp.sum(onehot * class_id, axis=0, keepdims=True).astype(jnp.int32)
    carry = carry + jnp.sum(onehot, axis=1, keepdims=True)
    carry_ref[...] = carry
    tot_ref[...] = carry

    gw_ref[...] = jnp.concatenate([wa * inv, wb * inv] + [zero_row] * (GATE_ROWS - 2), axis=0)


def _route(x, wrt, rb):
    return pl.pallas_call(
        _route_kernel,
        grid=(T // TM_ROUTER,),
        in_specs=[
            pl.BlockSpec((TM_ROUTER, D_MODEL), lambda i: (i, 0)),
            pl.BlockSpec((2 * N_EXPERTS, D_MODEL), lambda i: (0, 0)),
            pl.BlockSpec((N_EXPERTS, 1), lambda i: (0, 0)),
        ],
        out_specs=[
            pl.BlockSpec((GATE_ROWS, TM_ROUTER), lambda i: (0, i)),
            pl.BlockSpec((1, TM_ROUTER), lambda i: (0, i)),
            pl.BlockSpec((1, TM_ROUTER), lambda i: (0, i)),
            pl.BlockSpec((CLASS_ROWS, 1), lambda i: (0, 0)),
        ],
        out_shape=[
            jax.ShapeDtypeStruct((GATE_ROWS, T), F32),
            jax.ShapeDtypeStruct((1, T), jnp.int32),
            jax.ShapeDtypeStruct((1, T), jnp.int32),
            jax.ShapeDtypeStruct((CLASS_ROWS, 1), F32),
        ],
        scratch_shapes=[
            pltpu.VMEM((TM_ROUTER, TM_ROUTER), BF16),
            pltpu.VMEM((CLASS_ROWS, 1), F32),
        ],
        compiler_params=_params(("arbitrary",)),
        name="route",
    )(x, wrt, rb)


def _start_all(copies):
    for cp in copies:
        cp.start()


def _row_copies(pay, slot, dst_hbm, dest_ref, base, sem):
    return [pltpu.make_async_copy(pay.at[slot, r], dst_hbm.at[dest_ref[base + r]], sem.at[slot])
            for r in range(TD)]


def _dispatch_kernel(dest_ref, zstart_ref, x_ref, gw_ref, xs_hbm, pay, zbuf, zsem, sem):
    i = pl.program_id(0)
    last = pl.num_programs(0) - 1

    @pl.when(i == 0)
    def _():
        zbuf[...] = jnp.zeros_like(zbuf)

        def zero_tile(c):
            start = pl.multiple_of(zstart_ref[c], TM_G)
            return pltpu.make_async_copy(zbuf, xs_hbm.at[pl.ds(start, TM_G)], zsem)

        for c in range(N_ZERO_TILES):
            @pl.when(zstart_ref[c] >= 0)
            def _():
                zero_tile(c).start()
        for c in range(N_ZERO_TILES):
            @pl.when(zstart_ref[c] >= 0)
            def _():
                zero_tile(c).wait()

    for parity in range(2):
        @pl.when(i % 2 == parity)
        def _():
            pay[parity, :, :D_MODEL] = x_ref[...]
            gates = jnp.concatenate(
                [gw_ref[...], jnp.zeros((LANE - GATE_ROWS, TD), F32)], axis=0)
            pay[parity, :, D_MODEL:] = gates.T
            _start_all(_row_copies(pay, parity, xs_hbm, dest_ref, i * TD, sem))

            @pl.when(i > 0)
            def _():
                for cp in _row_copies(pay, 1 - parity, xs_hbm, dest_ref, (i - 1) * TD, sem):
                    cp.wait()

            @pl.when(i == last)
            def _():
                for cp in _row_copies(pay, parity, xs_hbm, dest_ref, i * TD, sem):
                    cp.wait()


def _dispatch(dest, zstart, x, gw):
    return pl.pallas_call(
        _dispatch_kernel,
        grid_spec=pltpu.PrefetchScalarGridSpec(
            num_scalar_prefetch=2,
            grid=(T // TD,),
            in_specs=[
                pl.BlockSpec((TD, D_MODEL), lambda i, d, z: (i, 0)),
                pl.BlockSpec((GATE_ROWS, TD), lambda i, d, z: (0, i)),
            ],
            out_specs=pl.BlockSpec(memory_space=pl.ANY),
            scratch_shapes=[
                pltpu.VMEM((2, TD, XG_W), F32),
                pltpu.VMEM((TM_G, XG_W), F32),
                pltpu.SemaphoreType.DMA,
                pltpu.SemaphoreType.DMA((2,)),
            ],
        ),
        out_shape=jax.ShapeDtypeStruct((P_MAX, XG_W), F32),
        compiler_params=_params(("arbitrary",)),
        name="dispatch",
    )(dest, zstart, x, gw)


def _experts_kernel(te_ref, nt_ref, xs_ref, wga_ref, wua_ref, wda_ref, wgb_ref, wub_ref, wdb_ref,
                    ys_ref):
    i = pl.program_id(0)

    @pl.when(i < nt_ref[0])
    def _():
        xb = xs_ref[:, :D_MODEL].astype(BF16)
        gates = xs_ref[:, D_MODEL:]
        y = None
        for which, (wg_ref, wu_ref, wd_ref) in enumerate(((wga_ref, wua_ref, wda_ref),
                                                          (wgb_ref, wub_ref, wdb_ref))):
            hg = jnp.dot(xb, wg_ref[0], preferred_element_type=F32)
            hu = jnp.dot(xb, wu_ref[0], preferred_element_type=F32)
            h = hg * _sigmoid(hg) * hu * gates[:, which:which + 1]
            part = jnp.dot(h.astype(BF16), wd_ref[0], preferred_element_type=F32)
            y = part if y is None else y + part
        ys_ref[...] = y

    @pl.when(i >= nt_ref[0])
    def _():
        ys_ref[...] = jnp.zeros_like(ys_ref)


def _experts(tile_e, n_tiles, xs, wg, wu, wd):
    def row_map(i, te, nt):
        return (jnp.minimum(i, nt[0] - 1), 0)

    def w_map(which):
        return lambda i, te, nt: (te[which * NT_MAX + i], 0, 0)

    up_spec = lambda which: pl.BlockSpec((1, D_MODEL, D_EXPERT), w_map(which))
    down_spec = lambda which: pl.BlockSpec((1, D_EXPERT, D_MODEL), w_map(which))
    return pl.pallas_call(
        _experts_kernel,
        grid_spec=pltpu.PrefetchScalarGridSpec(
            num_scalar_prefetch=2,
            grid=(NT_MAX,),
            in_specs=[
                pl.BlockSpec((TM_G, XG_W), row_map),
                up_spec(0), up_spec(0), down_spec(0),
                up_spec(1), up_spec(1), down_spec(1),
            ],
            out_specs=pl.BlockSpec((TM_G, D_MODEL), lambda i, te, nt: (i, 0)),
        ),
        out_shape=jax.ShapeDtypeStruct((P_MAX, D_MODEL), F32),
        compiler_params=_params(("arbitrary",)),
        name="experts",
    )(tile_e, n_tiles, xs, wg, wu, wd, wg, wu, wd)


def _gather_copies(ys_hbm, buf, dest_ref, base, slot, sem):
    return [pltpu.make_async_copy(ys_hbm.at[dest_ref[base + r]], buf.at[slot, r], sem.at[slot])
            for r in range(TD)]


def _combine_kernel(dest_ref, x_ref, ys_hbm, lg_ref, lb_ref, *rest, q_scale):
    if q_scale is None:
        o_ref, buf, sem = rest
    else:
        wqt_ref, o_ref, qt_ref, buf, sem = rest
    i = pl.program_id(0)
    last = pl.num_programs(0) - 1

    @pl.when(i == 0)
    def _():
        _start_all(_gather_copies(ys_hbm, buf, dest_ref, 0, 0, sem))

    for parity in range(2):
        @pl.when(i % 2 == parity)
        def _():
            @pl.when(i < last)
            def _():
                _start_all(_gather_copies(ys_hbm, buf, dest_ref, (i + 1) * TD, 1 - parity, sem))

            for cp in _gather_copies(ys_hbm, buf, dest_ref, i * TD, parity, sem):
                cp.wait()
            xn = _layer_norm(ALPHA * x_ref[...] + buf[parity], lg_ref[...], lb_ref[...])
            o_ref[...] = xn
            if q_scale is not None:
                q = lax.dot_general(wqt_ref[...], xn.astype(BF16), _NT, preferred_element_type=F32)
                qt_ref[...] = (q * q_scale).astype(BF16)


def _combine(dest, x, ys, lg, lb, wq_t=None, q_scale=None):
    vec = pl.BlockSpec((1, D_MODEL), lambda i, d: (0, 0))
    rows = pl.BlockSpec((TD, D_MODEL), lambda i, d: (i, 0))
    in_specs = [rows, pl.BlockSpec(memory_space=pl.ANY), vec, vec]
    out_specs = [rows]
    out_shape = [jax.ShapeDtypeStruct((T, D_MODEL), F32)]
    operands = [dest, x, ys, lg, lb]
    if wq_t is not None:
        in_specs.append(pl.BlockSpec((QK_WIDTH, D_MODEL), lambda i, d: (0, 0)))
        out_specs.append(pl.BlockSpec((QK_WIDTH, TD), lambda i, d: (0, i)))
        out_shape.append(jax.ShapeDtypeStruct((QK_WIDTH, T), BF16))
        operands.append(wq_t)
    outs = pl.pallas_call(
        functools.partial(_combine_kernel, q_scale=q_scale if wq_t is not None else None),
        grid_spec=pltpu.PrefetchScalarGridSpec(
            num_scalar_prefetch=1,
            grid=(T // TD,),
            in_specs=in_specs,
            out_specs=out_specs,
            scratch_shapes=[
                pltpu.VMEM((2, TD, D_MODEL), F32),
                pltpu.SemaphoreType.DMA((2,)),
            ],
        ),
        out_shape=out_shape,
        compiler_params=_params(("arbitrary",)),
        name="combine",
    )(*operands)
    return outs[0] if wq_t is None else tuple(outs)


TM_G_LOG2 = TM_G.bit_length() - 1
assert 1 << TM_G_LOG2 == TM_G


def _plan_kernel(cnt_ref, cls_ref, pos_ref, dest_ref, zstart_ref, te_ref, nt_ref):
    shr = lax.shift_right_logical
    run = jnp.int32(0)
    starts, ends = [], []
    for c in range(N_CLASSES):
        padded = shr(cnt_ref[c] + (TM_G - 1), TM_G_LOG2) * TM_G
        starts.append(run)
        run = run + padded
        ends.append(run)
        zstart_ref[c] = jnp.where(padded > 0, run - TM_G, -1)
    n_tiles = shr(run, TM_G_LOG2)
    nt_ref[0] = n_tiles
    for t in range(NT_MIN, NT_MAX):
        zstart_ref[N_CLASSES + t - NT_MIN] = jnp.where(t >= n_tiles, t * TM_G, -1)

    def tile_body(t, carry):
        row0 = jnp.minimum(t, n_tiles - 1) * TM_G
        ea = jnp.int32(0)
        eb = jnp.int32(0)
        for c in range(N_CLASSES):
            inside = (row0 >= starts[c]) & (row0 < ends[c])
            ea = jnp.where(inside, CLASS_EXPERTS[c][0], ea)
            eb = jnp.where(inside, CLASS_EXPERTS[c][1], eb)
        te_ref[t] = ea
        te_ref[NT_MAX + t] = eb
        return carry

    lax.fori_loop(0, NT_MAX, tile_body, 0)

    cls = cls_ref[...]
    dest = pos_ref[...]
    for c in range(N_CLASSES):
        dest = dest + jnp.where(cls == c, starts[c], 0)
    dest_ref[...] = dest


def _plan(counts, cls, pos):
    smem = pl.BlockSpec(memory_space=pltpu.SMEM)
    vmem = pl.BlockSpec(memory_space=pltpu.VMEM)
    return pl.pallas_call(
        _plan_kernel,
        in_specs=[smem, vmem, vmem],
        out_specs=[vmem, smem, smem, smem],
        out_shape=[
            jax.ShapeDtypeStruct((1, T), jnp.int32),
            jax.ShapeDtypeStruct((N_ZERO_TILES,), jnp.int32),
            jax.ShapeDtypeStruct((2 * NT_MAX,), jnp.int32),
            jax.ShapeDtypeStruct((1,), jnp.int32),
        ],
        name="plan",
    )(counts, cls, pos)


def _moe(x, wrt, rb, wg_b, wu_b, wd_b, lg, lb, wq_t=None, q_scale=None):
    gw, cls, pos, tot = _route(x, wrt, rb)
    dest, zstart, tile_e, n_tiles = _plan(tot[:, 0].astype(jnp.int32), cls, pos)
    dest = dest.reshape(T)
    xs = _dispatch(dest, zstart, x, gw)
    ys = _experts(tile_e, n_tiles, xs, wg_b, wu_b, wd_b)
    return _combine(dest, x, ys, lg, lb, wq_t, q_scale)


def _kv_proj_kernel(x_ref, wk_ref, wvt_ref, k_ref, vt_ref):
    xb = x_ref[...].astype(BF16)
    k_ref[...] = jnp.dot(xb, wk_ref[...], preferred_element_type=F32).astype(BF16)
    vt_ref[0] = lax.dot_general(wvt_ref[...], xb, _NT, preferred_element_type=F32).astype(BF16)


def _kv_proj(x, wk, wvt):
    return pl.pallas_call(
        _kv_proj_kernel,
        grid=(T // TK,),
        in_specs=[
            pl.BlockSpec((TK, D_MODEL), lambda i: (i, 0)),
            pl.BlockSpec((D_MODEL, QK_WIDTH), lambda i: (0, 0)),
            pl.BlockSpec((V_WIDTH, D_MODEL), lambda i: (0, 0)),
        ],
        out_specs=[
            pl.BlockSpec((TK, QK_WIDTH), lambda i: (i, 0)),
            pl.BlockSpec((1, V_WIDTH, TK), lambda i: (i, 0, 0)),
        ],
        out_shape=[
            jax.ShapeDtypeStruct((T, QK_WIDTH), BF16),
            jax.ShapeDtypeStruct((T // TK, V_WIDTH, TK), BF16),
        ],
        compiler_params=_params(("parallel",)),
        name="kv_proj",
    )(x, wk, wvt)


def _proj_ln_kernel(a_ref, x_ref, w_ref, g_ref, b_ref, o_ref):
    mix = jnp.dot(a_ref[...], w_ref[...], preferred_element_type=F32)
    o_ref[...] = _layer_norm(ALPHA * x_ref[...] + mix, g_ref[...], b_ref[...])


def _proj_ln(a, x, w, g, b):
    vec = pl.BlockSpec((1, D_MODEL), lambda i: (0, 0))
    return pl.pallas_call(
        _proj_ln_kernel,
        grid=(T // TM_PROJ,),
        in_specs=[
            pl.BlockSpec((TM_PROJ, V_WIDTH), lambda i: (i, 0)),
            pl.BlockSpec((TM_PROJ, D_MODEL), lambda i: (i, 0)),
            pl.BlockSpec((V_WIDTH, D_MODEL), lambda i: (0, 0)),
            vec, vec,
        ],
        out_specs=pl.BlockSpec((TM_PROJ, D_MODEL), lambda i: (i, 0)),
        out_shape=jax.ShapeDtypeStruct((T, D_MODEL), F32),
        compiler_params=_params(("parallel",)),
        name="proj_ln",
    )(a, x, w, g, b)


assert MAX_DISTANCE <= LANE
def _bias_kernel(rb_ref, diag_ref, corner_ref):
    h = pl.program_id(0)
    max_exact = N_BUCKETS // 2
    far = rb_ref[N_BUCKETS - 1, h]

    def tile(shape, offset):
        c = lax.broadcasted_iota(jnp.int32, shape, 0)
        r = lax.broadcasted_iota(jnp.int32, shape, 1)
        rel = offset + r - c
        n = jnp.maximum(rel, 0)
        nf = jnp.maximum(n, 1).astype(F32)
        large = max_exact + (jnp.log(nf / max_exact) / math.log(MAX_DISTANCE / max_exact)
                             * (N_BUCKETS - max_exact)).astype(jnp.int32)
        large = jnp.minimum(large, N_BUCKETS - 1)
        bucket = jnp.where(n < max_exact, n, large)
        bias = jnp.zeros(shape, F32)
        for b in range(N_BUCKETS):
            bias = jnp.where(bucket == b, rb_ref[b, h] - far, bias)
        return jnp.where(rel >= 0, bias * LOG2E, NEG_BIG)

    diag_ref[0] = tile((TK, TQ), 0)
    corner_ref[0] = tile((LANE, LANE), LANE)


def _bias_tiles(rel_bias):
    return pl.pallas_call(
        _bias_kernel,
        grid=(N_HEADS,),
        in_specs=[pl.BlockSpec(memory_space=pltpu.SMEM)],
        out_specs=[
            pl.BlockSpec((1, TK, TQ), lambda h: (h, 0, 0)),
            pl.BlockSpec((1, LANE, LANE), lambda h: (h, 0, 0)),
        ],
        out_shape=[
            jax.ShapeDtypeStruct((N_HEADS, TK, TQ), F32),
            jax.ShapeDtypeStruct((N_HEADS, LANE, LANE), F32),
        ],
        compiler_params=_params(("parallel",)),
        name="bias_tiles",
    )(rel_bias)


def _attn_kernel(q1t_ref, q2t_ref, q1n_ref, q2n_ref, k1_ref, k2_ref, vt_ref, diag_ref, corner_ref,
                 lam_ref, sg_ref, o_ref, s_ref, mx_ref, m_ref, l_ref, acc_ref, *, lambda_init):
    qi = pl.program_id(2)
    dim = lax.broadcasted_iota(jnp.int32, (LANE, 1), 0)
    lo = dim < HEAD_DIM

    def head_masked(q1, q2):
        zero = jnp.zeros_like(q1)
        return [jnp.where(lo, q1, zero), jnp.where(lo, q2, zero),
                jnp.where(lo, zero, q1), jnp.where(lo, zero, q2)]

    qs = head_masked(q1t_ref[...], q2t_ref[...])
    qs_next = head_masked(q1n_ref[...], q2n_ref[...])

    def reset_stats():
        m_ref[...] = jnp.full(m_ref.shape, NEG_BIG, F32)
        l_ref[...] = jnp.zeros(l_ref.shape, F32)
        acc_ref[...] = jnp.zeros(acc_ref.shape, F32)

    def scores(j, slot, q_slots=qs):
        k0 = pl.multiple_of(j * TK, TK)
        k_ref = k1_ref if slot % 2 == 0 else k2_ref
        st = jnp.dot(k_ref[pl.ds(k0, TK), :], q_slots[slot], preferred_element_type=F32)
        s_ref[slot] = st
        mx_ref[slot] = jnp.max(st, axis=0, keepdims=True)

    def update(j, slot, kind):
        head = slot // 2
        if kind == 1:
            rows = slice(TK - LANE, TK)
            corner = s_ref[slot, rows, 0:LANE] + corner_ref[head]
            s_ref[slot, rows, 0:LANE] = corner
            mx_ref[slot, :, 0:LANE] = jnp.maximum(mx_ref[slot, :, 0:LANE],
                                                  jnp.max(corner, axis=0, keepdims=True))
        st = s_ref[slot]
        if kind == 0:
            st = st + diag_ref[head]
            mx = jnp.max(st, axis=0, keepdims=True)
        else:
            mx = mx_ref[slot]
        m_prev = m_ref[slot]
        m_new = jnp.maximum(m_prev, mx)
        alpha = jnp.exp2(m_prev - m_new)
        p = jnp.exp2(st - m_new)
        l_ref[slot] = alpha * l_ref[slot] + jnp.sum(p, axis=0, keepdims=True)
        vt = vt_ref[j, head * V_DIM:(head + 1) * V_DIM, :]
        acc_ref[slot] = alpha * acc_ref[slot] + jnp.dot(vt, p.astype(BF16),
                                                        preferred_element_type=F32)
        m_ref[slot] = m_new

    def tile(j, kind, nxt):
        for slot in range(4):
            ahead = slot + 2
            if ahead < 4:
                scores(j, ahead)
            elif nxt is None:
                scores(0, ahead - 4, qs_next)
            else:
                scores(nxt, ahead - 4)
            update(j, slot, kind)

    @pl.when(qi == 0)
    def _():
        reset_stats()
        scores(0, 0)
        scores(0, 1)

    n_far = jnp.maximum(qi - 1, 0)
    n_groups = n_far // FAR_UNROLL

    def far_group(t, carry):
        for u in range(FAR_UNROLL):
            tile(FAR_UNROLL * t + u, None, FAR_UNROLL * t + u + 1)
        return carry

    lax.fori_loop(0, n_groups, far_group, 0)

    def far_single(j, carry):
        tile(j, None, j + 1)
        return carry

    lax.fori_loop(n_groups * FAR_UNROLL, n_far, far_single, 0)

    @pl.when(qi >= 1)
    def _():
        tile(qi - 1, 1, qi)

    tile(qi, 0, None)

    lp = lam_ref[...]
    lam = (jnp.exp(jnp.sum(lp[0:1] * lp[1:2], axis=1, keepdims=True))
           - jnp.exp(jnp.sum(lp[2:3] * lp[3:4], axis=1, keepdims=True)) + lambda_init)
    for head in range(2):
        a1 = acc_ref[2 * head] / l_ref[2 * head]
        a2 = acc_ref[2 * head + 1] / l_ref[2 * head + 1]
        of = (a1 - lam * a2).T
        of = of * lax.rsqrt(jnp.mean(of * of, axis=-1, keepdims=True) + LN_EPS) * sg_ref[...]
        of = of * (1.0 - lambda_init)
        o_ref[:, head * V_DIM:(head + 1) * V_DIM] = of.astype(o_ref.dtype)
    reset_stats()


def _attention(qt, k, vt, bias_diag, bias_corner, lam_params, subln_g, lambda_init):
    nq = SEQ // TQ
    nk = SEQ // TK
    qk_blocks = QK_WIDTH // 2 // LANE
    return pl.pallas_call(
        functools.partial(_attn_kernel, lambda_init=lambda_init),
        grid=(BATCH, N_PAIRS, nq),
        in_specs=[
            pl.BlockSpec((LANE, TQ), lambda b, p, i: (p, b * nq + i)),
            pl.BlockSpec((LANE, TQ), lambda b, p, i: (qk_blocks + p, b * nq + i)),
            pl.BlockSpec((LANE, TQ), lambda b, p, i: (p, b * nq + jnp.minimum(i + 1, nq - 1))),
            pl.BlockSpec((LANE, TQ),
                         lambda b, p, i: (qk_blocks + p, b * nq + jnp.minimum(i + 1, nq - 1))),
            pl.BlockSpec((SEQ, LANE), lambda b, p, i: (b, p)),
            pl.BlockSpec((SEQ, LANE), lambda b, p, i: (b, qk_blocks + p)),
            pl.BlockSpec((nk, 2 * V_DIM, TK), lambda b, p, i: (b, p, 0)),
            pl.BlockSpec((2, TK, TQ), lambda b, p, i: (p, 0, 0)),
            pl.BlockSpec((2, LANE, LANE), lambda b, p, i: (p, 0, 0)),
            pl.BlockSpec((4, HEAD_DIM), lambda b, p, i: (0, 0)),
            pl.BlockSpec((1, V_DIM), lambda b, p, i: (0, 0)),
        ],
        out_specs=pl.BlockSpec((TQ, 2 * V_DIM), lambda b, p, i: (b * nq + i, p)),
        out_shape=jax.ShapeDtypeStruct((T, V_WIDTH), BF16),
        scratch_shapes=[
            pltpu.VMEM((4, TK, TQ), F32),
            pltpu.VMEM((4, 1, TQ), F32),
            pltpu.VMEM((4, 1, TQ), F32),
            pltpu.VMEM((4, 1, TQ), F32),
            pltpu.VMEM((4, V_DIM, TQ), F32),
        ],
        compiler_params=_params(("parallel", "parallel", "arbitrary"), vmem_mb=56),
        name="diff_attention",
    )(qt, qt, qt, qt, k, k, vt, bias_diag, bias_corner, lam_params, subln_g)


def kernel(x, a_w_pw1, a_b_pw1, a_w_dw, a_b_dw, a_ln_g, a_ln_b, a_w_pw2, a_b_pw2, w_kv, b_w_q,
           b_lambda, b_subln_g, b_w_o, rel_bias, ln_mix_g, ln_mix_b, ln_ffn_g, ln_ffn_b,
           router_w, router_bias, moe_w_gate, moe_w_up, moe_w_down):
    x = x.reshape(T, D_MODEL)
    row = lambda v: v.reshape(1, -1)
    wr_hi = router_w.T.astype(BF16)
    wr_mid = (router_w.T - wr_hi.astype(F32)).astype(BF16)
    wrt = jnp.concatenate([wr_hi, wr_mid], axis=0)
    rb = router_bias.reshape(N_EXPERTS, 1)
    k_all = vt_all = bias_diag = bias_corner = qt = None
    experts_bf16 = {}
    for l in range(DEPTH):
        if l < N_A:
            g = _glu(x, a_w_pw1[l].astype(BF16), row(a_b_pw1[l]))
            wdw = jnp.pad(a_w_dw[l], ((0, CONV_HALO - CONV_WIDTH), (0, 0)))
            x, cast = _conv_back(g, x, wdw, row(a_b_dw[l]), row(a_ln_g[l]), row(a_ln_b[l]),
                                 a_w_pw2[l].astype(BF16), row(a_b_pw2[l]),
                                 row(ln_mix_g[l]), row(ln_mix_b[l]),
                                 moe_w_gate, moe_w_up, moe_w_down, (l, l + N_A))
            experts_bf16.update(cast)
        else:
            if l == N_A:
                k_all, vt_all = _kv_proj(x, w_kv[:, :QK_WIDTH].astype(BF16),
                                         w_kv[:, QK_WIDTH:].T.astype(BF16))
                bias_diag, bias_corner = _bias_tiles(rel_bias)
            j = l - N_A
            lambda_init = 0.8 - 0.6 * math.exp(-0.3 * l)
            o = _attention(qt, k_all, vt_all, bias_diag, bias_corner, b_lambda[j],
                           row(b_subln_g[j]), lambda_init)
            x = _proj_ln(o, x, b_w_o[j].astype(BF16), row(ln_mix_g[l]), row(ln_mix_b[l]))
        moe_args = (x, wrt, rb, *experts_bf16[l], row(ln_ffn_g[l]), row(ln_ffn_b[l]))
        if N_A <= l + 1 < DEPTH:
            x, qt = _moe(*moe_args, wq_t=b_w_q[l + 1 - N_A].T.astype(BF16),
                         q_scale=HEAD_DIM ** -0.5 * LOG2E)
        else:
            x = _moe(*moe_args)
    return x.reshape(BATCH, SEQ, D_MODEL)
```

```python
import functools
import math

import jax
import jax.numpy as jnp
from jax import lax
from jax.experimental import pallas as pl
from jax.experimental.pallas import tpu as pltpu

D_MODEL = 1024
BATCH = 2
SEQ = 8192
DEPTH = 4
N_A = DEPTH // 2
CONV_WIDTH = 31
N_HEADS = 8
HEAD_DIM = 64
V_DIM = 2 * HEAD_DIM
QK_WIDTH = 2 * N_HEADS * HEAD_DIM
V_WIDTH = N_HEADS * V_DIM
N_BUCKETS = 32
MAX_DISTANCE = 128
N_EXPERTS = 16
N_GROUPS = 4
EXPERTS_PER_GROUP = N_EXPERTS // N_GROUPS
D_EXPERT = 512
ALPHA = (2.0 * DEPTH) ** 0.25
LN_EPS = 1e-5

T = BATCH * SEQ
F32 = jnp.float32
BF16 = jnp.bfloat16
LOG2E = 1.4426950408889634
NEG_BIG = -1e30
LANE = 128
SUBLANES = 8
CONV_HALO = 32
N_PAIRS = N_HEADS // 2

TM_GLU = 512
TM_CONV = 256
TM_ROUTER = 1024
TM_PROJ = 512
TQ = 512
TK = 512
FAR_UNROLL = 4
PROJ_CHUNKS = 4


def _params(sem, vmem_mb=None):
    kw = dict(dimension_semantics=sem)
    if vmem_mb is not None:
        kw["vmem_limit_bytes"] = vmem_mb * 1024 * 1024
    return pltpu.CompilerParams(**kw)


def _layer_norm(v, g, b):
    mu = jnp.mean(v, axis=-1, keepdims=True)
    d = v - mu
    var = jnp.mean(d * d, axis=-1, keepdims=True)
    return d * lax.rsqrt(var + LN_EPS) * g + b


def _sigmoid(v):
    return 1.0 / (1.0 + jnp.exp(-v))


_NT = (((1,), (1,)), ((), ()))


def _glu_kernel(x_ref, w_ref, b_ref, o_ref):
    rows = TM_GLU // PROJ_CHUNKS
    parts = []
    for c in range(PROJ_CHUNKS):
        xb = x_ref[c * rows:(c + 1) * rows, :].astype(BF16)
        a = jnp.dot(xb, w_ref[:, :D_MODEL], preferred_element_type=F32) + b_ref[:, :D_MODEL]
        gate = jnp.dot(xb, w_ref[:, D_MODEL:], preferred_element_type=F32) + b_ref[:, D_MODEL:]
        parts.append((a, gate))
    for c, (a, gate) in enumerate(parts):
        o_ref[c * rows:(c + 1) * rows, :] = a * _sigmoid(gate)


def _glu(x, w, b):
    return pl.pallas_call(
        _glu_kernel,
        grid=(T // TM_GLU,),
        in_specs=[
            pl.BlockSpec((TM_GLU, D_MODEL), lambda i: (i, 0)),
            pl.BlockSpec((D_MODEL, 2 * D_MODEL), lambda i: (0, 0)),
            pl.BlockSpec((1, 2 * D_MODEL), lambda i: (0, 0)),
        ],
        out_specs=pl.BlockSpec((TM_GLU, D_MODEL), lambda i: (i, 0)),
        out_shape=jax.ShapeDtypeStruct((T, D_MODEL), F32),
        compiler_params=_params(("parallel",)),
        name="glu_front",
    )(x, w, b)


CONV_BASE = CONV_HALO - (CONV_WIDTH - 1)
N_CAST_PER_CONV = DEPTH // N_A
W_PARTS = (T // TM_CONV) // N_EXPERTS
assert W_PARTS * N_EXPERTS == T // TM_CONV and N_CAST_PER_CONV * N_A == DEPTH


def _conv_kernel(g_ref, halo_ref, x_ref, wdw_ref, bdw_ref, lng_ref, lnb_ref, w2_ref, b2_ref,
                 mg_ref, mb_ref, *rest):
    n_w = 3 * N_CAST_PER_CONV
    w_refs, o_ref, wb_refs, (buf_ref, cv_ref) = (rest[:n_w], rest[n_w], rest[n_w + 1:2 * n_w + 1],
                                                 rest[2 * n_w + 1:])
    for w_ref, wb_ref in zip(w_refs, wb_refs):
        wb_ref[0] = w_ref[0, 0].astype(BF16)

    i = pl.program_id(0)
    first = (i % (SEQ // TM_CONV)) == 0
    buf_ref[0:CONV_HALO, :] = jnp.where(first, 0.0, halo_ref[...])
    buf_ref[CONV_HALO:CONV_HALO + TM_CONV, :] = g_ref[...]
    buf_ref[CONV_HALO + TM_CONV:, :] = jnp.zeros((SUBLANES, D_MODEL), F32)
    sub = lax.broadcasted_iota(jnp.int32, (SUBLANES, LANE), 0)
    n_q = (CONV_BASE + CONV_WIDTH - 1) // SUBLANES + 1
    for c in range(D_MODEL // LANE):
        cs = slice(c * LANE, (c + 1) * LANE)
        w_b = [jnp.broadcast_to(wdw_ref[j:j + 1, cs], (SUBLANES, LANE)) for j in range(CONV_WIDTH)]
        bias = jnp.broadcast_to(bdw_ref[:, cs], (SUBLANES, LANE))

        def partials(v):
            tiles = [buf_ref[pl.ds(v + SUBLANES * q, SUBLANES), cs] for q in range(n_q)]
            out = []
            for s in range(SUBLANES):
                a = None
                for q in range(n_q):
                    j = SUBLANES * q + s - CONV_BASE
                    if 0 <= j < CONV_WIDTH:
                        term = w_b[j] * tiles[q]
                        a = term if a is None else a + term
                out.append(a)
            return tuple(out)

        def row_body(t, prev):
            r = pl.multiple_of(t * SUBLANES, SUBLANES)
            cur = partials(pl.multiple_of(r + SUBLANES, SUBLANES))
            acc = bias + prev[0]
            for s in range(1, SUBLANES):
                mixed = jnp.where(sub >= s, prev[s], cur[s])
                acc = acc + pltpu.roll(mixed, SUBLANES - s, axis=0)
            cv_ref[pl.ds(r, SUBLANES), cs] = acc
            return cur

        lax.fori_loop(0, TM_CONV // SUBLANES, row_body, partials(0), unroll=2)
    h = _layer_norm(cv_ref[...], lng_ref[...], lnb_ref[...])
    h = h * _sigmoid(h)
    mix = jnp.dot(h.astype(BF16), w2_ref[...], preferred_element_type=F32) + b2_ref[...]
    o_ref[...] = _layer_norm(ALPHA * x_ref[...] + mix, mg_ref[...], mb_ref[...])


def _conv_back(g, x, wdw, bdw, lng, lnb, w2, b2, mg, mb, wg, wu, wd, cast_layers):
    assert len(cast_layers) == N_CAST_PER_CONV
    row = lambda i: (i, 0)
    fixed = lambda i: (0, 0)
    vec = pl.BlockSpec((1, D_MODEL), fixed)
    halo_blocks = TM_CONV // CONV_HALO
    up_rows, down_rows = D_MODEL // W_PARTS, D_EXPERT // W_PARTS
    w_out = lambda i: (i // W_PARTS, i % W_PARTS, 0)
    w_specs, wb_specs, wb_shapes, w_ops = [], [], [], []
    for layer in cast_layers:
        w_in = lambda i, layer=layer: (layer, i // W_PARTS, i % W_PARTS, 0)
        for w, rows, cols in ((wg, up_rows, D_EXPERT), (wu, up_rows, D_EXPERT),
                              (wd, down_rows, D_MODEL)):
            w_specs.append(pl.BlockSpec((1, 1, rows, cols), w_in))
            wb_specs.append(pl.BlockSpec((1, rows, cols), w_out))
            wb_shapes.append(jax.ShapeDtypeStruct((N_EXPERTS, rows * W_PARTS, cols), BF16))
            w_ops.append(w)
    outs = pl.pallas_call(
        _conv_kernel,
        grid=(T // TM_CONV,),
        in_specs=[
            pl.BlockSpec((TM_CONV, D_MODEL), row),
            pl.BlockSpec((CONV_HALO, D_MODEL), lambda i: (jnp.maximum(i * halo_blocks - 1, 0), 0)),
            pl.BlockSpec((TM_CONV, D_MODEL), row),
            pl.BlockSpec((CONV_HALO, D_MODEL), fixed),
            vec, vec, vec,
            pl.BlockSpec((D_MODEL, D_MODEL), fixed),
            vec, vec, vec,
        ] + w_specs,
        out_specs=[pl.BlockSpec((TM_CONV, D_MODEL), row)] + wb_specs,
        out_shape=[jax.ShapeDtypeStruct((T, D_MODEL), F32)] + wb_shapes,
        scratch_shapes=[
            pltpu.VMEM((TM_CONV + CONV_HALO + SUBLANES, D_MODEL), F32),
            pltpu.VMEM((TM_CONV, D_MODEL), F32),
        ],
        compiler_params=_params(("parallel",)),
        name="conv_back",
    )(g, g, x, wdw, bdw, lng, lnb, w2, b2, mg, mb, *w_ops)
    cast = {layer: tuple(outs[1 + 3 * n:4 + 3 * n]) for n, layer in enumerate(cast_layers)}
    return outs[0], cast


def _ranks_before(vals):
    n = len(vals)
    ranks = []
    for j in range(n):
        r = jnp.zeros_like(vals[j])
        for i in range(n):
            if i == j:
                continue
            before = (vals[i] > vals[j]) | ((vals[i] == vals[j]) & (i < j))
            r = r + before.astype(F32)
        ranks.append(r)
    return ranks


PAIRS = [(a, b) for a in range(EXPERTS_PER_GROUP) for b in range(a + 1, EXPERTS_PER_GROUP)]
N_CLASSES = N_GROUPS * len(PAIRS)
CLASS_ROWS = 32
CLASS_EXPERTS = [(g * EXPERTS_PER_GROUP + a, g * EXPERTS_PER_GROUP + b)
                 for g in range(N_GROUPS) for (a, b) in PAIRS]
TM_G = 256
NT_MAX = -(-(T + N_CLASSES * (TM_G - 1)) // TM_G)
P_MAX = NT_MAX * TM_G
XG_W = D_MODEL + LANE
GATE_ROWS = 8
TD = 256
NT_MIN = T // TM_G
N_ZERO_TILES = N_CLASSES + NT_MAX - NT_MIN


def _route_kernel(x_ref, wrt_ref, rb_ref, gw_ref, cls_ref, pos_ref, tot_ref, tri_ref, carry_ref):
    i = pl.program_id(0)

    @pl.when(i == 0)
    def _():
        r = lax.broadcasted_iota(jnp.int32, (TM_ROUTER, TM_ROUTER), 0)
        c = lax.broadcasted_iota(jnp.int32, (TM_ROUTER, TM_ROUTER), 1)
        tri_ref[...] = jnp.where(r < c, 1.0, 0.0).astype(BF16)
        carry_ref[...] = jnp.zeros_like(carry_ref)

    x = x_ref[...]
    x_hi = x.astype(BF16)
    x_mid = (x - x_hi.astype(F32)).astype(BF16)
    both = lax.dot_general(wrt_ref[...], x_hi, _NT, preferred_element_type=F32)
    logits = (both[:N_EXPERTS] + both[N_EXPERTS:]
              + lax.dot_general(wrt_ref[:N_EXPERTS, :], x_mid, _NT, preferred_element_type=F32))
    aff = _sigmoid(logits)
    sel = aff + rb_ref[...]
    aff_rows = [aff[e:e + 1, :] for e in range(N_EXPERTS)]
    sel_rows = [sel[e:e + 1, :] for e in range(N_EXPERTS)]
    in_top2 = []
    scores = []
    for g in range(N_GROUPS):
        members = sel_rows[g * EXPERTS_PER_GROUP:(g + 1) * EXPERTS_PER_GROUP]
        ranks = _ranks_before(members)
        top = [r < 2.0 for r in ranks]
        in_top2.extend(top)
        s = jnp.zeros_like(members[0])
        for v, t in zip(members, top):
            s = s + jnp.where(t, v, 0.0)
        scores.append(s)
    g_ranks = _ranks_before(scores)
    w_rows = []
    for e in range(N_EXPERTS):
        chosen = (g_ranks[e // EXPERTS_PER_GROUP] < 1.0) & in_top2[e]
        w_rows.append(jnp.where(chosen, aff_rows[e], 0.0))
    denom = w_rows[0]
    for e in range(1, N_EXPERTS):
        denom = denom + w_rows[e]
    inv = 1.0 / denom

    masks = []
    wa = jnp.zeros_like(denom)
    wb = jnp.zeros_like(denom)
    for c, (ea, eb) in enumerate(CLASS_EXPERTS):
        m = (g_ranks[c // len(PAIRS)] < 1.0) & in_top2[ea] & in_top2[eb]
        masks.append(m.astype(F32))
        wa = wa + jnp.where(m, aff_rows[ea], 0.0)
        wb = wb + jnp.where(m, aff_rows[eb], 0.0)
    zero_row = jnp.zeros_like(denom)
    onehot = jnp.concatenate(masks + [zero_row] * (CLASS_ROWS - N_CLASSES), axis=0)

    before = jnp.dot(onehot.astype(BF16), tri_ref[...], preferred_element_type=F32)
    carry = carry_ref[...]
    class_id = lax.broadcasted_iota(jnp.int32, (CLASS_ROWS, 1), 0).astype(F32)
    pos_ref[...] = jnp.sum(onehot * (before + carry), axis=0, keepdims=True).astype(jnp.int32)
    cls_ref[...] = jnp.sum(onehot * class_id, axis=0, keepdims=True).astype(jnp.int32)
    carry = carry + jnp.sum(onehot, axis=1, keepdims=True)
    carry_ref[...] = carry
    tot_ref[...] = carry

    gw_ref[...] = jnp.concatenate([wa * inv, wb * inv] + [zero_row] * (GATE_ROWS - 2), axis=0)


def _route(x, wrt, rb):
    return pl.pallas_call(
        _route_kernel,
        grid=(T // TM_ROUTER,),
        in_specs=[
            pl.BlockSpec((TM_ROUTER, D_MODEL), lambda i: (i, 0)),
            pl.BlockSpec((2 * N_EXPERTS, D_MODEL), lambda i: (0, 0)),
            pl.BlockSpec((N_EXPERTS, 1), lambda i: (0, 0)),
        ],
        out_specs=[
            pl.BlockSpec((GATE_ROWS, TM_ROUTER), lambda i: (0, i)),
            pl.BlockSpec((1, TM_ROUTER), lambda i: (0, i)),
            pl.BlockSpec((1, TM_ROUTER), lambda i: (0, i)),
            pl.BlockSpec((CLASS_ROWS, 1), lambda i: (0, 0)),
        ],
        out_shape=[
            jax.ShapeDtypeStruct((GATE_ROWS, T), F32),
            jax.ShapeDtypeStruct((1, T), jnp.int32),
            jax.ShapeDtypeStruct((1, T), jnp.int32),
            jax.ShapeDtypeStruct((CLASS_ROWS, 1), F32),
        ],
        scratch_shapes=[
            pltpu.VMEM((TM_ROUTER, TM_ROUTER), BF16),
            pltpu.VMEM((CLASS_ROWS, 1), F32),
        ],
        compiler_params=_params(("arbitrary",)),
        name="route",
    )(x, wrt, rb)


def _start_all(copies):
    for cp in copies:
        cp.start()


def _row_copies(pay, slot, dst_hbm, dest_ref, base, sem):
    return [pltpu.make_async_copy(pay.at[slot, r], dst_hbm.at[dest_ref[base + r]], sem.at[slot])
            for r in range(TD)]


def _dispatch_kernel(dest_ref, zstart_ref, x_ref, gw_ref, xs_hbm, pay, zbuf, zsem, sem):
    i = pl.program_id(0)
    last = pl.num_programs(0) - 1

    @pl.when(i == 0)
    def _():
        zbuf[...] = jnp.zeros_like(zbuf)

        def zero_tile(c):
            start = pl.multiple_of(zstart_ref[c], TM_G)
            return pltpu.make_async_copy(zbuf, xs_hbm.at[pl.ds(start, TM_G)], zsem)

        for c in range(N_ZERO_TILES):
            @pl.when(zstart_ref[c] >= 0)
            def _():
                zero_tile(c).start()
        for c in range(N_ZERO_TILES):
            @pl.when(zstart_ref[c] >= 0)
            def _():
                zero_tile(c).wait()

    for parity in range(2):
        @pl.when(i % 2 == parity)
        def _():
            pay[parity, :, :D_MODEL] = x_ref[...]
            gates = jnp.concatenate(
                [gw_ref[...], jnp.zeros((LANE - GATE_ROWS, TD), F32)], axis=0)
            pay[parity, :, D_MODEL:] = gates.T
            _start_all(_row_copies(pay, parity, xs_hbm, dest_ref, i * TD, sem))

            @pl.when(i > 0)
            def _():
                for cp in _row_copies(pay, 1 - parity, xs_hbm, dest_ref, (i - 1) * TD, sem):
                    cp.wait()

            @pl.when(i == last)
            def _():
                for cp in _row_copies(pay, parity, xs_hbm, dest_ref, i * TD, sem):
                    cp.wait()


def _dispatch(dest, zstart, x, gw):
    return pl.pallas_call(
        _dispatch_kernel,
        grid_spec=pltpu.PrefetchScalarGridSpec(
            num_scalar_prefetch=2,
            grid=(T // TD,),
            in_specs=[
                pl.BlockSpec((TD, D_MODEL), lambda i, d, z: (i, 0)),
                pl.BlockSpec((GATE_ROWS, TD), lambda i, d, z: (0, i)),
            ],
            out_specs=pl.BlockSpec(memory_space=pl.ANY),
            scratch_shapes=[
                pltpu.VMEM((2, TD, XG_W), F32),
                pltpu.VMEM((TM_G, XG_W), F32),
                pltpu.SemaphoreType.DMA,
                pltpu.SemaphoreType.DMA((2,)),
            ],
        ),
        out_shape=jax.ShapeDtypeStruct((P_MAX, XG_W), F32),
        compiler_params=_params(("arbitrary",)),
        name="dispatch",
    )(dest, zstart, x, gw)


def _experts_kernel(te_ref, nt_ref, xs_ref, wga_ref, wua_ref, wda_ref, wgb_ref, wub_ref, wdb_ref,
                    ys_ref):
    i = pl.program_id(0)

    @pl.when(i < nt_ref[0])
    def _():
        xb = xs_ref[:, :D_MODEL].astype(BF16)
        gates = xs_ref[:, D_MODEL:]
        y = None
        for which, (wg_ref, wu_ref, wd_ref) in enumerate(((wga_ref, wua_ref, wda_ref),
                                                          (wgb_ref, wub_ref, wdb_ref))):
            hg = jnp.dot(xb, wg_ref[0], preferred_element_type=F32)
            hu = jnp.dot(xb, wu_ref[0], preferred_element_type=F32)
            h = hg * _sigmoid(hg) * hu * gates[:, which:which + 1]
            part = jnp.dot(h.astype(BF16), wd_ref[0], preferred_element_type=F32)
            y = part if y is None else y + part
        ys_ref[...] = y

    @pl.when(i >= nt_ref[0])
    def _():
        ys_ref[...] = jnp.zeros_like(ys_ref)


def _experts(tile_e, n_tiles, xs, wg, wu, wd):
    def row_map(i, te, nt):
        return (jnp.minimum(i, nt[0] - 1), 0)

    def w_map(which):
        return lambda i, te, nt: (te[which * NT_MAX + i], 0, 0)

    up_spec = lambda which: pl.BlockSpec((1, D_MODEL, D_EXPERT), w_map(which))
    down_spec = lambda which: pl.BlockSpec((1, D_EXPERT, D_MODEL), w_map(which))
    return pl.pallas_call(
        _experts_kernel,
        grid_spec=pltpu.PrefetchScalarGridSpec(
            num_scalar_prefetch=2,
            grid=(NT_MAX,),
            in_specs=[
                pl.BlockSpec((TM_G, XG_W), row_map),
                up_spec(0), up_spec(0), down_spec(0),
                up_spec(1), up_spec(1), down_spec(1),
            ],
            out_specs=pl.BlockSpec((TM_G, D_MODEL), lambda i, te, nt: (i, 0)),
        ),
        out_shape=jax.ShapeDtypeStruct((P_MAX, D_MODEL), F32),
        compiler_params=_params(("arbitrary",)),
        name="experts",
    )(tile_e, n_tiles, xs, wg, wu, wd, wg, wu, wd)


def _gather_copies(ys_hbm, buf, dest_ref, base, slot, sem):
    return [pltpu.make_async_copy(ys_hbm.at[dest_ref[base + r]], buf.at[slot, r], sem.at[slot])
            for r in range(TD)]


def _combine_kernel(dest_ref, x_ref, ys_hbm, lg_ref, lb_ref, *rest, q_scale):
    if q_scale is None:
        o_ref, buf, sem = rest
    else:
        wqt_ref, o_ref, qt_ref, buf, sem = rest
    i = pl.program_id(0)
    last = pl.num_programs(0) - 1

    @pl.when(i == 0)
    def _():
        _start_all(_gather_copies(ys_hbm, buf, dest_ref, 0, 0, sem))

    for parity in range(2):
        @pl.when(i % 2 == parity)
        def _():
            @pl.when(i < last)
            def _():
                _start_all(_gather_copies(ys_hbm, buf, dest_ref, (i + 1) * TD, 1 - parity, sem))

            for cp in _gather_copies(ys_hbm, buf, dest_ref, i * TD, parity, sem):
                cp.wait()
            xn = _layer_norm(ALPHA * x_ref[...] + buf[parity], lg_ref[...], lb_ref[...])
            o_ref[...] = xn
            if q_scale is not None:
                q = lax.dot_general(wqt_ref[...], xn.astype(BF16), _NT, preferred_element_type=F32)
                qt_ref[...] = (q * q_scale).astype(BF16)


def _combine(dest, x, ys, lg, lb, wq_t=None, q_scale=None):
    vec = pl.BlockSpec((1, D_MODEL), lambda i, d: (0, 0))
    rows = pl.BlockSpec((TD, D_MODEL), lambda i, d: (i, 0))
    in_specs = [rows, pl.BlockSpec(memory_space=pl.ANY), vec, vec]
    out_specs = [rows]
    out_shape = [jax.ShapeDtypeStruct((T, D_MODEL), F32)]
    operands = [dest, x, ys, lg, lb]
    if wq_t is not None:
        in_specs.append(pl.BlockSpec((QK_WIDTH, D_MODEL), lambda i, d: (0, 0)))
        out_specs.append(pl.BlockSpec((QK_WIDTH, TD), lambda i, d: (0, i)))
        out_shape.append(jax.ShapeDtypeStruct((QK_WIDTH, T), BF16))
        operands.append(wq_t)
    outs = pl.pallas_call(
        functools.partial(_combine_kernel, q_scale=q_scale if wq_t is not None else None),
        grid_spec=pltpu.PrefetchScalarGridSpec(
            num_scalar_prefetch=1,
            grid=(T // TD,),
            in_specs=in_specs,
            out_specs=out_specs,
            scratch_shapes=[
                pltpu.VMEM((2, TD, D_MODEL), F32),
                pltpu.SemaphoreType.DMA((2,)),
            ],
        ),
        out_shape=out_shape,
        compiler_params=_params(("arbitrary",)),
        name="combine",
    )(*operands)
    return outs[0] if wq_t is None else tuple(outs)


TM_G_LOG2 = TM_G.bit_length() - 1
assert 1 << TM_G_LOG2 == TM_G


def _plan_kernel(cnt_ref, cls_ref, pos_ref, dest_ref, zstart_ref, te_ref, nt_ref):
    shr = lax.shift_right_logical
    run = jnp.int32(0)
    starts, ends = [], []
    for c in range(N_CLASSES):
        padded = shr(cnt_ref[c] + (TM_G - 1), TM_G_LOG2) * TM_G
        starts.append(run)
        run = run + padded
        ends.append(run)
        zstart_ref[c] = jnp.where(padded > 0, run - TM_G, -1)
    n_tiles = shr(run, TM_G_LOG2)
    nt_ref[0] = n_tiles
    for t in range(NT_MIN, NT_MAX):
        zstart_ref[N_CLASSES + t - NT_MIN] = jnp.where(t >= n_tiles, t * TM_G, -1)

    def tile_body(t, carry):
        row0 = jnp.minimum(t, n_tiles - 1) * TM_G
        ea = jnp.int32(0)
        eb = jnp.int32(0)
        for c in range(N_CLASSES):
            inside = (row0 >= starts[c]) & (row0 < ends[c])
            ea = jnp.where(inside, CLASS_EXPERTS[c][0], ea)
            eb = jnp.where(inside, CLASS_EXPERTS[c][1], eb)
        te_ref[t] = ea
        te_ref[NT_MAX + t] = eb
        return carry

    lax.fori_loop(0, NT_MAX, tile_body, 0)

    cls = cls_ref[...]
    dest = pos_ref[...]
    for c in range(N_CLASSES):
        dest = dest + jnp.where(cls == c, starts[c], 0)
    dest_ref[...] = dest


def _plan(counts, cls, pos):
    smem = pl.BlockSpec(memory_space=pltpu.SMEM)
    vmem = pl.BlockSpec(memory_space=pltpu.VMEM)
    return pl.pallas_call(
        _plan_kernel,
        in_specs=[smem, vmem, vmem],
        out_specs=[vmem, smem, smem, smem],
        out_shape=[
            jax.ShapeDtypeStruct((1, T), jnp.int32),
            jax.ShapeDtypeStruct((N_ZERO_TILES,), jnp.int32),
            jax.ShapeDtypeStruct((2 * NT_MAX,), jnp.int32),
            jax.ShapeDtypeStruct((1,), jnp.int32),
        ],
        name="plan",
    )(counts, cls, pos)


def _moe(x, wrt, rb, wg_b, wu_b, wd_b, lg, lb, wq_t=None, q_scale=None):
    gw, cls, pos, tot = _route(x, wrt, rb)
    dest, zstart, tile_e, n_tiles = _plan(tot[:, 0].astype(jnp.int32), cls, pos)
    dest = dest.reshape(T)
    xs = _dispatch(dest, zstart, x, gw)
    ys = _experts(tile_e, n_tiles, xs, wg_b, wu_b, wd_b)
    return _combine(dest, x, ys, lg, lb, wq_t, q_scale)


def _kv_proj_kernel(x_ref, wk_ref, wvt_ref, k_ref, vt_ref):
    xb = x_ref[...].astype(BF16)
    k_ref[...] = jnp.dot(xb, wk_ref[...], preferred_element_type=F32).astype(BF16)
    vt_ref[0] = lax.dot_general(wvt_ref[...], xb, _NT, preferred_element_type=F32).astype(BF16)


def _kv_proj(x, wk, wvt):
    return pl.pallas_call(
        _kv_proj_kernel,
        grid=(T // TK,),
        in_specs=[
            pl.BlockSpec((TK, D_MODEL), lambda i: (i, 0)),
            pl.BlockSpec((D_MODEL, QK_WIDTH), lambda i: (0, 0)),
            pl.BlockSpec((V_WIDTH, D_MODEL), lambda i: (0, 0)),
        ],
        out_specs=[
            pl.BlockSpec((TK, QK_WIDTH), lambda i: (i, 0)),
            pl.BlockSpec((1, V_WIDTH, TK), lambda i: (i, 0, 0)),
        ],
        out_shape=[
            jax.ShapeDtypeStruct((T, QK_WIDTH), BF16),
            jax.ShapeDtypeStruct((T // TK, V_WIDTH, TK), BF16),
        ],
        compiler_params=_params(("parallel",)),
        name="kv_proj",
    )(x, wk, wvt)


def _proj_ln_kernel(a_ref, x_ref, w_ref, g_ref, b_ref, o_ref):
    rows = TM_PROJ // PROJ_CHUNKS
    mixes = [jnp.dot(a_ref[c * rows:(c + 1) * rows, :], w_ref[...], preferred_element_type=F32)
             for c in range(PROJ_CHUNKS)]
    for c, mix in enumerate(mixes):
        rs = slice(c * rows, (c + 1) * rows)
        o_ref[rs, :] = _layer_norm(ALPHA * x_ref[rs, :] + mix, g_ref[...], b_ref[...])


def _proj_ln(a, x, w, g, b):
    vec = pl.BlockSpec((1, D_MODEL), lambda i: (0, 0))
    return pl.pallas_call(
        _proj_ln_kernel,
        grid=(T // TM_PROJ,),
        in_specs=[
            pl.BlockSpec((TM_PROJ, V_WIDTH), lambda i: (i, 0)),
            pl.BlockSpec((TM_PROJ, D_MODEL), lambda i: (i, 0)),
            pl.BlockSpec((V_WIDTH, D_MODEL), lambda i: (0, 0)),
            vec, vec,
        ],
        out_specs=pl.BlockSpec((TM_PROJ, D_MODEL), lambda i: (i, 0)),
        out_shape=jax.ShapeDtypeStruct((T, D_MODEL), F32),
        compiler_params=_params(("parallel",)),
        name="proj_ln",
    )(a, x, w, g, b)


assert MAX_DISTANCE <= LANE
def _bias_kernel(rb_ref, diag_ref, corner_ref):
    h = pl.program_id(0)
    max_exact = N_BUCKETS // 2
    far = rb_ref[N_BUCKETS - 1, h]

    def tile(shape, offset):
        c = lax.broadcasted_iota(jnp.int32, shape, 0)
        r = lax.broadcasted_iota(jnp.int32, shape, 1)
        rel = offset + r - c
        n = jnp.maximum(rel, 0)
        nf = jnp.maximum(n, 1).astype(F32)
        large = max_exact + (jnp.log(nf / max_exact) / math.log(MAX_DISTANCE / max_exact)
                             * (N_BUCKETS - max_exact)).astype(jnp.int32)
        large = jnp.minimum(large, N_BUCKETS - 1)
        bucket = jnp.where(n < max_exact, n, large)
        bias = jnp.zeros(shape, F32)
        for b in range(N_BUCKETS):
            bias = jnp.where(bucket == b, rb_ref[b, h] - far, bias)
        return jnp.where(rel >= 0, bias * LOG2E, NEG_BIG)

    diag_ref[0] = tile((TK, TQ), 0)
    corner_ref[0] = tile((LANE, LANE), LANE)


def _bias_tiles(rel_bias):
    return pl.pallas_call(
        _bias_kernel,
        grid=(N_HEADS,),
        in_specs=[pl.BlockSpec(memory_space=pltpu.SMEM)],
        out_specs=[
            pl.BlockSpec((1, TK, TQ), lambda h: (h, 0, 0)),
            pl.BlockSpec((1, LANE, LANE), lambda h: (h, 0, 0)),
        ],
        out_shape=[
            jax.ShapeDtypeStruct((N_HEADS, TK, TQ), F32),
            jax.ShapeDtypeStruct((N_HEADS, LANE, LANE), F32),
        ],
        compiler_params=_params(("parallel",)),
        name="bias_tiles",
    )(rel_bias)


def _attn_kernel(q1t_ref, q2t_ref, q1n_ref, q2n_ref, k1_ref, k2_ref, vt_ref, diag_ref, corner_ref,
                 lam_ref, sg_ref, o_ref, s_ref, mx_ref, m_ref, l_ref, acc_ref, *, lambda_init):
    qi = pl.program_id(2)
    dim = lax.broadcasted_iota(jnp.int32, (LANE, 1), 0)
    lo = dim < HEAD_DIM

    def head_masked(q1, q2):
        zero = jnp.zeros_like(q1)
        return [jnp.where(lo, q1, zero), jnp.where(lo, q2, zero),
                jnp.where(lo, zero, q1), jnp.where(lo, zero, q2)]

    qs = head_masked(q1t_ref[...], q2t_ref[...])
    qs_next = head_masked(q1n_ref[...], q2n_ref[...])

    def reset_stats():
        m_ref[...] = jnp.full(m_ref.shape, NEG_BIG, F32)
        l_ref[...] = jnp.zeros(l_ref.shape, F32)
        acc_ref[...] = jnp.zeros(acc_ref.shape, F32)

    def scores(j, slot, q_slots=qs):
        k0 = pl.multiple_of(j * TK, TK)
        k_ref = k1_ref if slot % 2 == 0 else k2_ref
        st = jnp.dot(k_ref[pl.ds(k0, TK), :], q_slots[slot], preferred_element_type=F32)
        s_ref[slot] = st
        mx_ref[slot] = jnp.max(st, axis=0, keepdims=True)

    def update(j, slot, kind):
        head = slot // 2
        if kind == 1:
            rows = slice(TK - LANE, TK)
            corner = s_ref[slot, rows, 0:LANE] + corner_ref[head]
            s_ref[slot, rows, 0:LANE] = corner
            mx_ref[slot, :, 0:LANE] = jnp.maximum(mx_ref[slot, :, 0:LANE],
                                                  jnp.max(corner, axis=0, keepdims=True))
        st = s_ref[slot]
        if kind == 0:
            st = st + diag_ref[head]
            mx = jnp.max(st, axis=0, keepdims=True)
        else:
            mx = mx_ref[slot]
        m_prev = m_ref[slot]
        m_new = jnp.maximum(m_prev, mx)
        alpha = jnp.exp2(m_prev - m_new)
        p = jnp.exp2(st - m_new)
        l_ref[slot] = alpha * l_ref[slot] + jnp.sum(p, axis=0, keepdims=True)
        vt = vt_ref[j, head * V_DIM:(head + 1) * V_DIM, :]
        acc_ref[slot] = alpha * acc_ref[slot] + jnp.dot(vt, p.astype(BF16),
                                                        preferred_element_type=F32)
        m_ref[slot] = m_new

    def tile(j, kind, nxt):
        for slot in range(4):
            ahead = slot + 2
            if ahead < 4:
                scores(j, ahead)
            elif nxt is None:
                scores(0, ahead - 4, qs_next)
            else:
                scores(nxt, ahead - 4)
            update(j, slot, kind)

    @pl.when(qi == 0)
    def _():
        reset_stats()
        scores(0, 0)
        scores(0, 1)

    n_far = jnp.maximum(qi - 1, 0)
    n_groups = n_far // FAR_UNROLL

    def far_group(t, carry):
        for u in range(FAR_UNROLL):
            tile(FAR_UNROLL * t + u, None, FAR_UNROLL * t + u + 1)
        return carry

    lax.fori_loop(0, n_groups, far_group, 0)

    def far_single(j, carry):
        tile(j, None, j + 1)
        return carry

    lax.fori_loop(n_groups * FAR_UNROLL, n_far, far_single, 0)

    @pl.when(qi >= 1)
    def _():
        tile(qi - 1, 1, qi)

    tile(qi, 0, None)

    lp = lam_ref[...]
    lam = (jnp.exp(jnp.sum(lp[0:1] * lp[1:2], axis=1, keepdims=True))
           - jnp.exp(jnp.sum(lp[2:3] * lp[3:4], axis=1, keepdims=True)) + lambda_init)
    for head in range(2):
        a1 = acc_ref[2 * head] / l_ref[2 * head]
        a2 = acc_ref[2 * head + 1] / l_ref[2 * head + 1]
        of = (a1 - lam * a2).T
        of = of * lax.rsqrt(jnp.mean(of * of, axis=-1, keepdims=True) + LN_EPS) * sg_ref[...]
        of = of * (1.0 - lambda_init)
        o_ref[:, head * V_DIM:(head + 1) * V_DIM] = of.astype(o_ref.dtype)
    reset_stats()


def _attention(qt, k, vt, bias_diag, bias_corner, lam_params, subln_g, lambda_init):
    nq = SEQ // TQ
    nk = SEQ // TK
    qk_blocks = QK_WIDTH // 2 // LANE
    return pl.pallas_call(
        functools.partial(_attn_kernel, lambda_init=lambda_init),
        grid=(BATCH, N_PAIRS, nq),
        in_specs=[
            pl.BlockSpec((LANE, TQ), lambda b, p, i: (p, b * nq + i)),
            pl.BlockSpec((LANE, TQ), lambda b, p, i: (qk_blocks + p, b * nq + i)),
            pl.BlockSpec((LANE, TQ), lambda b, p, i: (p, b * nq + jnp.minimum(i + 1, nq - 1))),
            pl.BlockSpec((LANE, TQ),
                         lambda b, p, i: (qk_blocks + p, b * nq + jnp.minimum(i + 1, nq - 1))),
            pl.BlockSpec((SEQ, LANE), lambda b, p, i: (b, p)),
            pl.BlockSpec((SEQ, LANE), lambda b, p, i: (b, qk_blocks + p)),
            pl.BlockSpec((nk, 2 * V_DIM, TK), lambda b, p, i: (b, p, 0)),
            pl.BlockSpec((2, TK, TQ), lambda b, p, i: (p, 0, 0)),
            pl.BlockSpec((2, LANE, LANE), lambda b, p, i: (p, 0, 0)),
            pl.BlockSpec((4, HEAD_DIM), lambda b, p, i: (0, 0)),
            pl.BlockSpec((1, V_DIM), lambda b, p, i: (0, 0)),
        ],
        out_specs=pl.BlockSpec((TQ, 2 * V_DIM), lambda b, p, i: (b * nq + i, p)),
        out_shape=jax.ShapeDtypeStruct((T, V_WIDTH), BF16),
        scratch_shapes=[
            pltpu.VMEM((4, TK, TQ), F32),
            pltpu.VMEM((4, 1, TQ), F32),
            pltpu.VMEM((4, 1, TQ), F32),
            pltpu.VMEM((4, 1, TQ), F32),
            pltpu.VMEM((4, V_DIM, TQ), F32),
        ],
        compiler_params=_params(("parallel", "parallel", "arbitrary"), vmem_mb=56),
        name="diff_attention",
    )(qt, qt, qt, qt, k, k, vt, bias_diag, bias_corner, lam_params, subln_g)


def kernel(x, a_w_pw1, a_b_pw1, a_w_dw, a_b_dw, a_ln_g, a_ln_b, a_w_pw2, a_b_pw2, w_kv, b_w_q,
           b_lambda, b_subln_g, b_w_o, rel_bias, ln_mix_g, ln_mix_b, ln_ffn_g, ln_ffn_b,
           router_w, router_bias, moe_w_gate, moe_w_up, moe_w_down):
    x = x.reshape(T, D_MODEL)
    row = lambda v: v.reshape(1, -1)
    wr_hi = router_w.T.astype(BF16)
    wr_mid = (router_w.T - wr_hi.astype(F32)).astype(BF16)
    wrt = jnp.concatenate([wr_hi, wr_mid], axis=0)
    rb = router_bias.reshape(N_EXPERTS, 1)
    k_all = vt_all = bias_diag = bias_corner = qt = None
    experts_bf16 = {}
    for l in range(DEPTH):
        if l < N_A:
            g = _glu(x, a_w_pw1[l].astype(BF16), row(a_b_pw1[l]))
            wdw = jnp.pad(a_w_dw[l], ((0, CONV_HALO - CONV_WIDTH), (0, 0)))
            x, cast = _conv_back(g, x, wdw, row(a_b_dw[l]), row(a_ln_g[l]), row(a_ln_b[l]),
                                 a_w_pw2[l].astype(BF16), row(a_b_pw2[l]),
                                 row(ln_mix_g[l]), row(ln_mix_b[l]),
                                 moe_w_gate, moe_w_up, moe_w_down, (l, l + N_A))
            experts_bf16.update(cast)
        else:
            if l == N_A:
                k_all, vt_all = _kv_proj(x, w_kv[:, :QK_WIDTH].astype(BF16),
                                         w_kv[:, QK_WIDTH:].T.astype(BF16))
                bias_diag, bias_corner = _bias_tiles(rel_bias)
            j = l - N_A
            lambda_init = 0.8 - 0.6 * math.exp(-0.3 * l)
            o = _attention(qt, k_all, vt_all, bias_diag, bias_corner, b_lambda[j],
                           row(b_subln_g[j]), lambda_init)
            x = _proj_ln(o, x, b_w_o[j].astype(BF16), row(ln_mix_g[l]), row(ln_mix_b[l]))
        moe_args = (x, wrt, rb, *experts_bf16[l], row(ln_ffn_g[l]), row(ln_ffn_b[l]))
        if N_A <= l + 1 < DEPTH:
            x, qt = _moe(*moe_args, wq_t=b_w_q[l + 1 - N_A].T.astype(BF16),
                         q_scale=HEAD_DIM ** -0.5 * LOG2E)
        else:
            x = _moe(*moe_args)
    return x.reshape(BATCH, SEQ, D_MODEL)
```

```python
import functools
import math

import jax
import jax.numpy as jnp
from jax import lax
from jax.experimental import pallas as pl
from jax.experimental.pallas import tpu as pltpu

D_MODEL = 1024
BATCH = 2
SEQ = 8192
DEPTH = 4
N_A = DEPTH // 2
CONV_WIDTH = 31
N_HEADS = 8
HEAD_DIM = 64
V_DIM = 2 * HEAD_DIM
QK_WIDTH = 2 * N_HEADS * HEAD_DIM
V_WIDTH = N_HEADS * V_DIM
N_BUCKETS = 32
MAX_DISTANCE = 128
N_EXPERTS = 16
N_GROUPS = 4
EXPERTS_PER_GROUP = N_EXPERTS // N_GROUPS
D_EXPERT = 512
ALPHA = (2.0 * DEPTH) ** 0.25
LN_EPS = 1e-5

T = BATCH * SEQ
F32 = jnp.float32
BF16 = jnp.bfloat16
LOG2E = 1.4426950408889634
NEG_BIG = -1e30
LANE = 128
SUBLANES = 8
CONV_HALO = 32
N_PAIRS = N_HEADS // 2

TM_GLU = 512
TM_CONV = 256
TM_ROUTER = 1024
TM_PROJ = 512
TQ = 512
TK = 512
FAR_UNROLL = 4
PROJ_CHUNKS = 4


def _params(sem, vmem_mb=None):
    kw = dict(dimension_semantics=sem)
    if vmem_mb is not None:
        kw["vmem_limit_bytes"] = vmem_mb * 1024 * 1024
    return pltpu.CompilerParams(**kw)


def _layer_norm(v, g, b):
    mu = jnp.mean(v, axis=-1, keepdims=True)
    d = v - mu
    var = jnp.mean(d * d, axis=-1, keepdims=True)
    return d * lax.rsqrt(var + LN_EPS) * g + b


def _sigmoid(v):
    return 1.0 / (1.0 + jnp.exp(-v))


_NT = (((1,), (1,)), ((), ()))


def _glu_kernel(x_ref, w_ref, b_ref, o_ref):
    rows = TM_GLU // PROJ_CHUNKS
    parts = []
    for c in range(PROJ_CHUNKS):
        xb = x_ref[c * rows:(c + 1) * rows, :].astype(BF16)
        a = jnp.dot(xb, w_ref[:, :D_MODEL], preferred_element_type=F32) + b_ref[:, :D_MODEL]
        gate = jnp.dot(xb, w_ref[:, D_MODEL:], preferred_element_type=F32) + b_ref[:, D_MODEL:]
        parts.append((a, gate))
    for c, (a, gate) in enumerate(parts):
        o_ref[c * rows:(c + 1) * rows, :] = a * _sigmoid(gate)


def _glu(x, w, b):
    return pl.pallas_call(
        _glu_kernel,
        grid=(T // TM_GLU,),
        in_specs=[
            pl.BlockSpec((TM_GLU, D_MODEL), lambda i: (i, 0)),
            pl.BlockSpec((D_MODEL, 2 * D_MODEL), lambda i: (0, 0)),
            pl.BlockSpec((1, 2 * D_MODEL), lambda i: (0, 0)),
        ],
        out_specs=pl.BlockSpec((TM_GLU, D_MODEL), lambda i: (i, 0)),
        out_shape=jax.ShapeDtypeStruct((T, D_MODEL), F32),
        compiler_params=_params(("parallel",)),
        name="glu_front",
    )(x, w, b)


CONV_BASE = CONV_HALO - (CONV_WIDTH - 1)
N_CAST_PER_CONV = DEPTH // N_A
W_PARTS = (T // TM_CONV) // N_EXPERTS
assert W_PARTS * N_EXPERTS == T // TM_CONV and N_CAST_PER_CONV * N_A == DEPTH


def _conv_kernel(g_ref, halo_ref, x_ref, wdw_ref, bdw_ref, lng_ref, lnb_ref, w2_ref, b2_ref,
                 mg_ref, mb_ref, *rest):
    n_w = 3 * N_CAST_PER_CONV
    w_refs, o_ref, wb_refs, (buf_ref, cv_ref) = (rest[:n_w], rest[n_w], rest[n_w + 1:2 * n_w + 1],
                                                 rest[2 * n_w + 1:])
    for w_ref, wb_ref in zip(w_refs, wb_refs):
        wb_ref[0] = w_ref[0, 0].astype(BF16)

    i = pl.program_id(0)
    first = (i % (SEQ // TM_CONV)) == 0
    buf_ref[0:CONV_HALO, :] = jnp.where(first, 0.0, halo_ref[...])
    buf_ref[CONV_HALO:CONV_HALO + TM_CONV, :] = g_ref[...]
    buf_ref[CONV_HALO + TM_CONV:, :] = jnp.zeros((SUBLANES, D_MODEL), F32)
    sub = lax.broadcasted_iota(jnp.int32, (SUBLANES, LANE), 0)
    n_q = (CONV_BASE + CONV_WIDTH - 1) // SUBLANES + 1
    for c in range(D_MODEL // LANE):
        cs = slice(c * LANE, (c + 1) * LANE)
        w_b = [jnp.broadcast_to(wdw_ref[j:j + 1, cs], (SUBLANES, LANE)) for j in range(CONV_WIDTH)]
        bias = jnp.broadcast_to(bdw_ref[:, cs], (SUBLANES, LANE))

        def partials(v):
            tiles = [buf_ref[pl.ds(v + SUBLANES * q, SUBLANES), cs] for q in range(n_q)]
            out = []
            for s in range(SUBLANES):
                a = None
                for q in range(n_q):
                    j = SUBLANES * q + s - CONV_BASE
                    if 0 <= j < CONV_WIDTH:
                        term = w_b[j] * tiles[q]
                        a = term if a is None else a + term
                out.append(a)
            return tuple(out)

        def row_body(t, prev):
            r = pl.multiple_of(t * SUBLANES, SUBLANES)
            cur = partials(pl.multiple_of(r + SUBLANES, SUBLANES))
            acc = bias + prev[0]
            for s in range(1, SUBLANES):
                mixed = jnp.where(sub >= s, prev[s], cur[s])
                acc = acc + pltpu.roll(mixed, SUBLANES - s, axis=0)
            cv_ref[pl.ds(r, SUBLANES), cs] = acc
            return cur

        lax.fori_loop(0, TM_CONV // SUBLANES, row_body, partials(0), unroll=4)
    h = _layer_norm(cv_ref[...], lng_ref[...], lnb_ref[...])
    h = h * _sigmoid(h)
    mix = jnp.dot(h.astype(BF16), w2_ref[...], preferred_element_type=F32) + b2_ref[...]
    o_ref[...] = _layer_norm(ALPHA * x_ref[...] + mix, mg_ref[...], mb_ref[...])


def _conv_back(g, x, wdw, bdw, lng, lnb, w2, b2, mg, mb, wg, wu, wd, cast_layers):
    assert len(cast_layers) == N_CAST_PER_CONV
    row = lambda i: (i, 0)
    fixed = lambda i: (0, 0)
    vec = pl.BlockSpec((1, D_MODEL), fixed)
    halo_blocks = TM_CONV // CONV_HALO
    up_rows, down_rows = D_MODEL // W_PARTS, D_EXPERT // W_PARTS
    w_out = lambda i: (i // W_PARTS, i % W_PARTS, 0)
    w_specs, wb_specs, wb_shapes, w_ops = [], [], [], []
    for layer in cast_layers:
        w_in = lambda i, layer=layer: (layer, i // W_PARTS, i % W_PARTS, 0)
        for w, rows, cols in ((wg, up_rows, D_EXPERT), (wu, up_rows, D_EXPERT),
                              (wd, down_rows, D_MODEL)):
            w_specs.append(pl.BlockSpec((1, 1, rows, cols), w_in))
            wb_specs.append(pl.BlockSpec((1, rows, cols), w_out))
            wb_shapes.append(jax.ShapeDtypeStruct((N_EXPERTS, rows * W_PARTS, cols), BF16))
            w_ops.append(w)
    outs = pl.pallas_call(
        _conv_kernel,
        grid=(T // TM_CONV,),
        in_specs=[
            pl.BlockSpec((TM_CONV, D_MODEL), row),
            pl.BlockSpec((CONV_HALO, D_MODEL), lambda i: (jnp.maximum(i * halo_blocks - 1, 0), 0)),
            pl.BlockSpec((TM_CONV, D_MODEL), row),
            pl.BlockSpec((CONV_HALO, D_MODEL), fixed),
            vec, vec, vec,
            pl.BlockSpec((D_MODEL, D_MODEL), fixed),
            vec, vec, vec,
        ] + w_specs,
        out_specs=[pl.BlockSpec((TM_CONV, D_MODEL), row)] + wb_specs,
        out_shape=[jax.ShapeDtypeStruct((T, D_MODEL), F32)] + wb_shapes,
        scratch_shapes=[
            pltpu.VMEM((TM_CONV + CONV_HALO + SUBLANES, D_MODEL), F32),
            pltpu.VMEM((TM_CONV, D_MODEL), F32),
        ],
        compiler_params=_params(("parallel",)),
        name="conv_back",
    )(g, g, x, wdw, bdw, lng, lnb, w2, b2, mg, mb, *w_ops)
    cast = {layer: tuple(outs[1 + 3 * n:4 + 3 * n]) for n, layer in enumerate(cast_layers)}
    return outs[0], cast


def _ranks_before(vals):
    n = len(vals)
    ranks = []
    for j in range(n):
        r = jnp.zeros_like(vals[j])
        for i in range(n):
            if i == j:
                continue
            before = (vals[i] > vals[j]) | ((vals[i] == vals[j]) & (i < j))
            r = r + before.astype(F32)
        ranks.append(r)
    return ranks


PAIRS = [(a, b) for a in range(EXPERTS_PER_GROUP) for b in range(a + 1, EXPERTS_PER_GROUP)]
N_CLASSES = N_GROUPS * len(PAIRS)
CLASS_ROWS = 32
CLASS_EXPERTS = [(g * EXPERTS_PER_GROUP + a, g * EXPERTS_PER_GROUP + b)
                 for g in range(N_GROUPS) for (a, b) in PAIRS]
TM_G = 256
NT_MAX = -(-(T + N_CLASSES * (TM_G - 1)) // TM_G)
P_MAX = NT_MAX * TM_G
XG_W = D_MODEL + LANE
GATE_ROWS = 8
TD = 256
NT_MIN = T // TM_G
N_ZERO_TILES = N_CLASSES + NT_MAX - NT_MIN


def _route_kernel(x_ref, wrt_ref, rb_ref, gw_ref, cls_ref, pos_ref, tot_ref, tri_ref, carry_ref):
    i = pl.program_id(0)

    @pl.when(i == 0)
    def _():
        r = lax.broadcasted_iota(jnp.int32, (TM_ROUTER, TM_ROUTER), 0)
        c = lax.broadcasted_iota(jnp.int32, (TM_ROUTER, TM_ROUTER), 1)
        tri_ref[...] = jnp.where(r < c, 1.0, 0.0).astype(BF16)
        carry_ref[...] = jnp.zeros_like(carry_ref)

    x = x_ref[...]
    x_hi = x.astype(BF16)
    x_mid = (x - x_hi.astype(F32)).astype(BF16)
    both = lax.dot_general(wrt_ref[...], x_hi, _NT, preferred_element_type=F32)
    logits = (both[:N_EXPERTS] + both[N_EXPERTS:]
              + lax.dot_general(wrt_ref[:N_EXPERTS, :], x_mid, _NT, preferred_element_type=F32))
    aff = _sigmoid(logits)
    sel = aff + rb_ref[...]
    aff_rows = [aff[e:e + 1, :] for e in range(N_EXPERTS)]
    sel_rows = [sel[e:e + 1, :] for e in range(N_EXPERTS)]
    in_top2 = []
    scores = []
    for g in range(N_GROUPS):
        members = sel_rows[g * EXPERTS_PER_GROUP:(g + 1) * EXPERTS_PER_GROUP]
        ranks = _ranks_before(members)
        top = [r < 2.0 for r in ranks]
        in_top2.extend(top)
        s = jnp.zeros_like(members[0])
        for v, t in zip(members, top):
            s = s + jnp.where(t, v, 0.0)
        scores.append(s)
    g_ranks = _ranks_before(scores)
    w_rows = []
    for e in range(N_EXPERTS):
        chosen = (g_ranks[e // EXPERTS_PER_GROUP] < 1.0) & in_top2[e]
        w_rows.append(jnp.where(chosen, aff_rows[e], 0.0))
    denom = w_rows[0]
    for e in range(1, N_EXPERTS):
        denom = denom + w_rows[e]
    inv = 1.0 / denom

    masks = []
    wa = jnp.zeros_like(denom)
    wb = jnp.zeros_like(denom)
    for c, (ea, eb) in enumerate(CLASS_EXPERTS):
        m = (g_ranks[c // len(PAIRS)] < 1.0) & in_top2[ea] & in_top2[eb]
        masks.append(m.astype(F32))
        wa = wa + jnp.where(m, aff_rows[ea], 0.0)
        wb = wb + jnp.where(m, aff_rows[eb], 0.0)
    zero_row = jnp.zeros_like(denom)
    onehot = jnp.concatenate(masks + [zero_row] * (CLASS_ROWS - N_CLASSES), axis=0)

    before = jnp.dot(onehot.astype(BF16), tri_ref[...], preferred_element_type=F32)
    carry = carry_ref[...]
    class_id = lax.broadcasted_iota(jnp.int32, (CLASS_ROWS, 1), 0).astype(F32)
    pos_ref[...] = jnp.sum(onehot * (before + carry), axis=0, keepdims=True).astype(jnp.int32)
    cls_ref[...] = jnp.sum(onehot * class_id, axis=0, keepdims=True).astype(jnp.int32)
    carry = carry + jnp.sum(onehot, axis=1, keepdims=True)
    carry_ref[...] = carry
    tot_ref[...] = carry

    gw_ref[...] = jnp.concatenate([wa * inv, wb * inv] + [zero_row] * (GATE_ROWS - 2), axis=0)


def _route(x, wrt, rb):
    return pl.pallas_call(
        _route_kernel,
        grid=(T // TM_ROUTER,),
        in_specs=[
            pl.BlockSpec((TM_ROUTER, D_MODEL), lambda i: (i, 0)),
            pl.BlockSpec((2 * N_EXPERTS, D_MODEL), lambda i: (0, 0)),
            pl.BlockSpec((N_EXPERTS, 1), lambda i: (0, 0)),
        ],
        out_specs=[
            pl.BlockSpec((GATE_ROWS, TM_ROUTER), lambda i: (0, i)),
            pl.BlockSpec((1, TM_ROUTER), lambda i: (0, i)),
            pl.BlockSpec((1, TM_ROUTER), lambda i: (0, i)),
            pl.BlockSpec((CLASS_ROWS, 1), lambda i: (0, 0)),
        ],
        out_shape=[
            jax.ShapeDtypeStruct((GATE_ROWS, T), F32),
            jax.ShapeDtypeStruct((1, T), jnp.int32),
            jax.ShapeDtypeStruct((1, T), jnp.int32),
            jax.ShapeDtypeStruct((CLASS_ROWS, 1), F32),
        ],
        scratch_shapes=[
            pltpu.VMEM((TM_ROUTER, TM_ROUTER), BF16),
            pltpu.VMEM((CLASS_ROWS, 1), F32),
        ],
        compiler_params=_params(("arbitrary",)),
        name="route",
    )(x, wrt, rb)


def _start_all(copies):
    for cp in copies:
        cp.start()


def _row_copies(pay, slot, dst_hbm, dest_ref, base, sem):
    return [pltpu.make_async_copy(pay.at[slot, r], dst_hbm.at[dest_ref[base + r]], sem.at[slot])
            for r in range(TD)]


def _dispatch_kernel(dest_ref, zstart_ref, x_ref, gw_ref, xs_hbm, pay, zbuf, zsem, sem):
    i = pl.program_id(0)
    last = pl.num_programs(0) - 1

    @pl.when(i == 0)
    def _():
        zbuf[...] = jnp.zeros_like(zbuf)

        def zero_tile(c):
            start = pl.multiple_of(zstart_ref[c], TM_G)
            return pltpu.make_async_copy(zbuf, xs_hbm.at[pl.ds(start, TM_G)], zsem)

        for c in range(N_ZERO_TILES):
            @pl.when(zstart_ref[c] >= 0)
            def _():
                zero_tile(c).start()
        for c in range(N_ZERO_TILES):
            @pl.when(zstart_ref[c] >= 0)
            def _():
                zero_tile(c).wait()

    for parity in range(2):
        @pl.when(i % 2 == parity)
        def _():
            pay[parity, :, :D_MODEL] = x_ref[...]
            gates = jnp.concatenate(
                [gw_ref[...], jnp.zeros((LANE - GATE_ROWS, TD), F32)], axis=0)
            pay[parity, :, D_MODEL:] = gates.T
            _start_all(_row_copies(pay, parity, xs_hbm, dest_ref, i * TD, sem))

            @pl.when(i > 0)
            def _():
                for cp in _row_copies(pay, 1 - parity, xs_hbm, dest_ref, (i - 1) * TD, sem):
                    cp.wait()

            @pl.when(i == last)
            def _():
                for cp in _row_copies(pay, parity, xs_hbm, dest_ref, i * TD, sem):
                    cp.wait()


def _dispatch(dest, zstart, x, gw):
    return pl.pallas_call(
        _dispatch_kernel,
        grid_spec=pltpu.PrefetchScalarGridSpec(
            num_scalar_prefetch=2,
            grid=(T // TD,),
            in_specs=[
                pl.BlockSpec((TD, D_MODEL), lambda i, d, z: (i, 0)),
                pl.BlockSpec((GATE_ROWS, TD), lambda i, d, z: (0, i)),
            ],
            out_specs=pl.BlockSpec(memory_space=pl.ANY),
            scratch_shapes=[
                pltpu.VMEM((2, TD, XG_W), F32),
                pltpu.VMEM((TM_G, XG_W), F32),
                pltpu.SemaphoreType.DMA,
                pltpu.SemaphoreType.DMA((2,)),
            ],
        ),
        out_shape=jax.ShapeDtypeStruct((P_MAX, XG_W), F32),
        compiler_params=_params(("arbitrary",)),
        name="dispatch",
    )(dest, zstart, x, gw)


def _experts_kernel(te_ref, nt_ref, xs_ref, wga_ref, wua_ref, wda_ref, wgb_ref, wub_ref, wdb_ref,
                    ys_ref):
    i = pl.program_id(0)

    @pl.when(i < nt_ref[0])
    def _():
        xb = xs_ref[:, :D_MODEL].astype(BF16)
        gates = xs_ref[:, D_MODEL:]
        y = None
        for which, (wg_ref, wu_ref, wd_ref) in enumerate(((wga_ref, wua_ref, wda_ref),
                                                          (wgb_ref, wub_ref, wdb_ref))):
            hg = jnp.dot(xb, wg_ref[0], preferred_element_type=F32)
            hu = jnp.dot(xb, wu_ref[0], preferred_element_type=F32)
            h = hg * _sigmoid(hg) * hu * gates[:, which:which + 1]
            part = jnp.dot(h.astype(BF16), wd_ref[0], preferred_element_type=F32)
            y = part if y is None else y + part
        ys_ref[...] = y

    @pl.when(i >= nt_ref[0])
    def _():
        ys_ref[...] = jnp.zeros_like(ys_ref)


def _experts(tile_e, n_tiles, xs, wg, wu, wd):
    def row_map(i, te, nt):
        return (jnp.minimum(i, nt[0] - 1), 0)

    def w_map(which):
        return lambda i, te, nt: (te[which * NT_MAX + i], 0, 0)

    up_spec = lambda which: pl.BlockSpec((1, D_MODEL, D_EXPERT), w_map(which))
    down_spec = lambda which: pl.BlockSpec((1, D_EXPERT, D_MODEL), w_map(which))
    return pl.pallas_call(
        _experts_kernel,
        grid_spec=pltpu.PrefetchScalarGridSpec(
            num_scalar_prefetch=2,
            grid=(NT_MAX,),
            in_specs=[
                pl.BlockSpec((TM_G, XG_W), row_map),
                up_spec(0), up_spec(0), down_spec(0),
                up_spec(1), up_spec(1), down_spec(1),
            ],
            out_specs=pl.BlockSpec((TM_G, D_MODEL), lambda i, te, nt: (i, 0)),
        ),
        out_shape=jax.ShapeDtypeStruct((P_MAX, D_MODEL), F32),
        compiler_params=_params(("arbitrary",)),
        name="experts",
    )(tile_e, n_tiles, xs, wg, wu, wd, wg, wu, wd)


def _gather_copies(ys_hbm, buf, dest_ref, base, slot, sem):
    return [pltpu.make_async_copy(ys_hbm.at[dest_ref[base + r]], buf.at[slot, r], sem.at[slot])
            for r in range(TD)]


def _combine_kernel(dest_ref, x_ref, ys_hbm, lg_ref, lb_ref, *rest, q_scale):
    if q_scale is None:
        o_ref, buf, sem = rest
    else:
        wqt_ref, o_ref, qt_ref, buf, sem = rest
    i = pl.program_id(0)
    last = pl.num_programs(0) - 1

    @pl.when(i == 0)
    def _():
        _start_all(_gather_copies(ys_hbm, buf, dest_ref, 0, 0, sem))

    for parity in range(2):
        @pl.when(i % 2 == parity)
        def _():
            @pl.when(i < last)
            def _():
                _start_all(_gather_copies(ys_hbm, buf, dest_ref, (i + 1) * TD, 1 - parity, sem))

            for cp in _gather_copies(ys_hbm, buf, dest_ref, i * TD, parity, sem):
                cp.wait()
            xn = _layer_norm(ALPHA * x_ref[...] + buf[parity], lg_ref[...], lb_ref[...])
            o_ref[...] = xn
            if q_scale is not None:
                q = lax.dot_general(wqt_ref[...], xn.astype(BF16), _NT, preferred_element_type=F32)
                qt_ref[...] = (q * q_scale).astype(BF16)


def _combine(dest, x, ys, lg, lb, wq_t=None, q_scale=None):
    vec = pl.BlockSpec((1, D_MODEL), lambda i, d: (0, 0))
    rows = pl.BlockSpec((TD, D_MODEL), lambda i, d: (i, 0))
    in_specs = [rows, pl.BlockSpec(memory_space=pl.ANY), vec, vec]
    out_specs = [rows]
    out_shape = [jax.ShapeDtypeStruct((T, D_MODEL), F32)]
    operands = [dest, x, ys, lg, lb]
    if wq_t is not None:
        in_specs.append(pl.BlockSpec((QK_WIDTH, D_MODEL), lambda i, d: (0, 0)))
        out_specs.append(pl.BlockSpec((QK_WIDTH, TD), lambda i, d: (0, i)))
        out_shape.append(jax.ShapeDtypeStruct((QK_WIDTH, T), BF16))
        operands.append(wq_t)
    outs = pl.pallas_call(
        functools.partial(_combine_kernel, q_scale=q_scale if wq_t is not None else None),
        grid_spec=pltpu.PrefetchScalarGridSpec(
            num_scalar_prefetch=1,
            grid=(T // TD,),
            in_specs=in_specs,
            out_specs=out_specs,
            scratch_shapes=[
                pltpu.VMEM((2, TD, D_MODEL), F32),
                pltpu.SemaphoreType.DMA((2,)),
            ],
        ),
        out_shape=out_shape,
        compiler_params=_params(("arbitrary",)),
        name="combine",
    )(*operands)
    return outs[0] if wq_t is None else tuple(outs)


TM_G_LOG2 = TM_G.bit_length() - 1
assert 1 << TM_G_LOG2 == TM_G


def _plan_kernel(cnt_ref, cls_ref, pos_ref, dest_ref, zstart_ref, te_ref, nt_ref):
    shr = lax.shift_right_logical
    run = jnp.int32(0)
    starts, ends = [], []
    for c in range(N_CLASSES):
        padded = shr(cnt_ref[c] + (TM_G - 1), TM_G_LOG2) * TM_G
        starts.append(run)
        run = run + padded
        ends.append(run)
        zstart_ref[c] = jnp.where(padded > 0, run - TM_G, -1)
    n_tiles = shr(run, TM_G_LOG2)
    nt_ref[0] = n_tiles
    for t in range(NT_MIN, NT_MAX):
        zstart_ref[N_CLASSES + t - NT_MIN] = jnp.where(t >= n_tiles, t * TM_G, -1)

    def tile_body(t, carry):
        row0 = jnp.minimum(t, n_tiles - 1) * TM_G
        ea = jnp.int32(0)
        eb = jnp.int32(0)
        for c in range(N_CLASSES):
            inside = (row0 >= starts[c]) & (row0 < ends[c])
            ea = jnp.where(inside, CLASS_EXPERTS[c][0], ea)
            eb = jnp.where(inside, CLASS_EXPERTS[c][1], eb)
        te_ref[t] = ea
        te_ref[NT_MAX + t] = eb
        return carry

    lax.fori_loop(0, NT_MAX, tile_body, 0)

    cls = cls_ref[...]
    dest = pos_ref[...]
    for c in range(N_CLASSES):
        dest = dest + jnp.where(cls == c, starts[c], 0)
    dest_ref[...] = dest


def _plan(counts, cls, pos):
    smem = pl.BlockSpec(memory_space=pltpu.SMEM)
    vmem = pl.BlockSpec(memory_space=pltpu.VMEM)
    return pl.pallas_call(
        _plan_kernel,
        in_specs=[smem, vmem, vmem],
        out_specs=[vmem, smem, smem, smem],
        out_shape=[
            jax.ShapeDtypeStruct((1, T), jnp.int32),
            jax.ShapeDtypeStruct((N_ZERO_TILES,), jnp.int32),
            jax.ShapeDtypeStruct((2 * NT_MAX,), jnp.int32),
            jax.ShapeDtypeStruct((1,), jnp.int32),
        ],
        name="plan",
    )(counts, cls, pos)


def _moe(x, wrt, rb, wg_b, wu_b, wd_b, lg, lb, wq_t=None, q_scale=None):
    gw, cls, pos, tot = _route(x, wrt, rb)
    dest, zstart, tile_e, n_tiles = _plan(tot[:, 0].astype(jnp.int32), cls, pos)
    dest = dest.reshape(T)
    xs = _dispatch(dest, zstart, x, gw)
    ys = _experts(tile_e, n_tiles, xs, wg_b, wu_b, wd_b)
    return _combine(dest, x, ys, lg, lb, wq_t, q_scale)


def _kv_proj_kernel(x_ref, wk_ref, wvt_ref, k_ref, vt_ref):
    xb = x_ref[...].astype(BF16)
    k_ref[...] = jnp.dot(xb, wk_ref[...], preferred_element_type=F32).astype(BF16)
    vt_ref[0] = lax.dot_general(wvt_ref[...], xb, _NT, preferred_element_type=F32).astype(BF16)


def _kv_proj(x, wk, wvt):
    return pl.pallas_call(
        _kv_proj_kernel,
        grid=(T // TK,),
        in_specs=[
            pl.BlockSpec((TK, D_MODEL), lambda i: (i, 0)),
            pl.BlockSpec((D_MODEL, QK_WIDTH), lambda i: (0, 0)),
            pl.BlockSpec((V_WIDTH, D_MODEL), lambda i: (0, 0)),
        ],
        out_specs=[
            pl.BlockSpec((TK, QK_WIDTH), lambda i: (i, 0)),
            pl.BlockSpec((1, V_WIDTH, TK), lambda i: (i, 0, 0)),
        ],
        out_shape=[
            jax.ShapeDtypeStruct((T, QK_WIDTH), BF16),
            jax.ShapeDtypeStruct((T // TK, V_WIDTH, TK), BF16),
        ],
        compiler_params=_params(("parallel",)),
        name="kv_proj",
    )(x, wk, wvt)


def _proj_ln_kernel(a_ref, x_ref, w_ref, g_ref, b_ref, o_ref):
    rows = TM_PROJ // PROJ_CHUNKS
    mixes = [jnp.dot(a_ref[c * rows:(c + 1) * rows, :], w_ref[...], preferred_element_type=F32)
             for c in range(PROJ_CHUNKS)]
    for c, mix in enumerate(mixes):
        rs = slice(c * rows, (c + 1) * rows)
        o_ref[rs, :] = _layer_norm(ALPHA * x_ref[rs, :] + mix, g_ref[...], b_ref[...])


def _proj_ln(a, x, w, g, b):
    vec = pl.BlockSpec((1, D_MODEL), lambda i: (0, 0))
    return pl.pallas_call(
        _proj_ln_kernel,
        grid=(T // TM_PROJ,),
        in_specs=[
            pl.BlockSpec((TM_PROJ, V_WIDTH), lambda i: (i, 0)),
            pl.BlockSpec((TM_PROJ, D_MODEL), lambda i: (i, 0)),
            pl.BlockSpec((V_WIDTH, D_MODEL), lambda i: (0, 0)),
            vec, vec,
        ],
        out_specs=pl.BlockSpec((TM_PROJ, D_MODEL), lambda i: (i, 0)),
        out_shape=jax.ShapeDtypeStruct((T, D_MODEL), F32),
        compiler_params=_params(("parallel",)),
        name="proj_ln",
    )(a, x, w, g, b)


assert MAX_DISTANCE <= LANE
def _bias_kernel(rb_ref, diag_ref, corner_ref):
    h = pl.program_id(0)
    max_exact = N_BUCKETS // 2
    far = rb_ref[N_BUCKETS - 1, h]

    def tile(shape, offset):
        c = lax.broadcasted_iota(jnp.int32, shape, 0)
        r = lax.broadcasted_iota(jnp.int32, shape, 1)
        rel = offset + r - c
        n = jnp.maximum(rel, 0)
        nf = jnp.maximum(n, 1).astype(F32)
        large = max_exact + (jnp.log(nf / max_exact) / math.log(MAX_DISTANCE / max_exact)
                             * (N_BUCKETS - max_exact)).astype(jnp.int32)
        large = jnp.minimum(large, N_BUCKETS - 1)
        bucket = jnp.where(n < max_exact, n, large)
        bias = jnp.zeros(shape, F32)
        for b in range(N_BUCKETS):
            bias = jnp.where(bucket == b, rb_ref[b, h] - far, bias)
        return jnp.where(rel >= 0, bias * LOG2E, NEG_BIG)

    diag_ref[0] = tile((TK, TQ), 0)
    corner_ref[0] = tile((LANE, LANE), LANE)


def _bias_tiles(rel_bias):
    return pl.pallas_call(
        _bias_kernel,
        grid=(N_HEADS,),
        in_specs=[pl.BlockSpec(memory_space=pltpu.SMEM)],
        out_specs=[
            pl.BlockSpec((1, TK, TQ), lambda h: (h, 0, 0)),
            pl.BlockSpec((1, LANE, LANE), lambda h: (h, 0, 0)),
        ],
        out_shape=[
            jax.ShapeDtypeStruct((N_HEADS, TK, TQ), F32),
            jax.ShapeDtypeStruct((N_HEADS, LANE, LANE), F32),
        ],
        compiler_params=_params(("parallel",)),
        name="bias_tiles",
    )(rel_bias)


def _attn_kernel(q1t_ref, q2t_ref, q1n_ref, q2n_ref, k1_ref, k2_ref, vt_ref, diag_ref, corner_ref,
                 lam_ref, sg_ref, o_ref, s_ref, mx_ref, m_ref, l_ref, acc_ref, *, lambda_init):
    qi = pl.program_id(2)
    dim = lax.broadcasted_iota(jnp.int32, (LANE, 1), 0)
    lo = dim < HEAD_DIM

    def head_masked(q1, q2):
        zero = jnp.zeros_like(q1)
        return [jnp.where(lo, q1, zero), jnp.where(lo, q2, zero),
                jnp.where(lo, zero, q1), jnp.where(lo, zero, q2)]

    qs = head_masked(q1t_ref[...], q2t_ref[...])
    qs_next = head_masked(q1n_ref[...], q2n_ref[...])

    def reset_stats():
        m_ref[...] = jnp.full(m_ref.shape, NEG_BIG, F32)
        l_ref[...] = jnp.zeros(l_ref.shape, F32)
        acc_ref[...] = jnp.zeros(acc_ref.shape, F32)

    def scores(j, slot, q_slots=qs):
        k0 = pl.multiple_of(j * TK, TK)
        k_ref = k1_ref if slot % 2 == 0 else k2_ref
        st = jnp.dot(k_ref[pl.ds(k0, TK), :], q_slots[slot], preferred_element_type=F32)
        s_ref[slot] = st
        mx_ref[slot] = jnp.max(st, axis=0, keepdims=True)

    def update(j, slot, kind):
        head = slot // 2
        if kind == 1:
            rows = slice(TK - LANE, TK)
            corner = s_ref[slot, rows, 0:LANE] + corner_ref[head]
            s_ref[slot, rows, 0:LANE] = corner
            mx_ref[slot, :, 0:LANE] = jnp.maximum(mx_ref[slot, :, 0:LANE],
                                                  jnp.max(corner, axis=0, keepdims=True))
        st = s_ref[slot]
        if kind == 0:
            st = st + diag_ref[head]
            mx = jnp.max(st, axis=0, keepdims=True)
        else:
            mx = mx_ref[slot]
        m_prev = m_ref[slot]
        m_new = jnp.maximum(m_prev, mx)
        alpha = jnp.exp2(m_prev - m_new)
        p = jnp.exp2(st - m_new)
        l_ref[slot] = alpha * l_ref[slot] + jnp.sum(p, axis=0, keepdims=True)
        vt = vt_ref[j, head * V_DIM:(head + 1) * V_DIM, :]
        acc_ref[slot] = alpha * acc_ref[slot] + jnp.dot(vt, p.astype(BF16),
                                                        preferred_element_type=F32)
        m_ref[slot] = m_new

    def tile(j, kind, nxt):
        for slot in range(4):
            ahead = slot + 2
            if ahead < 4:
                scores(j, ahead)
            elif nxt is None:
                scores(0, ahead - 4, qs_next)
            else:
                scores(nxt, ahead - 4)
            update(j, slot, kind)

    @pl.when(qi == 0)
    def _():
        reset_stats()
        scores(0, 0)
        scores(0, 1)

    n_far = jnp.maximum(qi - 1, 0)
    n_groups = n_far // FAR_UNROLL

    def far_group(t, carry):
        for u in range(FAR_UNROLL):
            tile(FAR_UNROLL * t + u, None, FAR_UNROLL * t + u + 1)
        return carry

    lax.fori_loop(0, n_groups, far_group, 0)

    def far_single(j, carry):
        tile(j, None, j + 1)
        return carry

    lax.fori_loop(n_groups * FAR_UNROLL, n_far, far_single, 0)

    @pl.when(qi >= 1)
    def _():
        tile(qi - 1, 1, qi)

    tile(qi, 0, None)

    lp = lam_ref[...]
    lam = (jnp.exp(jnp.sum(lp[0:1] * lp[1:2], axis=1, keepdims=True))
           - jnp.exp(jnp.sum(lp[2:3] * lp[3:4], axis=1, keepdims=True)) + lambda_init)
    for head in range(2):
        a1 = acc_ref[2 * head] / l_ref[2 * head]
        a2 = acc_ref[2 * head + 1] / l_ref[2 * head + 1]
        of = (a1 - lam * a2).T
        of = of * lax.rsqrt(jnp.mean(of * of, axis=-1, keepdims=True) + LN_EPS) * sg_ref[...]
        of = of * (1.0 - lambda_init)
        o_ref[:, head * V_DIM:(head + 1) * V_DIM] = of.astype(o_ref.dtype)
    reset_stats()


def _attention(qt, k, vt, bias_diag, bias_corner, lam_params, subln_g, lambda_init):
    nq = SEQ // TQ
    nk = SEQ // TK
    qk_blocks = QK_WIDTH // 2 // LANE
    return pl.pallas_call(
        functools.partial(_attn_kernel, lambda_init=lambda_init),
        grid=(BATCH, N_PAIRS, nq),
        in_specs=[
            pl.BlockSpec((LANE, TQ), lambda b, p, i: (p, b * nq + i)),
            pl.BlockSpec((LANE, TQ), lambda b, p, i: (qk_blocks + p, b * nq + i)),
            pl.BlockSpec((LANE, TQ), lambda b, p, i: (p, b * nq + jnp.minimum(i + 1, nq - 1))),
            pl.BlockSpec((LANE, TQ),
                         lambda b, p, i: (qk_blocks + p, b * nq + jnp.minimum(i + 1, nq - 1))),
            pl.BlockSpec((SEQ, LANE), lambda b, p, i: (b, p)),
            pl.BlockSpec((SEQ, LANE), lambda b, p, i: (b, qk_blocks + p)),
            pl.BlockSpec((nk, 2 * V_DIM, TK), lambda b, p, i: (b, p, 0)),
            pl.BlockSpec((2, TK, TQ), lambda b, p, i: (p, 0, 0)),
            pl.BlockSpec((2, LANE, LANE), lambda b, p, i: (p, 0, 0)),
            pl.BlockSpec((4, HEAD_DIM), lambda b, p, i: (0, 0)),
            pl.BlockSpec((1, V_DIM), lambda b, p, i: (0, 0)),
        ],
        out_specs=pl.BlockSpec((TQ, 2 * V_DIM), lambda b, p, i: (b * nq + i, p)),
        out_shape=jax.ShapeDtypeStruct((T, V_WIDTH), BF16),
        scratch_shapes=[
            pltpu.VMEM((4, TK, TQ), F32),
            pltpu.VMEM((4, 1, TQ), F32),
            pltpu.VMEM((4, 1, TQ), F32),
            pltpu.VMEM((4, 1, TQ), F32),
            pltpu.VMEM((4, V_DIM, TQ), F32),
        ],
        compiler_params=_params(("parallel", "parallel", "arbitrary"), vmem_mb=56),
        name="diff_attention",
    )(qt, qt, qt, qt, k, k, vt, bias_diag, bias_corner, lam_params, subln_g)


def kernel(x, a_w_pw1, a_b_pw1, a_w_dw, a_b_dw, a_ln_g, a_ln_b, a_w_pw2, a_b_pw2, w_kv, b_w_q,
           b_lambda, b_subln_g, b_w_o, rel_bias, ln_mix_g, ln_mix_b, ln_ffn_g, ln_ffn_b,
           router_w, router_bias, moe_w_gate, moe_w_up, moe_w_down):
    x = x.reshape(T, D_MODEL)
    row = lambda v: v.reshape(1, -1)
    wr_hi = router_w.T.astype(BF16)
    wr_mid = (router_w.T - wr_hi.astype(F32)).astype(BF16)
    wrt = jnp.concatenate([wr_hi, wr_mid], axis=0)
    rb = router_bias.reshape(N_EXPERTS, 1)
    k_all = vt_all = bias_diag = bias_corner = qt = None
    experts_bf16 = {}
    for l in range(DEPTH):
        if l < N_A:
            g = _glu(x, a_w_pw1[l].astype(BF16), row(a_b_pw1[l]))
            wdw = jnp.pad(a_w_dw[l], ((0, CONV_HALO - CONV_WIDTH), (0, 0)))
            x, cast = _conv_back(g, x, wdw, row(a_b_dw[l]), row(a_ln_g[l]), row(a_ln_b[l]),
                                 a_w_pw2[l].astype(BF16), row(a_b_pw2[l]),
                                 row(ln_mix_g[l]), row(ln_mix_b[l]),
                                 moe_w_gate, moe_w_up, moe_w_down, (l, l + N_A))
            experts_bf16.update(cast)
        else:
            if l == N_A:
                k_all, vt_all = _kv_proj(x, w_kv[:, :QK_WIDTH].astype(BF16),
                                         w_kv[:, QK_WIDTH:].T.astype(BF16))
                bias_diag, bias_corner = _bias_tiles(rel_bias)
            j = l - N_A
            lambda_init = 0.8 - 0.6 * math.exp(-0.3 * l)
            o = _attention(qt, k_all, vt_all, bias_diag, bias_corner, b_lambda[j],
                           row(b_subln_g[j]), lambda_init)
            x = _proj_ln(o, x, b_w_o[j].astype(BF16), row(ln_mix_g[l]), row(ln_mix_b[l]))
        moe_args = (x, wrt, rb, *experts_bf16[l], row(ln_ffn_g[l]), row(ln_ffn_b[l]))
        if N_A <= l + 1 < DEPTH:
            x, qt = _moe(*moe_args, wq_t=b_w_q[l + 1 - N_A].T.astype(BF16),
                         q_scale=HEAD_DIM ** -0.5 * LOG2E)
        else:
            x = _moe(*moe_args)
    return x.reshape(BATCH, SEQ, D_MODEL)
```
